```python
import jax, jax.numpy as jnp
from jax import lax
import numpy as np

D_MODEL = 1024
BATCH = 8
SEQ = 4096
DEPTH = 4

N_META = 16
H_A = 4
QK_NOPE = 128
QK_ROPE = 64
V_DIM = 128
Q_LORA = 256
KV_LORA = 128
ROPE_THETA = 10000.0
Q_BLOCK = 128
H_B = 8
N_B = 64
RW = H_B * N_B
DECAY_LORA = 64
AAA_LORA = 64
GATE_LORA = 128
VRES_LORA = 32
RWKV_GN_EPS = 64e-5
MIX = H_A * V_DIM + RW
MLA_COLS = Q_LORA + KV_LORA + QK_ROPE
RWKV_COLS = 3 * RW + 2 * DECAY_LORA + 2 * AAA_LORA + GATE_LORA
IN_COLS = MLA_COLS + RWKV_COLS
N_GROUPS = 4
EXPERTS_PER_GROUP = 8
N_EXPERTS = N_GROUPS * EXPERTS_PER_GROUP
TOP_K = 2
E_HID = 256
MOE_BLOCK = 128
DN_ALPHA = (2 * DEPTH) ** 0.25
DN_BETA = (8 * DEPTH) ** -0.25
LN_EPS = 1e-5
RMS_EPS = 1e-6

kernel_name = 'hymba_mla_rwkv7_hmoe_deepnorm_encoder'


def _split(t, sizes):
    return jnp.split(t, np.cumsum(sizes)[:-1].tolist(), axis=-1)


def _layer_norm(x, g, b):
    xf = x.astype(jnp.float32)
    mu = jnp.mean(xf, -1, keepdims=True)
    var = jnp.mean(jnp.square(xf - mu), -1, keepdims=True)
    return ((xf - mu) * lax.rsqrt(var + LN_EPS) * g + b).astype(x.dtype)


def _rms_norm(x, g):
    xf = x.astype(jnp.float32)
    return (xf * lax.rsqrt(jnp.mean(xf * xf, -1, keepdims=True) + RMS_EPS) * g).astype(x.dtype)


def _rope(t, cos, sin):
    half = t.shape[-1] // 2
    t1, t2 = t[..., :half], t[..., half:]
    return jnp.concatenate([t1 * cos - t2 * sin, t2 * cos + t1 * sin], -1).astype(t.dtype)


def _centred_shift(p, mu_prev, mu_next):
    prev = jnp.pad(p[:, :-1], ((0, 0), (1, 0), (0, 0)))
    nxt = jnp.pad(p[:, 1:], ((0, 0), (0, 1), (0, 0)))
    return p + mu_prev * (prev - p) + mu_next * (nxt - p)


def _mla(c_q, c_kv, k_pe, cos, sin, q_norm, kv_norm, w_uq, w_ukv, out_norm):
    B, L, _ = c_q.shape
    q = (_rms_norm(c_q, q_norm) @ w_uq).reshape(B, L, H_A, QK_NOPE + QK_ROPE)
    q_nope = q[..., :QK_NOPE]
    q_pe = _rope(q[..., QK_NOPE:], cos[:, :, None, :], sin[:, :, None, :])
    kv = (_rms_norm(c_kv, kv_norm) @ w_ukv).reshape(B, L, H_A, QK_NOPE + V_DIM)
    k_nope, v = kv[..., :QK_NOPE], kv[..., QK_NOPE:]
    k_pe = _rope(k_pe, cos, sin)
    scale = (QK_NOPE + QK_ROPE) ** -0.5

    def attend(blk):
        qn, qp = blk
        s = jnp.einsum('bqhd,bkhd->bhqk', qn, k_nope) + jnp.einsum('bqhd,bkd->bhqk', qp, k_pe)
        p = jax.nn.softmax(s.astype(jnp.float32) * scale, axis=-1).astype(v.dtype)
        return jnp.einsum('bhqk,bkhd->bqhd', p, v)

    out_meta = attend((q_nope[:, :N_META], q_pe[:, :N_META]))
    n_blk = (L - N_META) // Q_BLOCK

    def to_blocks(t):
        return jnp.moveaxis(t[:, N_META:].reshape(B, n_blk, Q_BLOCK, H_A, t.shape[-1]), 1, 0)

    out_real = lax.map(attend, (to_blocks(q_nope), to_blocks(q_pe)))
    out_real = jnp.moveaxis(out_real, 0, 1).reshape(B, L - N_META, H_A, V_DIM)
    out = jnp.concatenate([out_meta, out_real], axis=1)
    out = _rms_norm(out, out_norm.reshape(H_A, V_DIM))
    return out.reshape(B, L, H_A * V_DIM)


def _wkv_scan(r, decay, k, v, kk, a, reverse):
    B = r.shape[0]

    def step(S, inp):
        r_t, w_t, k_t, v_t, kk_t, b_t = inp
        sa = jnp.einsum('bhvk,bhk->bhv', S, kk_t)
        S = S * w_t[:, :, None, :] - sa[..., None] * b_t[:, :, None, :] + v_t[..., None] * k_t[:, :, None, :]
        return S, jnp.einsum('bhvk,bhk->bhv', S, r_t)

    xs = tuple(jnp.moveaxis(t, 1, 0) for t in (r, decay, k, v, kk, kk * a))
    S0 = jnp.zeros((B, H_B, N_B, N_B), jnp.float32)
    _, ys = lax.scan(step, S0, xs, reverse=reverse)
    return jnp.moveaxis(ys, 0, 1)


def _rwkv_bidir(u, v_first, vres, w0, w2, a0, a2, g2, k_k, k_a, r_k, gn_g, gn_b):
    dt = u.dtype
    u = u.astype(jnp.float32)
    B, L, _ = u.shape
    r, k, v, wd_f, wd_b, ad_f, ad_b, gd = _split(
        u, [RW, RW, RW, DECAY_LORA, DECAY_LORA, AAA_LORA, AAA_LORA, GATE_LORA])
    if vres is None:
        v_first = v
    else:
        v0, v1, v2 = vres
        v = v + (v_first - v) * jax.nn.sigmoid(v0 + (v @ v1) @ v2)
    g = jax.nn.sigmoid(gd) @ g2

    def heads(t):
        return t.reshape(B, L, H_B, N_B)

    kk = heads(k * k_k)
    kk = kk * lax.rsqrt(jnp.sum(kk * kk, -1, keepdims=True) + 1e-12)
    rh, vh = heads(r), heads(v)

    def direction(d, wd, ad, reverse):
        wl = w0[d] + jnp.tanh(wd) @ w2[d]
        decay = jnp.exp(-jnp.exp(-jax.nn.softplus(-wl) - 0.5))
        a = jax.nn.sigmoid(a0[d] + ad @ a2[d])
        kd = heads(k * (1.0 + (a - 1.0) * k_a))
        y = _wkv_scan(rh, heads(decay), kd, vh, kk, heads(a), reverse)
        bonus = jnp.sum(rh * kd * r_k, -1, keepdims=True) * vh
        return y, bonus

    y_f, bonus_f = direction(0, wd_f, ad_f, False)
    y_b, bonus_b = direction(1, wd_b, ad_b, True)
    y = y_f + y_b
    mu = jnp.mean(y, -1, keepdims=True)
    var = jnp.mean(jnp.square(y - mu), -1, keepdims=True)
    yn = ((y - mu) * lax.rsqrt(var + RWKV_GN_EPS)).reshape(B, L, RW) * gn_g + gn_b
    out = (yn + (bonus_f + bonus_b).reshape(B, L, RW)) * g
    return out.astype(dt), v_first


def _hier_moe(h, w_group, b_group, w_expert, b_expert, w1, w3, w2):
    T, D = h.shape
    g_prob = jax.nn.softmax((h @ w_group).astype(jnp.float32) + b_group, axis=-1)
    g_top_p, g_top = lax.top_k(g_prob, 1)
    e_logits = ((h @ w_expert).astype(jnp.float32) + b_expert).reshape(T, N_GROUPS, EXPERTS_PER_GROUP)
    e_logits = jnp.take_along_axis(e_logits, g_top[:, :, None], axis=1)[:, 0]
    e_top_p, e_top = lax.top_k(jax.nn.softmax(e_logits, axis=-1), TOP_K)
    gate = (g_top_p * e_top_p).reshape(-1)
    flat_e = (g_top * EXPERTS_PER_GROUP + e_top).reshape(-1)
    flat_tok = jnp.repeat(jnp.arange(T, dtype=jnp.int32), TOP_K)
    order = jnp.argsort(flat_e)
    se = flat_e[order]
    counts = jnp.bincount(flat_e, length=N_EXPERTS)
    starts = jnp.cumsum(counts) - counts
    padded = (counts + MOE_BLOCK - 1) // MOE_BLOCK * MOE_BLOCK
    pends = jnp.cumsum(padded)
    dest = (pends - padded)[se] + jnp.arange(T * TOP_K) - starts[se]
    n_rows = (T * TOP_K + N_EXPERTS * (MOE_BLOCK - 1) + MOE_BLOCK - 1) // MOE_BLOCK * MOE_BLOCK
    n_blocks = n_rows // MOE_BLOCK
    row_tok = jnp.zeros((n_rows,), jnp.int32).at[dest].set(flat_tok[order])
    row_gate = jnp.zeros((n_rows,), jnp.float32).at[dest].set(gate[order])
    block_e = jnp.clip(jnp.searchsorted(pends, jnp.arange(n_blocks) * MOE_BLOCK, side='right'),
                       0, N_EXPERTS - 1)
    xb = h[row_tok].reshape(n_blocks, MOE_BLOCK, D)

    def expert_block(args):
        xe, e = args
        return (jax.nn.silu(xe @ w1[e]) * (xe @ w3[e])) @ w2[e]

    yb = lax.map(expert_block, (xb, block_e)).reshape(n_rows, D)
    y = jnp.zeros((T, D), jnp.float32).at[row_tok].add(yb.astype(jnp.float32) * row_gate[:, None])
    return y.astype(h.dtype)


def setup_inputs(seed: int = 0) -> dict:
    key = jax.random.key(seed)
    ks = iter(jax.random.split(key, 48))

    def nrm(shape, s):
        return jax.random.normal(next(ks), shape, jnp.float32) * s

    def uni(shape, lo, hi):
        return jax.random.uniform(next(ks), shape, jnp.float32, lo, hi)

    D = D_MODEL
    return {
        'x': nrm((BATCH, SEQ, D), 1.0),
        'positions': jnp.tile(jnp.arange(SEQ, dtype=jnp.int32)[None], (BATCH, 1)),
        'meta_tokens': nrm((N_META, D), 1.0),
        'emb_ln_g': 1.0 + nrm((D,), 0.02),
        'emb_ln_b': nrm((D,), 0.02),
        'w_in': nrm((DEPTH, D, IN_COLS), D ** -0.5),
        'mla_q_norm': 1.0 + nrm((DEPTH, Q_LORA), 0.02),
        'mla_kv_norm': 1.0 + nrm((DEPTH, KV_LORA), 0.02),
        'mla_w_uq': nrm((DEPTH, Q_LORA, H_A * (QK_NOPE + QK_ROPE)), Q_LORA ** -0.5),
        'mla_w_ukv': nrm((DEPTH, KV_LORA, H_A * (QK_NOPE + V_DIM)), KV_LORA ** -0.5),
        'mla_out_norm': 1.0 + nrm((DEPTH, H_A * V_DIM), 0.02),
        'rwkv_mu_prev': uni((DEPTH, RWKV_COLS), 0.0, 0.5),
        'rwkv_mu_next': uni((DEPTH, RWKV_COLS), 0.0, 0.5),
        'rwkv_w0': uni((DEPTH, 2, RW), -4.0, 0.0),
        'rwkv_w2': nrm((DEPTH, 2, DECAY_LORA, RW), 0.1),
        'rwkv_a0': nrm((DEPTH, 2, RW), 0.3),
        'rwkv_a2': nrm((DEPTH, 2, AAA_LORA, RW), 0.1),
        'rwkv_g2': nrm((DEPTH, GATE_LORA, RW), GATE_LORA ** -0.5),
        'rwkv_k_k': 0.85 + nrm((DEPTH, RW), 0.05),
        'rwkv_k_a': 1.0 + nrm((DEPTH, RW), 0.05),
        'rwkv_r_k': nrm((DEPTH, H_B, N_B), 0.1),
        'rwkv_gn_g': 1.0 + nrm((DEPTH, RW), 0.02),
        'rwkv_gn_b': nrm((DEPTH, RW), 0.02),
        'rwkv_v0': nrm((DEPTH - 1, RW), 0.3),
        'rwkv_v1': nrm((DEPTH - 1, RW, VRES_LORA), RW ** -0.5),
        'rwkv_v2': nrm((DEPTH - 1, VRES_LORA, RW), 0.5 * VRES_LORA ** -0.5),
        'w_out': nrm((DEPTH, MIX, D), DN_BETA * MIX ** -0.5),
        'ln1_g': 1.0 + nrm((DEPTH, D), 0.02),
        'ln1_b': nrm((DEPTH, D), 0.02),
        'moe_w_group': nrm((DEPTH, D, N_GROUPS), D ** -0.5),
        'moe_b_group': nrm((DEPTH, N_GROUPS), 0.01),
        'moe_w_expert': nrm((DEPTH, D, N_EXPERTS), D ** -0.5),
        'moe_b_expert': nrm((DEPTH, N_EXPERTS), 0.01),
        'moe_w1': nrm((DEPTH, N_EXPERTS, D, E_HID), D ** -0.5),
        'moe_w3': nrm((DEPTH, N_EXPERTS, D, E_HID), D ** -0.5),
        'moe_w2': nrm((DEPTH, N_EXPERTS, E_HID, D), DN_BETA * E_HID ** -0.5),
        'ln2_g': 1.0 + nrm((DEPTH, D), 0.02),
        'ln2_b': nrm((DEPTH, D), 0.02),
    }


def reference(x, positions, meta_tokens, emb_ln_g, emb_ln_b, w_in, mla_q_norm, mla_kv_norm,
              mla_w_uq, mla_w_ukv, mla_out_norm, rwkv_mu_prev, rwkv_mu_next, rwkv_w0, rwkv_w2,
              rwkv_a0, rwkv_a2, rwkv_g2, rwkv_k_k, rwkv_k_a, rwkv_r_k, rwkv_gn_g, rwkv_gn_b,
              rwkv_v0, rwkv_v1, rwkv_v2, w_out, ln1_g, ln1_b, moe_w_group, moe_b_group,
              moe_w_expert, moe_b_expert, moe_w1, moe_w3, moe_w2, ln2_g, ln2_b):
    B = x.shape[0]
    meta = jnp.broadcast_to(meta_tokens.astype(x.dtype)[None], (B, N_META, D_MODEL))
    h = _layer_norm(jnp.concatenate([meta, x], axis=1), emb_ln_g, emb_ln_b)
    L = h.shape[1]
    pos = jnp.concatenate([jnp.broadcast_to(jnp.arange(N_META, dtype=jnp.int32), (B, N_META)),
                           positions + N_META], axis=1)
    inv_freq = ROPE_THETA ** (-jnp.arange(0, QK_ROPE, 2, dtype=jnp.float32) / QK_ROPE)
    ang = pos.astype(jnp.float32)[..., None] * inv_freq
    cos, sin = jnp.cos(ang), jnp.sin(ang)
    v_first = None
    for li in range(DEPTH):
        proj = h @ w_in[li]
        c_q, c_kv, k_pe, rw = _split(proj, [Q_LORA, KV_LORA, QK_ROPE, RWKV_COLS])
        y_att = _mla(c_q, c_kv, k_pe, cos, sin, mla_q_norm[li], mla_kv_norm[li],
                     mla_w_uq[li], mla_w_ukv[li], mla_out_norm[li])
        u = _centred_shift(rw, rwkv_mu_prev[li], rwkv_mu_next[li])
        vres = None if li == 0 else (rwkv_v0[li - 1], rwkv_v1[li - 1], rwkv_v2[li - 1])
        y_rwkv, v_first = _rwkv_bidir(u, v_first, vres, rwkv_w0[li], rwkv_w2[li], rwkv_a0[li],
                                      rwkv_a2[li], rwkv_g2[li], rwkv_k_k[li], rwkv_k_a[li],
                                      rwkv_r_k[li], rwkv_gn_g[li], rwkv_gn_b[li])
        mixed = jnp.concatenate([y_att, y_rwkv], axis=-1) @ w_out[li]
        h = _layer_norm(DN_ALPHA * h + mixed, ln1_g[li], ln1_b[li])
        ff = _hier_moe(h.reshape(B * L, D_MODEL), moe_w_group[li], moe_b_group[li],
                       moe_w_expert[li], moe_b_expert[li], moe_w1[li], moe_w3[li],
                       moe_w2[li]).reshape(B, L, D_MODEL)
        h = _layer_norm(DN_ALPHA * h + ff, ln2_g[li], ln2_b[li])
    return h[:, N_META:]
```

```python
import functools

import numpy as np
import jax
import jax.numpy as jnp
from jax import lax
from jax.experimental import pallas as pl
from jax.experimental.pallas import tpu as pltpu

F32 = jnp.float32
BF16 = jnp.bfloat16

N_META = 16
H_A = 4
QK_NOPE = 128
QK_ROPE = 64
V_DIM = 128
Q_LORA = 256
KV_LORA = 128
ROPE_THETA = 10000.0
H_B = 8
N_B = 64
RW = H_B * N_B
DECAY_LORA = 64
AAA_LORA = 64
GATE_LORA = 128
RWKV_GN_EPS = 64e-5
MLA_COLS = Q_LORA + KV_LORA + QK_ROPE
N_GROUPS = 4
EXPERTS_PER_GROUP = 8
N_EXPERTS = N_GROUPS * EXPERTS_PER_GROUP
TOP_K = 2
E_HID = 256
MOE_BLOCK = 128
LN_EPS = 1e-5
RMS_EPS = 1e-6

LANES = 128
SUBLANES = 8
TIME_BLOCK = 128
ROW_TILE = 512
VMEM_LIMIT = 56 * 1024 * 1024


def _cparams(sem):
    return pltpu.CompilerParams(dimension_semantics=sem, vmem_limit_bytes=VMEM_LIMIT)


def _row_tile(m):
    return next(t for t in (ROW_TILE, 384, 256, LANES) if m % t == 0)


def _mm_kernel(x_ref, w_ref, o_ref):
    o_ref[...] = jnp.dot(x_ref[...].astype(BF16), w_ref[...], preferred_element_type=F32)


def _matmul(x, w):
    M, K = x.shape
    N = w.shape[1]
    tm = _row_tile(M)
    return pl.pallas_call(
        _mm_kernel,
        out_shape=jax.ShapeDtypeStruct((M, N), F32),
        grid=(M // tm,),
        in_specs=[pl.BlockSpec((tm, K), lambda i: (i, 0)),
                  pl.BlockSpec((K, N), lambda i: (0, 0))],
        out_specs=pl.BlockSpec((tm, N), lambda i: (i, 0)),
        compiler_params=_cparams(("parallel",)),
        name="matmul",
    )(x, w.astype(BF16))


def _split_bf16(x):
    hi = x.astype(BF16)
    lo = (x - hi.astype(F32)).astype(BF16)
    return hi, lo


def _mm3_kernel(x_ref, wh_ref, wl_ref, o_ref):
    xh, xl = _split_bf16(x_ref[...])
    acc = jnp.dot(xh, wh_ref[...], preferred_element_type=F32)
    acc += jnp.dot(xl, wh_ref[...], preferred_element_type=F32)
    acc += jnp.dot(xh, wl_ref[...], preferred_element_type=F32)
    o_ref[...] = acc


def _matmul3(x, w):
    M, K = x.shape
    N = w.shape[1]
    tm = _row_tile(M)
    wh = w.astype(BF16)
    wl = (w - wh.astype(F32)).astype(BF16)
    return pl.pallas_call(
        _mm3_kernel,
        out_shape=jax.ShapeDtypeStruct((M, N), F32),
        grid=(M // tm,),
        in_specs=[pl.BlockSpec((tm, K), lambda i: (i, 0)),
                  pl.BlockSpec((K, N), lambda i: (0, 0)),
                  pl.BlockSpec((K, N), lambda i: (0, 0))],
        out_specs=pl.BlockSpec((tm, N), lambda i: (i, 0)),
        compiler_params=_cparams(("parallel",)),
        name="matmul3",
    )(x, wh, wl)


def _ln_kernel(x_ref, r_ref, g_ref, b_ref, o_ref, *, alpha):
    x = x_ref[...] * alpha + r_ref[...]
    mu = jnp.mean(x, -1, keepdims=True)
    xc = x - mu
    var = jnp.mean(xc * xc, -1, keepdims=True)
    o_ref[...] = xc * lax.rsqrt(var + LN_EPS) * g_ref[...] + b_ref[...]


def _ln_residual(x, r, g, b, alpha):
    M, D = x.shape
    tm = _row_tile(M)
    row = pl.BlockSpec((tm, D), lambda i: (i, 0))
    vec = pl.BlockSpec((1, D), lambda i: (0, 0))
    return pl.pallas_call(
        functools.partial(_ln_kernel, alpha=alpha),
        out_shape=jax.ShapeDtypeStruct((M, D), F32),
        grid=(M // tm,),
        in_specs=[row, row, vec, vec],
        out_specs=row,
        compiler_params=_cparams(("parallel",)),
        name="layer_norm",
    )(x, r, g.reshape(1, D), b.reshape(1, D))


def _attn_kernel(q_ref, k_ref, v_ref, bias_ref, g_ref, o_ref, *, scale):
    q = q_ref[0, 0]
    k = k_ref[0, 0]
    s = lax.dot_general(q, k, (((1,), (1,)), ((), ())), preferred_element_type=F32)
    s = s * scale + bias_ref[...]
    m = jnp.max(s, -1, keepdims=True)
    p = jnp.exp(s - m)
    l = jnp.sum(p, -1, keepdims=True)
    o = jnp.dot(p.astype(BF16), v_ref[0, 0], preferred_element_type=F32) / l
    o = o * lax.rsqrt(jnp.mean(o * o, -1, keepdims=True) + RMS_EPS) * g_ref[...]
    o_ref[0] = o


def _attention(q, k, v, bias, out_gain, tq):
    B, H, Lp, dqk = q.shape
    scale = float(dqk) ** -0.5
    return pl.pallas_call(
        functools.partial(_attn_kernel, scale=scale),
        out_shape=jax.ShapeDtypeStruct((B, Lp, H * V_DIM), F32),
        grid=(B, H, Lp // tq),
        in_specs=[pl.BlockSpec((1, 1, tq, dqk), lambda b, h, i: (b, h, i, 0)),
                  pl.BlockSpec((1, 1, Lp, dqk), lambda b, h, i: (b, h, 0, 0)),
                  pl.BlockSpec((1, 1, Lp, V_DIM), lambda b, h, i: (b, h, 0, 0)),
                  pl.BlockSpec((1, Lp), lambda b, h, i: (0, 0)),
                  pl.BlockSpec((1, V_DIM), lambda b, h, i: (0, h))],
        out_specs=pl.BlockSpec((1, tq, V_DIM), lambda b, h, i: (b, i, h)),
        compiler_params=_cparams(("parallel", "parallel", "parallel")),
        name="mla_attention",
    )(q, k, v, bias, out_gain.reshape(1, H * V_DIM))


def _scan_kernel(rf, kkf, wf, kdf, bf, vtf, rb, kkb, wb, kdb, bb_, vtb, ones_ref,
                 yf_ref, yb_ref, s_ref, *, nb):
    j = pl.program_id(1)
    C = nb * 4
    half_t = TIME_BLOCK // 2

    @pl.when(j == 0)
    def _():
        s_ref[...] = jnp.zeros_like(s_ref)

    yf_ref[...] = jnp.zeros_like(yf_ref)
    yb_ref[...] = jnp.zeros_like(yb_ref)

    ones = ones_ref[...]
    lane_id = lax.broadcasted_iota(jnp.int32, (C * N_B, LANES), 1)
    lane = lane_id % N_B
    head = lane_id // N_B
    row_refs = ((rf, kkf, wf, kdf, bf), (rb, kkb, wb, kdb, bb_))
    vt_refs = (vtf, vtb)
    y_refs = (yf_ref, yb_ref)

    def seg_sum(x):
        hi, lo = _split_bf16(x)
        return jnp.dot(jnp.concatenate([hi, lo], axis=-1), ones, preferred_element_type=F32)

    def step(d, tiles, row, oh, tt):
        def rows(tile):
            return jnp.concatenate(
                [jnp.broadcast_to(tile[n, row:row + 1, hp * LANES:(hp + 1) * LANES], (N_B, LANES))
                 for n in range(nb) for hp in range(4)], axis=0)

        r, kk, w, kd, b = [rows(x) for x in tiles]
        S = s_ref[d]
        sa = seg_sum(S * kk)
        vt = vt_refs[d][:, :, 0, :, oh * LANES:(oh + 1) * LANES].reshape(C * N_B, LANES)
        vb = jnp.take_along_axis(vt, 2 * tt + head, axis=1, mode="promise_in_bounds")
        S2 = S * w - sa * b + vb * kd
        s_ref[d] = S2
        y = seg_sum(S2 * r)
        yr = y_refs[d]
        cur = yr[:, :, 0, :, oh * LANES:(oh + 1) * LANES].reshape(C * N_B, LANES)
        new = jnp.where(lane == tt, y, cur)
        yr[:, :, 0, :, oh * LANES:(oh + 1) * LANES] = new.reshape(nb, 4, N_B, LANES)

    def make_body(half):
        def body(g, carry):
            t0f = pl.multiple_of(half * half_t + g * SUBLANES, SUBLANES)
            t0b = pl.multiple_of(TIME_BLOCK - SUBLANES - (half * half_t + g * SUBLANES), SUBLANES)
            tiles_f = [x[:, pl.ds(t0f, SUBLANES), :] for x in row_refs[0]]
            tiles_b = [x[:, pl.ds(t0b, SUBLANES), :] for x in row_refs[1]]
            for i in range(SUBLANES):
                step(0, tiles_f, i, half, g * SUBLANES + i)
                step(1, tiles_b, SUBLANES - 1 - i, 1 - half, half_t - 1 - (g * SUBLANES + i))
            return carry
        return body

    lax.fori_loop(0, half_t // SUBLANES, make_body(0), 0)
    lax.fori_loop(0, half_t // SUBLANES, make_body(1), 0)


def _wkv_bidir(r, kk, v, per_dir, nb):
    B, Lp, _ = r.shape
    nblk = Lp // TIME_BLOCK
    half_t = TIME_BLOCK // 2
    vt = v.reshape(B, nblk, 2, half_t, 4, 2, N_B).transpose(0, 4, 1, 6, 2, 3, 5).reshape(B, 4, nblk, N_B, 2 * LANES)
    ones = jnp.asarray((np.arange(2 * LANES)[:, None] % LANES) // N_B == (np.arange(LANES)[None, :] // N_B), BF16)
    fwd = lambda bi, j: (bi, j, 0)
    bwd = lambda bi, j: (bi, nblk - 1 - j, 0)
    fwd5 = lambda bi, j: (bi, 0, j, 0, 0)
    bwd5 = lambda bi, j: (bi, 0, nblk - 1 - j, 0, 0)
    row_blk = (nb, TIME_BLOCK, RW)
    vt_blk = (nb, 4, 1, N_B, 2 * LANES)
    (wf, kdf, bf), (wb, kdb, bb_) = per_dir
    yt_shape = jax.ShapeDtypeStruct((B, 4, nblk, N_B, 2 * LANES), F32)
    yf, yb = pl.pallas_call(
        functools.partial(_scan_kernel, nb=nb),
        out_shape=(yt_shape, yt_shape),
        grid=(B // nb, nblk),
        in_specs=[pl.BlockSpec(row_blk, fwd)] * 5 + [pl.BlockSpec(vt_blk, fwd5)]
                 + [pl.BlockSpec(row_blk, bwd)] * 5 + [pl.BlockSpec(vt_blk, bwd5)]
                 + [pl.BlockSpec(ones.shape, lambda bi, j: (0, 0))],
        out_specs=(pl.BlockSpec(vt_blk, fwd5), pl.BlockSpec(vt_blk, bwd5)),
        scratch_shapes=[pltpu.VMEM((2, nb * 4 * N_B, LANES), F32)],
        compiler_params=_cparams(("parallel", "arbitrary")),
        name="wkv_scan",
    )(r, kk, wf, kdf, bf, vt, r, kk, wb, kdb, bb_, vt, ones)

    def untranspose(yt):
        yt = yt.reshape(B, 4, nblk, N_B, 2, 2, half_t).transpose(0, 2, 4, 6, 1, 5, 3)
        return yt.reshape(B, Lp, RW)

    return untranspose(yf), untranspose(yb)


def _expert_kernel(be_ref, x_ref, g_ref, w1_ref, w3_ref, w2_ref, o_ref):
    x = x_ref[...]
    a = jnp.dot(x, w1_ref[0], preferred_element_type=F32)
    b = jnp.dot(x, w3_ref[0], preferred_element_type=F32)
    hid = (a * jax.nn.sigmoid(a)) * b
    y = jnp.dot(hid.astype(BF16), w2_ref[0], preferred_element_type=F32)
    o_ref[...] = y * g_ref[...]


def _expert_ffn(xb, row_gate, block_e, w1, w3, w2):
    n_rows, D = xb.shape
    n_blocks = n_rows // MOE_BLOCK
    grid_spec = pltpu.PrefetchScalarGridSpec(
        num_scalar_prefetch=1,
        grid=(n_blocks,),
        in_specs=[pl.BlockSpec((MOE_BLOCK, D), lambda i, be: (i, 0)),
                  pl.BlockSpec((MOE_BLOCK, 1), lambda i, be: (i, 0)),
                  pl.BlockSpec((1, D, E_HID), lambda i, be: (be[i], 0, 0)),
                  pl.BlockSpec((1, D, E_HID), lambda i, be: (be[i], 0, 0)),
                  pl.BlockSpec((1, E_HID, D), lambda i, be: (be[i], 0, 0))],
        out_specs=pl.BlockSpec((MOE_BLOCK, D), lambda i, be: (i, 0)),
    )
    return pl.pallas_call(
        _expert_kernel,
        out_shape=jax.ShapeDtypeStruct((n_rows, D), F32),
        grid_spec=grid_spec,
        compiler_params=_cparams(("arbitrary",)),
        name="moe_experts",
    )(block_e, xb, row_gate.reshape(n_rows, 1), w1, w3, w2)


def _hier_moe(h, w_router, b_group, b_expert, w1, w3, w2):
    T, D = h.shape
    logits = _matmul3(h, w_router)
    g_prob = jax.nn.softmax(logits[:, :N_GROUPS] + b_group, axis=-1)
    g_top_p, g_top = lax.top_k(g_prob, 1)
    e_logits = (logits[:, N_GROUPS:N_GROUPS + N_EXPERTS] + b_expert).reshape(T, N_GROUPS, EXPERTS_PER_GROUP)
    e_logits = jnp.take_along_axis(e_logits, g_top[:, :, None], axis=1)[:, 0]
    e_top_p, e_top = lax.top_k(jax.nn.softmax(e_logits, axis=-1), TOP_K)
    gate = (g_top_p * e_top_p).reshape(-1)
    flat_e = (g_top * EXPERTS_PER_GROUP + e_top).reshape(-1)
    flat_tok = jnp.repeat(jnp.arange(T, dtype=jnp.int32), TOP_K)
    order = jnp.argsort(flat_e)
    se = flat_e[order]
    counts = jnp.bincount(flat_e, length=N_EXPERTS)
    starts = jnp.cumsum(counts) - counts
    padded = (counts + MOE_BLOCK - 1) // MOE_BLOCK * MOE_BLOCK
    pends = jnp.cumsum(padded)
    dest = (pends - padded)[se] + jnp.arange(T * TOP_K) - starts[se]
    n_rows = (T * TOP_K + N_EXPERTS * (MOE_BLOCK - 1) + MOE_BLOCK - 1) // MOE_BLOCK * MOE_BLOCK
    n_blocks = n_rows // MOE_BLOCK
    row_tok = jnp.zeros((n_rows,), jnp.int32).at[dest].set(flat_tok[order])
    row_gate = jnp.zeros((n_rows,), F32).at[dest].set(gate[order])
    block_e = jnp.clip(jnp.searchsorted(pends, jnp.arange(n_blocks) * MOE_BLOCK, side='right'),
                       0, N_EXPERTS - 1).astype(jnp.int32)
    xb = h.astype(BF16)[row_tok]
    yb = _expert_ffn(xb, row_gate, block_e, w1, w3, w2)
    return jnp.zeros((T, D), F32).at[row_tok].add(yb)


def _rms(x, g):
    return x * lax.rsqrt(jnp.mean(x * x, -1, keepdims=True) + RMS_EPS) * g


def _rope(t, cos, sin):
    half = t.shape[-1] // 2
    t1, t2 = t[..., :half], t[..., half:]
    return jnp.concatenate([t1 * cos - t2 * sin, t2 * cos + t1 * sin], -1)


def _split_cols(t, sizes):
    return jnp.split(t, np.cumsum(sizes)[:-1].tolist(), axis=-1)


def kernel(x, positions, meta_tokens, emb_ln_g, emb_ln_b, w_in, mla_q_norm, mla_kv_norm, mla_w_uq, mla_w_ukv, mla_out_norm, rwkv_mu_prev, rwkv_mu_next, rwkv_w0, rwkv_w2, rwkv_a0, rwkv_a2, rwkv_g2, rwkv_k_k, rwkv_k_a, rwkv_r_k, rwkv_gn_g, rwkv_gn_b, rwkv_v0, rwkv_v1, rwkv_v2, w_out, ln1_g, ln1_b, moe_w_group, moe_b_group, moe_w_expert, moe_b_expert, moe_w1, moe_w3, moe_w2, ln2_g, ln2_b):
    B, seq, D = x.shape
    depth = w_in.shape[0]
    alpha = (2 * depth) ** 0.25
    L = seq + N_META
    Lp = -(-L // LANES) * LANES
    T = B * Lp
    nb = 2 if B % 2 == 0 else 1
    tq = 384 if Lp % 384 == 0 else LANES

    meta = jnp.broadcast_to(meta_tokens.astype(x.dtype)[None], (B, N_META, D))
    h = jnp.concatenate([meta, x, jnp.zeros((B, Lp - L, D), x.dtype)], axis=1).reshape(T, D)
    h = _ln_residual(h, jnp.zeros_like(h), emb_ln_g, emb_ln_b, 1.0)

    valid = (jnp.arange(Lp) < L).astype(F32)[None, :, None]
    key_bias = jnp.where(jnp.arange(Lp) < L, 0.0, -1e30).astype(F32)[None, :]
    pos = jnp.concatenate([jnp.broadcast_to(jnp.arange(N_META, dtype=jnp.int32), (B, N_META)),
                           positions + N_META, jnp.zeros((B, Lp - L), jnp.int32)], axis=1)
    inv_freq = ROPE_THETA ** (-jnp.arange(0, QK_ROPE, 2, dtype=F32) / QK_ROPE)
    ang = pos.astype(F32)[..., None] * inv_freq
    cos, sin = jnp.cos(ang), jnp.sin(ang)

    def heads(t):
        return t.reshape(B, Lp, H_B, N_B)

    v_first = None
    for li in range(depth):
        proj = _matmul(h, w_in[li]).reshape(B, Lp, -1)
        c_q, c_kv, k_pe, rw = _split_cols(proj, [Q_LORA, KV_LORA, QK_ROPE, proj.shape[-1] - MLA_COLS])

        q = _matmul(_rms(c_q, mla_q_norm[li]).reshape(T, Q_LORA), mla_w_uq[li]).reshape(B, Lp, H_A, QK_NOPE + QK_ROPE)
        q_pe = _rope(q[..., QK_NOPE:], cos[:, :, None, :], sin[:, :, None, :])
        kv = _matmul(_rms(c_kv, mla_kv_norm[li]).reshape(T, KV_LORA), mla_w_ukv[li]).reshape(B, Lp, H_A, QK_NOPE + V_DIM)
        k_pe = _rope(k_pe, cos, sin)
        qa = jnp.concatenate([q[..., :QK_NOPE], q_pe], -1).astype(BF16).transpose(0, 2, 1, 3)
        ka = jnp.concatenate([kv[..., :QK_NOPE], jnp.broadcast_to(k_pe[:, :, None, :], (B, Lp, H_A, QK_ROPE))], -1)
        ka = ka.astype(BF16).transpose(0, 2, 1, 3)
        va = kv[..., QK_NOPE:].astype(BF16).transpose(0, 2, 1, 3)
        y_att = _attention(qa, ka, va, key_bias, mla_out_norm[li], tq)

        rw = rw * valid
        prev = jnp.pad(rw[:, :-1], ((0, 0), (1, 0), (0, 0)))
        nxt = jnp.pad(rw[:, 1:], ((0, 0), (0, 1), (0, 0)))
        u = rw + rwkv_mu_prev[li] * (prev - rw) + rwkv_mu_next[li] * (nxt - rw)
        r, k, v, wd_f, wd_b, ad_f, ad_b, gd = _split_cols(
            u, [RW, RW, RW, DECAY_LORA, DECAY_LORA, AAA_LORA, AAA_LORA, GATE_LORA])
        if li == 0:
            v_first = v
        else:
            lo_rank = _matmul(v.reshape(T, RW), rwkv_v1[li - 1])
            v = v + (v_first - v) * jax.nn.sigmoid(rwkv_v0[li - 1] + _matmul(lo_rank, rwkv_v2[li - 1]).reshape(B, Lp, RW))
        g = _matmul(jax.nn.sigmoid(gd).reshape(T, GATE_LORA), rwkv_g2[li]).reshape(B, Lp, RW)
        kk = heads(k * rwkv_k_k[li])
        kk = (kk * lax.rsqrt(jnp.sum(kk * kk, -1, keepdims=True) + 1e-12)).reshape(B, Lp, RW) * valid
        vm = v * valid
        per_dir, kds = [], []
        for d, (wd, ad) in enumerate(((wd_f, ad_f), (wd_b, ad_b))):
            wl = rwkv_w0[li, d] + _matmul(jnp.tanh(wd).reshape(T, DECAY_LORA), rwkv_w2[li, d]).reshape(B, Lp, RW)
            decay = jnp.exp(-jnp.exp(-jax.nn.softplus(-wl) - 0.5))
            a = jax.nn.sigmoid(rwkv_a0[li, d] + _matmul(ad.reshape(T, AAA_LORA), rwkv_a2[li, d]).reshape(B, Lp, RW))
            kd = k * (1.0 + (a - 1.0) * rwkv_k_a[li]) * valid
            per_dir.append((decay, kd, kk * a))
            kds.append(kd)
        y_f, y_b = _wkv_bidir(r, kk, vm, per_dir, nb)
        y = heads(y_f + y_b)
        mu = jnp.mean(y, -1, keepdims=True)
        var = jnp.mean(jnp.square(y - mu), -1, keepdims=True)
        yn = ((y - mu) * lax.rsqrt(var + RWKV_GN_EPS)).reshape(B, Lp, RW) * rwkv_gn_g[li] + rwkv_gn_b[li]
        rh, vh = heads(r), heads(v)
        bonus = sum(jnp.sum(rh * heads(kd) * rwkv_r_k[li], -1, keepdims=True) * vh for kd in kds)
        y_rwkv = (yn + bonus.reshape(B, Lp, RW)) * g

        mixed = _matmul(jnp.concatenate([y_att, y_rwkv], axis=-1).reshape(T, -1), w_out[li])
        h = _ln_residual(h, mixed, ln1_g[li], ln1_b[li], alpha)

        w_router = jnp.concatenate([moe_w_group[li], moe_w_expert[li],
                                    jnp.zeros((D, LANES - N_GROUPS - N_EXPERTS), F32)], axis=1)
        ff = _hier_moe(h, w_router, moe_b_group[li], moe_b_expert[li],
                       moe_w1[li].astype(BF16), moe_w3[li].astype(BF16), moe_w2[li].astype(BF16))
        h = _ln_residual(h, ff, ln2_g[li], ln2_b[li], alpha)
    return h.reshape(B, Lp, D)[:, N_META:L]
```

```python
import functools
import math

import numpy as np
import jax
import jax.numpy as jnp
from jax import lax
from jax.experimental import pallas as pl
from jax.experimental.pallas import tpu as pltpu

F32 = jnp.float32
BF16 = jnp.bfloat16

N_META = 16
H_A = 4
QK_NOPE = 128
QK_ROPE = 64
V_DIM = 128
Q_LORA = 256
KV_LORA = 128
ROPE_THETA = 10000.0
H_B = 8
N_B = 64
RW = H_B * N_B
DECAY_LORA = 64
AAA_LORA = 64
GATE_LORA = 128
VRES_LORA = 32
RWKV_GN_EPS = 64e-5
MLA_COLS = Q_LORA + KV_LORA + QK_ROPE
N_GROUPS = 4
EXPERTS_PER_GROUP = 8
N_EXPERTS = N_GROUPS * EXPERTS_PER_GROUP
TOP_K = 2
E_HID = 256
MOE_BLOCK = 128
LN_EPS = 1e-5
RMS_EPS = 1e-6

LANES = 128
SUBLANES = 8
TIME_BLOCK = 128
ROW_TILE = 512
VMEM_LIMIT = 56 * 1024 * 1024
HEAD_W = 2 * LANES
NEG = -1e30


def _cparams(sem):
    return pltpu.CompilerParams(dimension_semantics=sem, vmem_limit_bytes=VMEM_LIMIT)


def _row_tile(m):
    return next(t for t in (ROW_TILE, 384, 256, LANES) if m % t == 0)


def _split_bf16(x):
    hi = x.astype(BF16)
    lo = (x - hi.astype(F32)).astype(BF16)
    return hi, lo


def _seg_sum(x, ones):
    hi, lo = _split_bf16(x)
    return jnp.dot(jnp.concatenate([hi, lo], axis=-1), ones, preferred_element_type=F32)


def _seg_ones(width):
    m = np.arange(2 * width)[:, None] % width
    n = np.arange(width)[None, :]
    return jnp.asarray(m // N_B == n // N_B, BF16)


def _ln(x, g, b):
    mu = jnp.mean(x, -1, keepdims=True)
    xc = x - mu
    var = jnp.mean(xc * xc, -1, keepdims=True)
    return xc * lax.rsqrt(var + LN_EPS) * g + b


def _ln_kernel(x_ref, r_ref, g_ref, b_ref, o_ref, *, alpha):
    o_ref[...] = _ln(x_ref[...] * alpha + r_ref[...], g_ref[...], b_ref[...])


def _ln_residual(x, r, g, b, alpha):
    M, D = x.shape
    tm = _row_tile(M)
    row = pl.BlockSpec((tm, D), lambda i: (i, 0))
    vec = pl.BlockSpec((1, D), lambda i: (0, 0))
    return pl.pallas_call(
        functools.partial(_ln_kernel, alpha=alpha),
        out_shape=jax.ShapeDtypeStruct((M, D), F32),
        grid=(M // tm,),
        in_specs=[row, row, vec, vec],
        out_specs=row,
        compiler_params=_cparams(("parallel",)),
        name="layer_norm",
    )(x, r, g.reshape(1, D), b.reshape(1, D))


def _rms(x, g):
    return x * lax.rsqrt(jnp.mean(x * x, -1, keepdims=True) + RMS_EPS) * g


def _proj_kernel(h_ref, win_ref, qn_ref, kvn_ref, wuq_ref, wukv_ref, c_ref, s_ref,
                 q_ref, k_ref, v_ref, rw_ref, *, qscale):
    proj = jnp.dot(h_ref[...].astype(BF16), win_ref[...], preferred_element_type=F32)
    rw_ref[...] = proj[:, MLA_COLS + N_B:]
    q = jnp.dot(_rms(proj[:, :Q_LORA], qn_ref[...]).astype(BF16), wuq_ref[...], preferred_element_type=F32)
    kv = jnp.dot(_rms(proj[:, Q_LORA:Q_LORA + KV_LORA], kvn_ref[...]).astype(BF16), wukv_ref[...],
                 preferred_element_type=F32)
    cos, sin = c_ref[...], s_ref[...]
    lane = lax.broadcasted_iota(jnp.int32, cos.shape, 1)
    half = QK_ROPE // 2

    def rope(x):
        partner = jnp.where(lane < half, pltpu.roll(x, LANES - half, 1), pltpu.roll(x, half, 1))
        return x * cos + partner * sin

    k_pe = rope(proj[:, Q_LORA + KV_LORA:Q_LORA + KV_LORA + LANES]).astype(BF16)
    for h in range(H_A):
        o = h * HEAD_W
        q_ref[:, o:o + LANES] = (q[:, o:o + LANES] * qscale).astype(BF16)
        q_ref[:, o + LANES:o + HEAD_W] = (rope(q[:, o + LANES:o + HEAD_W]) * qscale).astype(BF16)
        k_ref[:, o:o + LANES] = kv[:, o:o + LANES].astype(BF16)
        k_ref[:, o + LANES:o + HEAD_W] = k_pe
        v_ref[:, h * V_DIM:(h + 1) * V_DIM] = kv[:, o + LANES:o + HEAD_W].astype(BF16)


def _project(h, w_in, q_norm, kv_norm, w_uq, w_ukv, rope_cos, rope_sin, qscale):
    T, D = h.shape
    tm = _row_tile(T)
    n_rw = w_in.shape[1] - MLA_COLS
    pad = jnp.zeros((D, N_B), F32)
    win = jnp.concatenate([w_in[:, :MLA_COLS], pad, w_in[:, MLA_COLS:]], axis=1).astype(BF16)
    wq = w_uq.reshape(Q_LORA, H_A, QK_NOPE + QK_ROPE)
    wq = jnp.concatenate([wq, jnp.zeros((Q_LORA, H_A, HEAD_W - QK_NOPE - QK_ROPE), F32)], axis=-1)
    wq = wq.reshape(Q_LORA, H_A * HEAD_W).astype(BF16)
    row = lambda w: pl.BlockSpec((tm, w), lambda i: (i, 0))
    full = lambda a: pl.BlockSpec(a.shape, lambda i: (0, 0))
    args = (h, win, q_norm.reshape(1, -1), kv_norm.reshape(1, -1), wq, w_ukv.astype(BF16), rope_cos, rope_sin)
    return pl.pallas_call(
        functools.partial(_proj_kernel, qscale=qscale),
        out_shape=(jax.ShapeDtypeStruct((T, H_A * HEAD_W), BF16),
                   jax.ShapeDtypeStruct((T, H_A * HEAD_W), BF16),
                   jax.ShapeDtypeStruct((T, H_A * V_DIM), BF16),
                   jax.ShapeDtypeStruct((T, n_rw), F32)),
        grid=(T // tm,),
        in_specs=[row(D)] + [full(a) for a in args[1:6]] + [row(LANES), row(LANES)],
        out_specs=(row(H_A * HEAD_W), row(H_A * HEAD_W), row(H_A * V_DIM), row(n_rw)),
        compiler_params=_cparams(("parallel",)),
        name="in_proj",
    )(*args)


def _attn_kernel(q_ref, k_ref, v_ref, bias_ref, g_ref, o_ref):
    s = lax.dot_general(q_ref[0], k_ref[0], (((1,), (1,)), ((), ())), preferred_element_type=F32)
    s = s + bias_ref[...]
    m = jnp.max(s, -1, keepdims=True)
    p = jnp.exp2(s - m)
    l = jnp.sum(p, -1, keepdims=True)
    o = jnp.dot(p.astype(BF16), v_ref[0], preferred_element_type=F32) / l
    o = o * lax.rsqrt(jnp.mean(o * o, -1, keepdims=True) + RMS_EPS) * g_ref[...]
    o_ref[0] = o.astype(BF16)


def _attention(q, k, v, bias, out_gain, tq):
    B, Lp, _ = q.shape
    return pl.pallas_call(
        _attn_kernel,
        out_shape=jax.ShapeDtypeStruct((B, Lp, H_A * V_DIM), BF16),
        grid=(B, H_A, Lp // tq),
        in_specs=[pl.BlockSpec((1, tq, HEAD_W), lambda b, h, i: (b, i, h)),
                  pl.BlockSpec((1, Lp, HEAD_W), lambda b, h, i: (b, 0, h)),
                  pl.BlockSpec((1, Lp, V_DIM), lambda b, h, i: (b, 0, h)),
                  pl.BlockSpec((1, Lp), lambda b, h, i: (0, 0)),
                  pl.BlockSpec((1, V_DIM), lambda b, h, i: (0, h))],
        out_specs=pl.BlockSpec((1, tq, V_DIM), lambda b, h, i: (b, i, h)),
        compiler_params=_cparams(("parallel", "parallel", "parallel")),
        name="mla_attention",
    )(q, k, v, bias, out_gain.reshape(1, H_A * V_DIM))


def _prep_kernel(*refs, first, seq_len, tm):
    if first:
        (rw_ref, pv_ref, nx_ref, mup_ref, mun_ref, w2_ref, w0_ref, a2_ref, a0_ref, g2_ref, kk_ref, ka_ref, rk_ref,
         ones_ref, r_o, kk_o, v_o, wf_o, kdf_o, bf_o, wb_o, kdb_o, bb_o, g_o, bvg_o, vfirst_o) = refs
    else:
        (rw_ref, pv_ref, nx_ref, mup_ref, mun_ref, w2_ref, w0_ref, a2_ref, a0_ref, g2_ref, kk_ref, ka_ref, rk_ref,
         ones_ref, vf_ref, v0_ref, v1_ref, v2_ref,
         r_o, kk_o, v_o, wf_o, kdf_o, bf_o, wb_o, kdb_o, bb_o, g_o, bvg_o) = refs
    i = pl.program_id(1)
    n_t = pl.num_programs(1)
    row = lax.broadcasted_iota(jnp.int32, (tm, 1), 0)
    t = i * tm + row
    valid = t < seq_len
    rw = jnp.where(valid, rw_ref[0], 0.0)
    prev_row = jnp.where((i > 0) & (i * tm - 1 < seq_len), pv_ref[0, SUBLANES - 1:SUBLANES, :], 0.0)
    next_row = jnp.where((i < n_t - 1) & ((i + 1) * tm < seq_len), nx_ref[0, 0:1, :], 0.0)
    prev = jnp.where(row == 0, prev_row, pltpu.roll(rw, 1, 0))
    nxt = jnp.where(row == tm - 1, next_row, pltpu.roll(rw, tm - 1, 0))
    u = rw + mup_ref[...] * (prev - rw) + mun_ref[...] * (nxt - rw)
    r, k, v = u[:, :RW], u[:, RW:2 * RW], u[:, 2 * RW:3 * RW]
    wd = u[:, 3 * RW:3 * RW + LANES]
    ad = u[:, 3 * RW + LANES:3 * RW + 2 * LANES]
    gd = u[:, 3 * RW + 2 * LANES:]
    ones = ones_ref[...]
    if first:
        vfirst_o[0] = v
    else:
        low = jnp.dot(v.astype(BF16), v1_ref[...], preferred_element_type=F32)
        mix = jax.nn.sigmoid(v0_ref[...] + jnp.dot(low.astype(BF16), v2_ref[...], preferred_element_type=F32))
        v = v + (vf_ref[0] - v) * mix
    g = jnp.dot(jax.nn.sigmoid(gd).astype(BF16), g2_ref[...], preferred_element_type=F32)
    kk = k * kk_ref[...]
    kk = jnp.where(valid, kk * lax.rsqrt(_seg_sum(kk * kk, ones) + 1e-12), 0.0)
    wl = w0_ref[...] + jnp.dot(jnp.tanh(wd).astype(BF16), w2_ref[...], preferred_element_type=F32)
    decay = jnp.exp(-math.exp(-0.5) * jax.nn.sigmoid(wl))
    a = jax.nn.sigmoid(a0_ref[...] + jnp.dot(ad.astype(BF16), a2_ref[...], preferred_element_type=F32))
    ka = ka_ref[...]
    kd_f = jnp.where(valid, k * (1.0 + (a[:, :RW] - 1.0) * ka), 0.0)
    kd_b = jnp.where(valid, k * (1.0 + (a[:, RW:] - 1.0) * ka), 0.0)
    bonus = _seg_sum(r * (kd_f + kd_b) * rk_ref[...], ones)
    r_o[0] = r
    kk_o[0] = kk
    v_o[0] = jnp.where(valid, v, 0.0)
    wf_o[0] = decay[:, :RW]
    wb_o[0] = decay[:, RW:]
    kdf_o[0] = kd_f
    kdb_o[0] = kd_b
    bf_o[0] = kk * a[:, :RW]
    bb_o[0] = kk * a[:, RW:]
    g_o[0] = g
    bvg_o[0] = bonus * v * g


def _block_diag2(a, b):
    z = jnp.zeros_like(a)
    return jnp.concatenate([jnp.concatenate([a, z], 1), jnp.concatenate([z, b], 1)], 0)


def _rwkv_prep(rw, seq_len, mu_prev, mu_next, w0, w2, a0, a2, g2, k_k, k_a, r_k, v_first, vres):
    B, Lp, n_rw = rw.shape
    tm = _row_tile(Lp)
    tpb = tm // SUBLANES
    first = vres is None
    vec = lambda a: a.reshape(1, -1)
    consts = [vec(mu_prev), vec(mu_next), _block_diag2(w2[0], w2[1]).astype(BF16), vec(w0),
              _block_diag2(a2[0], a2[1]).astype(BF16), vec(a0), g2.astype(BF16), vec(k_k), vec(k_a), vec(r_k),
              _seg_ones(RW)]
    tile = lambda w: pl.BlockSpec((1, tm, w), lambda b, i: (b, i, 0))
    full = lambda a: pl.BlockSpec(a.shape, lambda b, i: (0, 0))
    in_specs = [tile(n_rw),
                pl.BlockSpec((1, SUBLANES, n_rw), lambda b, i: (b, jnp.maximum(i * tpb - 1, 0), 0)),
                pl.BlockSpec((1, SUBLANES, n_rw), lambda b, i: (b, jnp.minimum((i + 1) * tpb, Lp // SUBLANES - 1), 0))]
    in_specs += [full(c) for c in consts]
    args = [rw, rw, rw] + consts
    n_out = 11
    if first:
        n_out += 1
    else:
        v0, v1, v2 = vres
        v1p = jnp.concatenate([v1, jnp.zeros((RW, LANES - VRES_LORA), F32)], 1).astype(BF16)
        v2p = jnp.concatenate([v2, jnp.zeros((LANES - VRES_LORA, RW), F32)], 0).astype(BF16)
        extra = [vec(v0), v1p, v2p]
        in_specs += [tile(RW)] + [full(c) for c in extra]
        args += [v_first] + extra
    out = pl.pallas_call(
        functools.partial(_prep_kernel, first=first, seq_len=seq_len, tm=tm),
        out_shape=tuple(jax.ShapeDtypeStruct((B, Lp, RW), F32) for _ in range(n_out)),
        grid=(B, Lp // tm),
        in_specs=in_specs,
        out_specs=tuple(tile(RW) for _ in range(n_out)),
        compiler_params=_cparams(("parallel", "parallel")),
        name="rwkv_prep",
    )(*args)
    return out


def _scan_kernel(rf, kkf, wf, kdf, bf, vf, rb, kkb, wb, kdb, bb_, vb_, ones_ref,
                 yf_ref, yb_ref, s_ref, vt_ref, acc_ref, *, nb):
    j = pl.program_id(1)
    C = nb * 4
    half_t = TIME_BLOCK // 2

    @pl.when(j == 0)
    def _():
        s_ref[...] = jnp.zeros_like(s_ref)

    lane1 = lax.broadcasted_iota(jnp.int32, (N_B, LANES), 1)
    lo_half = lane1 < N_B

    for d, vref in enumerate((vf, vb_)):
        for n in range(nb):
            for hp in range(4):
                c = n * 4 + hp
                xt = vref[n, :, hp * LANES:(hp + 1) * LANES].T
                top, bot = xt[:N_B], xt[N_B:]
                vt_ref[d, 0, c * N_B:(c + 1) * N_B, :] = jnp.where(lo_half, top, pltpu.roll(bot, N_B, 1))
                vt_ref[d, 1, c * N_B:(c + 1) * N_B, :] = jnp.where(lo_half, pltpu.roll(top, N_B, 1), bot)
    acc_ref[...] = jnp.zeros_like(acc_ref)

    ones = ones_ref[...]
    lane_id = lax.broadcasted_iota(jnp.int32, (C * N_B, LANES), 1)
    lane = lane_id % N_B
    head_base = (lane_id // N_B) * N_B
    row_refs = ((rf, kkf, wf, kdf, bf), (rb, kkb, wb, kdb, bb_))

    def step(d, tiles, row, oh, tt):
        def rows(tile):
            return jnp.concatenate(
                [jnp.broadcast_to(tile[n, row:row + 1, hp * LANES:(hp + 1) * LANES], (N_B, LANES))
                 for n in range(nb) for hp in range(4)], axis=0)

        r, kk, w, kd, b = [rows(x) for x in tiles]
        S = s_ref[d]
        sa = _seg_sum(S * kk, ones)
        vb = jnp.take_along_axis(vt_ref[d, oh], head_base + tt, axis=1, mode="promise_in_bounds")
        S2 = S * w - sa * b + vb * kd
        s_ref[d] = S2
        y = _seg_sum(S2 * r, ones)
        acc_ref[d, oh] = jnp.where(lane == tt, y, acc_ref[d, oh])

    def make_body(half):
        def body(g, carry):
            t0f = pl.multiple_of(half * half_t + g * SUBLANES, SUBLANES)
            t0b = pl.multiple_of(TIME_BLOCK - SUBLANES - (half * half_t + g * SUBLANES), SUBLANES)
            tiles_f = [x[:, pl.ds(t0f, SUBLANES), :] for x in row_refs[0]]
            tiles_b = [x[:, pl.ds(t0b, SUBLANES), :] for x in row_refs[1]]
            for i in range(SUBLANES):
                step(0, tiles_f, i, half, g * SUBLANES + i)
                step(1, tiles_b, SUBLANES - 1 - i, 1 - half, half_t - 1 - (g * SUBLANES + i))
            return carry
        return body

    lax.fori_loop(0, half_t // SUBLANES, make_body(0), 0)
    lax.fori_loop(0, half_t // SUBLANES, make_body(1), 0)

    for d, yref in enumerate((yf_ref, yb_ref)):
        for n in range(nb):
            for hp in range(4):
                c = n * 4 + hp
                a = jnp.concatenate([acc_ref[d, 0, c * N_B:(c + 1) * N_B, :],
                                     acc_ref[d, 1, c * N_B:(c + 1) * N_B, :]], axis=0).T
                top, bot = a[:N_B], a[N_B:]
                yref[n, :N_B, hp * LANES:(hp + 1) * LANES] = jnp.where(lo_half, top, pltpu.roll(bot, N_B, 1))
                yref[n, N_B:, hp * LANES:(hp + 1) * LANES] = jnp.where(lo_half, pltpu.roll(top, N_B, 1), bot)


def _wkv_bidir(r, kk, v, per_dir, nb):
    B, Lp, _ = r.shape
    nblk = Lp // TIME_BLOCK
    ones = _seg_ones(LANES)
    fwd = lambda bi, j: (bi, j, 0)
    bwd = lambda bi, j: (bi, nblk - 1 - j, 0)
    blk = (nb, TIME_BLOCK, RW)
    (wf, kdf, bf), (wb, kdb, bb_) = per_dir
    y_shape = jax.ShapeDtypeStruct((B, Lp, RW), F32)
    rows = nb * 4 * N_B
    return pl.pallas_call(
        functools.partial(_scan_kernel, nb=nb),
        out_shape=(y_shape, y_shape),
        grid=(B // nb, nblk),
        in_specs=[pl.BlockSpec(blk, fwd)] * 6 + [pl.BlockSpec(blk, bwd)] * 6
                 + [pl.BlockSpec(ones.shape, lambda bi, j: (0, 0))],
        out_specs=(pl.BlockSpec(blk, fwd), pl.BlockSpec(blk, bwd)),
        scratch_shapes=[pltpu.VMEM((2, rows, LANES), F32),
                        pltpu.VMEM((2, 2, rows, LANES), F32),
                        pltpu.VMEM((2, 2, rows, LANES), F32)],
        compiler_params=_cparams(("parallel", "arbitrary")),
        name="wkv_scan",
    )(r, kk, wf, kdf, bf, v, r, kk, wb, kdb, bb_, v, ones)


def _post_kernel(yf_ref, yb_ref, g_ref, bvg_ref, att_ref, h_ref, gng_ref, gnb_ref, wo_ref, l1g_ref, l1b_ref,
                 wrh_ref, wrl_ref, rb_ref, tri_ref, ones_ref,
                 h1_ref, rf_ref, ri_ref, cnt_ref, *, alpha, tm):
    ones = ones_ref[...]
    y = yf_ref[...] + yb_ref[...]
    mu = _seg_sum(y, ones) * (1.0 / N_B)
    yc = y - mu
    var = _seg_sum(yc * yc, ones) * (1.0 / N_B)
    yr = (yc * lax.rsqrt(var + RWKV_GN_EPS) * gng_ref[...] + gnb_ref[...]) * g_ref[...] + bvg_ref[...]
    half = H_A * V_DIM
    mixed = (jnp.dot(att_ref[...], wo_ref[:half, :], preferred_element_type=F32)
             + jnp.dot(yr.astype(BF16), wo_ref[half:, :], preferred_element_type=F32))
    h1 = _ln(h_ref[...] * alpha + mixed, l1g_ref[...], l1b_ref[...])
    h1_ref[...] = h1

    xh, xl = _split_bf16(h1)
    x = (jnp.dot(xh, wrh_ref[...], preferred_element_type=F32) + jnp.dot(xl, wrh_ref[...], preferred_element_type=F32)
         + jnp.dot(xh, wrl_ref[...], preferred_element_type=F32)) + rb_ref[...]
    lane = lax.broadcasted_iota(jnp.int32, x.shape, 1)
    lanef = lane.astype(F32)
    big = float(LANES)
    gmask = lane < N_GROUPS
    gx = jnp.where(gmask, x, NEG)
    gmax = jnp.max(gx, -1, keepdims=True)
    gidx = jnp.min(jnp.where(gx == gmax, lanef, big), -1, keepdims=True)
    gsum = jnp.sum(jnp.where(gmask, jnp.exp(gx - gmax), 0.0), -1, keepdims=True)
    lo = N_GROUPS + EXPERTS_PER_GROUP * gidx
    emask = (lanef >= lo) & (lanef < lo + EXPERTS_PER_GROUP)
    ex = jnp.where(emask, x, NEG)
    m1 = jnp.max(ex, -1, keepdims=True)
    i1 = jnp.min(jnp.where(emask & (ex == m1), lanef, big), -1, keepdims=True)
    ex2 = jnp.where(lanef == i1, NEG, ex)
    m2 = jnp.max(ex2, -1, keepdims=True)
    i2 = jnp.min(jnp.where(emask & (lanef != i1) & (ex2 == m2), lanef, big), -1, keepdims=True)
    esum = jnp.sum(jnp.where(emask, jnp.exp(ex - m1), 0.0), -1, keepdims=True)
    gp = 1.0 / gsum
    gate0 = gp * (1.0 / esum)
    gate1 = gp * (jnp.exp(m2 - m1) / esum)
    e0 = i1 - N_GROUPS
    e1 = i2 - N_GROUPS

    @pl.when(pl.program_id(0) == 0)
    def _():
        cnt_ref[...] = jnp.zeros_like(cnt_ref)

    onehot = jnp.where((lanef == e0) | (lanef == e1), 1.0, 0.0)
    prefix = jnp.dot(tri_ref[...], onehot.astype(BF16), preferred_element_type=F32) + cnt_ref[...]
    r0 = jnp.sum(jnp.where(lanef == e0, prefix, 0.0), -1, keepdims=True)
    r1 = jnp.sum(jnp.where(lanef == e1, prefix, 0.0), -1, keepdims=True)
    cnt_ref[...] += jnp.sum(onehot, 0, keepdims=True)
    rf_ref[...] = jnp.where(lane == 0, gate0, jnp.where(lane == 1, gate1, 0.0))
    ri = jnp.where(lane == 0, e0, jnp.where(lane == 1, e1, jnp.where(lane == 2, r0, jnp.where(lane == 3, r1, 0.0))))
    ri_ref[...] = ri.astype(jnp.int32)


def _post_mixer(yf, yb, g, bvg, att, h, gn_g, gn_b, w_out, ln_g, ln_b, w_group, b_group, w_expert, b_expert, alpha):
    T, D = h.shape
    tm = _row_tile(T)
    vec = lambda a: a.reshape(1, -1)
    zpad = LANES - N_GROUPS - N_EXPERTS
    wr = jnp.concatenate([w_group, w_expert, jnp.zeros((D, zpad), F32)], axis=1)
    wrh = wr.astype(BF16)
    wrl = (wr - wrh.astype(F32)).astype(BF16)
    rb = jnp.concatenate([b_group, b_expert, jnp.zeros((zpad,), F32)]).reshape(1, LANES)
    tri = jnp.asarray(np.arange(tm)[:, None] > np.arange(tm)[None, :], BF16)
    consts = [vec(gn_g), vec(gn_b), w_out.astype(BF16), vec(ln_g), vec(ln_b), wrh, wrl, rb, tri, _seg_ones(RW)]
    row = lambda w: pl.BlockSpec((tm, w), lambda i: (i, 0))
    full = lambda a: pl.BlockSpec(a.shape, lambda i: (0, 0))
    return pl.pallas_call(
        functools.partial(_post_kernel, alpha=alpha, tm=tm),
        out_shape=(jax.ShapeDtypeStruct((T, D), F32),
                   jax.ShapeDtypeStruct((T, LANES), F32),
                   jax.ShapeDtypeStruct((T, LANES), jnp.int32),
                   jax.ShapeDtypeStruct((1, LANES), F32)),
        grid=(T // tm,),
        in_specs=[row(RW)] * 4 + [row(H_A * V_DIM), row(D)] + [full(c) for c in consts],
        out_specs=(row(D), row(LANES), row(LANES), pl.BlockSpec((1, LANES), lambda i: (0, 0))),
        compiler_params=_cparams(("arbitrary",)),
        name="post_mixer",
    )(yf, yb, g, bvg, att, h, *consts)


def _expert_kernel(be_ref, x_ref, g_ref, w1_ref, w3_ref, w2_ref, o_ref):
    x = x_ref[...]
    a = jnp.dot(x, w1_ref[0], preferred_element_type=F32)
    b = jnp.dot(x, w3_ref[0], preferred_element_type=F32)
    hid = (a * jax.nn.sigmoid(a)) * b
    y = jnp.dot(hid.astype(BF16), w2_ref[0], preferred_element_type=F32)
    o_ref[...] = y * g_ref[...]


def _expert_ffn(xb, row_gate, block_e, w1, w3, w2):
    n_rows, D = xb.shape
    n_blocks = n_rows // MOE_BLOCK
    grid_spec = pltpu.PrefetchScalarGridSpec(
        num_scalar_prefetch=1,
        grid=(n_blocks,),
        in_specs=[pl.BlockSpec((MOE_BLOCK, D), lambda i, be: (i, 0)),
                  pl.BlockSpec((MOE_BLOCK, 1), lambda i, be: (i, 0)),
                  pl.BlockSpec((1, D, E_HID), lambda i, be: (be[i], 0, 0)),
                  pl.BlockSpec((1, D, E_HID), lambda i, be: (be[i], 0, 0)),
                  pl.BlockSpec((1, E_HID, D), lambda i, be: (be[i], 0, 0))],
        out_specs=pl.BlockSpec((MOE_BLOCK, D), lambda i, be: (i, 0)),
    )
    return pl.pallas_call(
        _expert_kernel,
        out_shape=jax.ShapeDtypeStruct((n_rows, D), F32),
        grid_spec=grid_spec,
        compiler_params=_cparams(("arbitrary",)),
        name="moe_experts",
    )(block_e, xb, row_gate.reshape(n_rows, 1), w1, w3, w2)


def _hier_moe(h, route_f, route_i, counts, w1, w3, w2):
    T, D = h.shape
    counts = counts[0, :N_EXPERTS].astype(jnp.int32)
    padded = (counts + MOE_BLOCK - 1) // MOE_BLOCK * MOE_BLOCK
    pends = jnp.cumsum(padded)
    pstart = pends - padded
    dest = (pstart[route_i[:, :TOP_K]] + route_i[:, TOP_K:2 * TOP_K]).reshape(-1)
    gate = route_f[:, :TOP_K].reshape(-1)
    flat_tok = jnp.repeat(jnp.arange(T, dtype=jnp.int32), TOP_K)
    n_rows = (T * TOP_K + N_EXPERTS * (MOE_BLOCK - 1) + MOE_BLOCK - 1) // MOE_BLOCK * MOE_BLOCK
    n_blocks = n_rows // MOE_BLOCK
    row_tok = jnp.zeros((n_rows,), jnp.int32).at[dest].set(flat_tok)
    row_gate = jnp.zeros((n_rows,), F32).at[dest].set(gate)
    block_e = jnp.clip(jnp.searchsorted(pends, jnp.arange(n_blocks) * MOE_BLOCK, side='right'),
                       0, N_EXPERTS - 1).astype(jnp.int32)
    xb = h.astype(BF16)[row_tok]
    yb = _expert_ffn(xb, row_gate, block_e, w1, w3, w2)
    return jnp.zeros((T, D), F32).at[row_tok].add(yb)


def kernel(x, positions, meta_tokens, emb_ln_g, emb_ln_b, w_in, mla_q_norm, mla_kv_norm, mla_w_uq, mla_w_ukv, mla_out_norm, rwkv_mu_prev, rwkv_mu_next, rwkv_w0, rwkv_w2, rwkv_a0, rwkv_a2, rwkv_g2, rwkv_k_k, rwkv_k_a, rwkv_r_k, rwkv_gn_g, rwkv_gn_b, rwkv_v0, rwkv_v1, rwkv_v2, w_out, ln1_g, ln1_b, moe_w_group, moe_b_group, moe_w_expert, moe_b_expert, moe_w1, moe_w3, moe_w2, ln2_g, ln2_b):
    B, seq, D = x.shape
    depth = w_in.shape[0]
    alpha = (2 * depth) ** 0.25
    L = seq + N_META
    Lp = -(-L // LANES) * LANES
    T = B * Lp
    nb = 2 if B % 2 == 0 else 1
    tq = 384 if Lp % 384 == 0 else LANES

    meta = jnp.broadcast_to(meta_tokens.astype(x.dtype)[None], (B, N_META, D))
    h = jnp.concatenate([meta, x, jnp.zeros((B, Lp - L, D), x.dtype)], axis=1).reshape(T, D)
    h = _ln_residual(h, jnp.zeros_like(h), emb_ln_g, emb_ln_b, 1.0)

    key_bias = jnp.where(jnp.arange(Lp) < L, 0.0, NEG).astype(F32)[None, :]
    pos = jnp.concatenate([jnp.broadcast_to(jnp.arange(N_META, dtype=jnp.int32), (B, N_META)),
                           positions + N_META, jnp.zeros((B, Lp - L), jnp.int32)], axis=1)
    inv_freq = ROPE_THETA ** (-jnp.arange(0, QK_ROPE, 2, dtype=F32) / QK_ROPE)
    ang = pos.astype(F32)[..., None] * inv_freq
    cos, sin = jnp.cos(ang).reshape(T, -1), jnp.sin(ang).reshape(T, -1)
    zeros = jnp.zeros((T, LANES - QK_ROPE), F32)
    rope_cos = jnp.concatenate([cos, cos, zeros], axis=1)
    rope_sin = jnp.concatenate([-sin, sin, zeros], axis=1)
    qscale = (QK_NOPE + QK_ROPE) ** -0.5 * math.log2(math.e)

    v_first = None
    for li in range(depth):
        q, k, v, rw = _project(h, w_in[li], mla_q_norm[li], mla_kv_norm[li], mla_w_uq[li], mla_w_ukv[li],
                               rope_cos, rope_sin, qscale)
        y_att = _attention(q.reshape(B, Lp, -1), k.reshape(B, Lp, -1), v.reshape(B, Lp, -1),
                           key_bias, mla_out_norm[li], tq)
        vres = None if li == 0 else (rwkv_v0[li - 1], rwkv_v1[li - 1], rwkv_v2[li - 1])
        outs = _rwkv_prep(rw.reshape(B, Lp, -1), L, rwkv_mu_prev[li], rwkv_mu_next[li], rwkv_w0[li], rwkv_w2[li],
                          rwkv_a0[li], rwkv_a2[li], rwkv_g2[li], rwkv_k_k[li], rwkv_k_a[li], rwkv_r_k[li],
                          v_first, vres)
        r_, kk, vm, wf, kdf, bf, wb, kdb, bb_, g, bvg = outs[:11]
        if li == 0:
            v_first = outs[11]
        y_f, y_b = _wkv_bidir(r_, kk, vm, ((wf, kdf, bf), (wb, kdb, bb_)), nb)
        flat = lambda a: a.reshape(T, -1)
        h, route_f, route_i, counts = _post_mixer(
            flat(y_f), flat(y_b), flat(g), flat(bvg), flat(y_att), h, rwkv_gn_g[li], rwkv_gn_b[li], w_out[li],
            ln1_g[li], ln1_b[li], moe_w_group[li], moe_b_group[li], moe_w_expert[li], moe_b_expert[li], alpha)
        ff = _hier_moe(h, route_f, route_i, counts,
                       moe_w1[li].astype(BF16), moe_w3[li].astype(BF16), moe_w2[li].astype(BF16))
        h = _ln_residual(h, ff, ln2_g[li], ln2_b[li], alpha)
    return h.reshape(B, Lp, D)[:, N_META:L]
```

```python
import functools
import math

import numpy as np
import jax
import jax.numpy as jnp
from jax import lax
from jax.experimental import pallas as pl
from jax.experimental.pallas import tpu as pltpu

F32 = jnp.float32
BF16 = jnp.bfloat16

N_META = 16
H_A = 4
QK_NOPE = 128
QK_ROPE = 64
V_DIM = 128
Q_LORA = 256
KV_LORA = 128
ROPE_THETA = 10000.0
H_B = 8
N_B = 64
RW = H_B * N_B
DECAY_LORA = 64
AAA_LORA = 64
GATE_LORA = 128
VRES_LORA = 32
RWKV_GN_EPS = 64e-5
MLA_COLS = Q_LORA + KV_LORA + QK_ROPE
N_GROUPS = 4
EXPERTS_PER_GROUP = 8
N_EXPERTS = N_GROUPS * EXPERTS_PER_GROUP
TOP_K = 2
E_HID = 256
MOE_BLOCK = 128
LN_EPS = 1e-5
RMS_EPS = 1e-6

LANES = 128
SUBLANES = 8
TIME_BLOCK = 128
ROW_TILE = 512
VMEM_LIMIT = 56 * 1024 * 1024
HEAD_W = 2 * LANES
NEG = -1e30


def _cparams(sem):
    return pltpu.CompilerParams(dimension_semantics=sem, vmem_limit_bytes=VMEM_LIMIT)


def _row_tile(m):
    return next(t for t in (ROW_TILE, 384, 256, LANES) if m % t == 0)


def _split_bf16(x):
    hi = x.astype(BF16)
    lo = (x - hi.astype(F32)).astype(BF16)
    return hi, lo


def _seg_sum(x, ones):
    hi, lo = _split_bf16(x)
    return jnp.dot(jnp.concatenate([hi, lo], axis=-1), ones, preferred_element_type=F32)


def _seg_ones(width):
    m = np.arange(2 * width)[:, None] % width
    n = np.arange(width)[None, :]
    return jnp.asarray(m // N_B == n // N_B, BF16)


def _ln(x, g, b):
    mu = jnp.mean(x, -1, keepdims=True)
    xc = x - mu
    var = jnp.mean(xc * xc, -1, keepdims=True)
    return xc * lax.rsqrt(var + LN_EPS) * g + b


def _ln_kernel(x_ref, r_ref, g_ref, b_ref, o_ref, *, alpha):
    o_ref[...] = _ln(x_ref[...] * alpha + r_ref[...], g_ref[...], b_ref[...])


def _ln_residual(x, r, g, b, alpha):
    M, D = x.shape
    tm = _row_tile(M)
    row = pl.BlockSpec((tm, D), lambda i: (i, 0))
    vec = pl.BlockSpec((1, D), lambda i: (0, 0))
    return pl.pallas_call(
        functools.partial(_ln_kernel, alpha=alpha),
        out_shape=jax.ShapeDtypeStruct((M, D), F32),
        grid=(M // tm,),
        in_specs=[row, row, vec, vec],
        out_specs=row,
        compiler_params=_cparams(("parallel",)),
        name="layer_norm",
    )(x, r, g.reshape(1, D), b.reshape(1, D))


def _rms(x, g):
    return x * lax.rsqrt(jnp.mean(x * x, -1, keepdims=True) + RMS_EPS) * g


def _proj_kernel(h_ref, win_ref, qn_ref, kvn_ref, wuq_ref, wukv_ref, c_ref, s_ref,
                 q_ref, k_ref, v_ref, rw_ref, *, qscale):
    proj = jnp.dot(h_ref[...].astype(BF16), win_ref[...], preferred_element_type=F32)
    rw_ref[...] = proj[:, MLA_COLS + N_B:]
    q = jnp.dot(_rms(proj[:, :Q_LORA], qn_ref[...]).astype(BF16), wuq_ref[...], preferred_element_type=F32)
    kv = jnp.dot(_rms(proj[:, Q_LORA:Q_LORA + KV_LORA], kvn_ref[...]).astype(BF16), wukv_ref[...],
                 preferred_element_type=F32)
    cos, sin = c_ref[...], s_ref[...]
    lane = lax.broadcasted_iota(jnp.int32, cos.shape, 1)
    half = QK_ROPE // 2

    def rope(x):
        partner = jnp.where(lane < half, pltpu.roll(x, LANES - half, 1), pltpu.roll(x, half, 1))
        return x * cos + partner * sin

    k_pe = rope(proj[:, Q_LORA + KV_LORA:Q_LORA + KV_LORA + LANES]).astype(BF16)
    for h in range(H_A):
        o = h * HEAD_W
        q_ref[:, o:o + LANES] = (q[:, o:o + LANES] * qscale).astype(BF16)
        q_ref[:, o + LANES:o + HEAD_W] = (rope(q[:, o + LANES:o + HEAD_W]) * qscale).astype(BF16)
        k_ref[:, o:o + LANES] = kv[:, o:o + LANES].astype(BF16)
        k_ref[:, o + LANES:o + HEAD_W] = k_pe
        v_ref[:, h * V_DIM:(h + 1) * V_DIM] = kv[:, o + LANES:o + HEAD_W].astype(BF16)


def _project(h, w_in, q_norm, kv_norm, w_uq, w_ukv, rope_cos, rope_sin, qscale):
    T, D = h.shape
    tm = _row_tile(T)
    n_rw = w_in.shape[1] - MLA_COLS
    pad = jnp.zeros((D, N_B), F32)
    win = jnp.concatenate([w_in[:, :MLA_COLS], pad, w_in[:, MLA_COLS:]], axis=1).astype(BF16)
    wq = w_uq.reshape(Q_LORA, H_A, QK_NOPE + QK_ROPE)
    wq = jnp.concatenate([wq, jnp.zeros((Q_LORA, H_A, HEAD_W - QK_NOPE - QK_ROPE), F32)], axis=-1)
    wq = wq.reshape(Q_LORA, H_A * HEAD_W).astype(BF16)
    row = lambda w: pl.BlockSpec((tm, w), lambda i: (i, 0))
    full = lambda a: pl.BlockSpec(a.shape, lambda i: (0, 0))
    args = (h, win, q_norm.reshape(1, -1), kv_norm.reshape(1, -1), wq, w_ukv.astype(BF16), rope_cos, rope_sin)
    return pl.pallas_call(
        functools.partial(_proj_kernel, qscale=qscale),
        out_shape=(jax.ShapeDtypeStruct((T, H_A * HEAD_W), BF16),
                   jax.ShapeDtypeStruct((T, H_A * HEAD_W), BF16),
                   jax.ShapeDtypeStruct((T, H_A * V_DIM), BF16),
                   jax.ShapeDtypeStruct((T, n_rw), F32)),
        grid=(T // tm,),
        in_specs=[row(D)] + [full(a) for a in args[1:6]] + [row(LANES), row(LANES)],
        out_specs=(row(H_A * HEAD_W), row(H_A * HEAD_W), row(H_A * V_DIM), row(n_rw)),
        compiler_params=_cparams(("parallel",)),
        name="in_proj",
    )(*args)


def _attn_kernel(q_ref, k_ref, v_ref, bias_ref, g_ref, o_ref):
    s = lax.dot_general(q_ref[0], k_ref[0], (((1,), (1,)), ((), ())), preferred_element_type=F32)
    s = s + bias_ref[...]
    m = jnp.max(s, -1, keepdims=True)
    p = jnp.exp2(s - m)
    l = jnp.sum(p, -1, keepdims=True)
    o = jnp.dot(p.astype(BF16), v_ref[0], preferred_element_type=F32) / l
    o = o * lax.rsqrt(jnp.mean(o * o, -1, keepdims=True) + RMS_EPS) * g_ref[...]
    o_ref[0] = o.astype(BF16)


def _attention(q, k, v, bias, out_gain, tq):
    B, Lp, _ = q.shape
    return pl.pallas_call(
        _attn_kernel,
        out_shape=jax.ShapeDtypeStruct((B, Lp, H_A * V_DIM), BF16),
        grid=(B, H_A, Lp // tq),
        in_specs=[pl.BlockSpec((1, tq, HEAD_W), lambda b, h, i: (b, i, h)),
                  pl.BlockSpec((1, Lp, HEAD_W), lambda b, h, i: (b, 0, h)),
                  pl.BlockSpec((1, Lp, V_DIM), lambda b, h, i: (b, 0, h)),
                  pl.BlockSpec((1, Lp), lambda b, h, i: (0, 0)),
                  pl.BlockSpec((1, V_DIM), lambda b, h, i: (0, h))],
        out_specs=pl.BlockSpec((1, tq, V_DIM), lambda b, h, i: (b, i, h)),
        compiler_params=_cparams(("parallel", "parallel", "parallel")),
        name="mla_attention",
    )(q, k, v, bias, out_gain.reshape(1, H_A * V_DIM))


def _prep_kernel(*refs, first, seq_len, tm):
    if first:
        (rw_ref, pv_ref, nx_ref, mup_ref, mun_ref, w2_ref, w0_ref, a2_ref, a0_ref, g2_ref, kk_ref, ka_ref, rk_ref,
         ones_ref, r_o, kk_o, v_o, wf_o, kdf_o, bf_o, wb_o, kdb_o, bb_o, g_o, bvg_o, vfirst_o) = refs
    else:
        (rw_ref, pv_ref, nx_ref, mup_ref, mun_ref, w2_ref, w0_ref, a2_ref, a0_ref, g2_ref, kk_ref, ka_ref, rk_ref,
         ones_ref, vf_ref, v0_ref, v1_ref, v2_ref,
         r_o, kk_o, v_o, wf_o, kdf_o, bf_o, wb_o, kdb_o, bb_o, g_o, bvg_o) = refs
    i = pl.program_id(1)
    n_t = pl.num_programs(1)
    row = lax.broadcasted_iota(jnp.int32, (tm, 1), 0)
    t = i * tm + row
    valid = t < seq_len
    rw = jnp.where(valid, rw_ref[0], 0.0)
    prev_row = jnp.where((i > 0) & (i * tm - 1 < seq_len), pv_ref[0, SUBLANES - 1:SUBLANES, :], 0.0)
    next_row = jnp.where((i < n_t - 1) & ((i + 1) * tm < seq_len), nx_ref[0, 0:1, :], 0.0)
    prev = jnp.where(row == 0, prev_row, pltpu.roll(rw, 1, 0))
    nxt = jnp.where(row == tm - 1, next_row, pltpu.roll(rw, tm - 1, 0))
    u = rw + mup_ref[...] * (prev - rw) + mun_ref[...] * (nxt - rw)
    r, k, v = u[:, :RW], u[:, RW:2 * RW], u[:, 2 * RW:3 * RW]
    wd = u[:, 3 * RW:3 * RW + LANES]
    ad = u[:, 3 * RW + LANES:3 * RW + 2 * LANES]
    gd = u[:, 3 * RW + 2 * LANES:]
    ones = ones_ref[...]
    if first:
        vfirst_o[0] = v
    else:
        low = jnp.dot(v.astype(BF16), v1_ref[...], preferred_element_type=F32)
        mix = jax.nn.sigmoid(v0_ref[...] + jnp.dot(low.astype(BF16), v2_ref[...], preferred_element_type=F32))
        v = v + (vf_ref[0] - v) * mix
    g = jnp.dot(jax.nn.sigmoid(gd).astype(BF16), g2_ref[...], preferred_element_type=F32)
    kk = k * kk_ref[...]
    kk = jnp.where(valid, kk * lax.rsqrt(_seg_sum(kk * kk, ones) + 1e-12), 0.0)
    wl = w0_ref[...] + jnp.dot(jnp.tanh(wd).astype(BF16), w2_ref[...], preferred_element_type=F32)
    decay = jnp.exp(-math.exp(-0.5) * jax.nn.sigmoid(wl))
    a = jax.nn.sigmoid(a0_ref[...] + jnp.dot(ad.astype(BF16), a2_ref[...], preferred_element_type=F32))
    ka = ka_ref[...]
    kd_f = jnp.where(valid, k * (1.0 + (a[:, :RW] - 1.0) * ka), 0.0)
    kd_b = jnp.where(valid, k * (1.0 + (a[:, RW:] - 1.0) * ka), 0.0)
    bonus = _seg_sum(r * (kd_f + kd_b) * rk_ref[...], ones)
    r_o[0] = r
    kk_o[0] = kk
    v_o[0] = jnp.where(valid, v, 0.0)
    wf_o[0] = decay[:, :RW]
    wb_o[0] = decay[:, RW:]
    kdf_o[0] = kd_f
    kdb_o[0] = kd_b
    bf_o[0] = kk * a[:, :RW]
    bb_o[0] = kk * a[:, RW:]
    g_o[0] = g
    bvg_o[0] = bonus * v * g


def _block_diag2(a, b):
    z = jnp.zeros_like(a)
    return jnp.concatenate([jnp.concatenate([a, z], 1), jnp.concatenate([z, b], 1)], 0)


def _rwkv_prep(rw, seq_len, mu_prev, mu_next, w0, w2, a0, a2, g2, k_k, k_a, r_k, v_first, vres):
    B, Lp, n_rw = rw.shape
    tm = _row_tile(Lp)
    tpb = tm // SUBLANES
    first = vres is None
    vec = lambda a: a.reshape(1, -1)
    consts = [vec(mu_prev), vec(mu_next), _block_diag2(w2[0], w2[1]).astype(BF16), vec(w0),
              _block_diag2(a2[0], a2[1]).astype(BF16), vec(a0), g2.astype(BF16), vec(k_k), vec(k_a), vec(r_k),
              _seg_ones(RW)]
    tile = lambda w: pl.BlockSpec((1, tm, w), lambda b, i: (b, i, 0))
    full = lambda a: pl.BlockSpec(a.shape, lambda b, i: (0, 0))
    in_specs = [tile(n_rw),
                pl.BlockSpec((1, SUBLANES, n_rw), lambda b, i: (b, jnp.maximum(i * tpb - 1, 0), 0)),
                pl.BlockSpec((1, SUBLANES, n_rw), lambda b, i: (b, jnp.minimum((i + 1) * tpb, Lp // SUBLANES - 1), 0))]
    in_specs += [full(c) for c in consts]
    args = [rw, rw, rw] + consts
    n_out = 11
    if first:
        n_out += 1
    else:
        v0, v1, v2 = vres
        v1p = jnp.concatenate([v1, jnp.zeros((RW, LANES - VRES_LORA), F32)], 1).astype(BF16)
        v2p = jnp.concatenate([v2, jnp.zeros((LANES - VRES_LORA, RW), F32)], 0).astype(BF16)
        extra = [vec(v0), v1p, v2p]
        in_specs += [tile(RW)] + [full(c) for c in extra]
        args += [v_first] + extra
    out = pl.pallas_call(
        functools.partial(_prep_kernel, first=first, seq_len=seq_len, tm=tm),
        out_shape=tuple(jax.ShapeDtypeStruct((B, Lp, RW), F32) for _ in range(n_out)),
        grid=(B, Lp // tm),
        in_specs=in_specs,
        out_specs=tuple(tile(RW) for _ in range(n_out)),
        compiler_params=_cparams(("parallel", "parallel")),
        name="rwkv_prep",
    )(*args)
    return out


def _scan_kernel(rf, kkf, wf, kdf, bf, vf, rb, kkb, wb, kdb, bb_, vb_, ones_ref,
                 yf_ref, yb_ref, s_ref, vt_ref, acc_ref, *, nb):
    j = pl.program_id(1)
    C = nb * 4
    half_t = TIME_BLOCK // 2

    @pl.when(j == 0)
    def _():
        s_ref[...] = jnp.zeros_like(s_ref)

    lane1 = lax.broadcasted_iota(jnp.int32, (N_B, LANES), 1)
    lo_half = lane1 < N_B

    for d, vref in enumerate((vf, vb_)):
        for n in range(nb):
            for hp in range(4):
                c = n * 4 + hp
                xt = vref[n, :, hp * LANES:(hp + 1) * LANES].T
                top, bot = xt[:N_B], xt[N_B:]
                vt_ref[d, 0, c * N_B:(c + 1) * N_B, :] = jnp.where(lo_half, top, pltpu.roll(bot, N_B, 1))
                vt_ref[d, 1, c * N_B:(c + 1) * N_B, :] = jnp.where(lo_half, pltpu.roll(top, N_B, 1), bot)
    acc_ref[...] = jnp.zeros_like(acc_ref)

    ones = ones_ref[...]
    lane_id = lax.broadcasted_iota(jnp.int32, (C * N_B, LANES), 1)
    lane = lane_id % N_B
    head_base = (lane_id // N_B) * N_B
    row_refs = ((rf, kkf, wf, kdf, bf), (rb, kkb, wb, kdb, bb_))

    def step(d, tiles, row, oh, tt):
        def rows(tile):
            return jnp.concatenate(
                [jnp.broadcast_to(tile[n, row:row + 1, hp * LANES:(hp + 1) * LANES], (N_B, LANES))
                 for n in range(nb) for hp in range(4)], axis=0)

        r, kk, w, kd, b = [rows(x) for x in tiles]
        S = s_ref[d]
        sa = _seg_sum(S * kk, ones)
        vb = jnp.take_along_axis(vt_ref[d, oh], head_base + tt, axis=1, mode="promise_in_bounds")
        S2 = S * w - sa * b + vb * kd
        s_ref[d] = S2
        y = _seg_sum(S2 * r, ones)
        acc_ref[d, oh] = jnp.where(lane == tt, y, acc_ref[d, oh])

    def make_body(half):
        def body(g, carry):
            t0f = pl.multiple_of(half * half_t + g * SUBLANES, SUBLANES)
            t0b = pl.multiple_of(TIME_BLOCK - SUBLANES - (half * half_t + g * SUBLANES), SUBLANES)
            tiles_f = [x[:, pl.ds(t0f, SUBLANES), :] for x in row_refs[0]]
            tiles_b = [x[:, pl.ds(t0b, SUBLANES), :] for x in row_refs[1]]
            for i in range(SUBLANES):
                step(0, tiles_f, i, half, g * SUBLANES + i)
                step(1, tiles_b, SUBLANES - 1 - i, 1 - half, half_t - 1 - (g * SUBLANES + i))
            return carry
        return body

    lax.fori_loop(0, half_t // SUBLANES, make_body(0), 0)
    lax.fori_loop(0, half_t // SUBLANES, make_body(1), 0)

    for d, yref in enumerate((yf_ref, yb_ref)):
        for n in range(nb):
            for hp in range(4):
                c = n * 4 + hp
                a = jnp.concatenate([acc_ref[d, 0, c * N_B:(c + 1) * N_B, :],
                                     acc_ref[d, 1, c * N_B:(c + 1) * N_B, :]], axis=0).T
                top, bot = a[:N_B], a[N_B:]
                yref[n, :N_B, hp * LANES:(hp + 1) * LANES] = jnp.where(lo_half, top, pltpu.roll(bot, N_B, 1))
                yref[n, N_B:, hp * LANES:(hp + 1) * LANES] = jnp.where(lo_half, pltpu.roll(top, N_B, 1), bot)


def _wkv_bidir(r, kk, v, per_dir, nb):
    B, Lp, _ = r.shape
    nblk = Lp // TIME_BLOCK
    ones = _seg_ones(LANES)
    fwd = lambda bi, j: (bi, j, 0)
    bwd = lambda bi, j: (bi, nblk - 1 - j, 0)
    blk = (nb, TIME_BLOCK, RW)
    (wf, kdf, bf), (wb, kdb, bb_) = per_dir
    y_shape = jax.ShapeDtypeStruct((B, Lp, RW), F32)
    rows = nb * 4 * N_B
    return pl.pallas_call(
        functools.partial(_scan_kernel, nb=nb),
        out_shape=(y_shape, y_shape),
        grid=(B // nb, nblk),
        in_specs=[pl.BlockSpec(blk, fwd)] * 6 + [pl.BlockSpec(blk, bwd)] * 6
                 + [pl.BlockSpec(ones.shape, lambda bi, j: (0, 0))],
        out_specs=(pl.BlockSpec(blk, fwd), pl.BlockSpec(blk, bwd)),
        scratch_shapes=[pltpu.VMEM((2, rows, LANES), F32),
                        pltpu.VMEM((2, 2, rows, LANES), F32),
                        pltpu.VMEM((2, 2, rows, LANES), F32)],
        compiler_params=_cparams(("parallel", "arbitrary")),
        name="wkv_scan",
    )(r, kk, wf, kdf, bf, v, r, kk, wb, kdb, bb_, v, ones)


def _post_kernel(yf_ref, yb_ref, g_ref, bvg_ref, att_ref, h_ref, gng_ref, gnb_ref, wo_ref, l1g_ref, l1b_ref,
                 wrh_ref, wrl_ref, rb_ref, tri_ref, ones_ref,
                 h1_ref, rf_ref, ri_ref, cnt_ref, *, alpha, tm):
    ones = ones_ref[...]
    y = yf_ref[...] + yb_ref[...]
    mu = _seg_sum(y, ones) * (1.0 / N_B)
    yc = y - mu
    var = _seg_sum(yc * yc, ones) * (1.0 / N_B)
    yr = (yc * lax.rsqrt(var + RWKV_GN_EPS) * gng_ref[...] + gnb_ref[...]) * g_ref[...] + bvg_ref[...]
    half = H_A * V_DIM
    mixed = (jnp.dot(att_ref[...], wo_ref[:half, :], preferred_element_type=F32)
             + jnp.dot(yr.astype(BF16), wo_ref[half:, :], preferred_element_type=F32))
    h1 = _ln(h_ref[...] * alpha + mixed, l1g_ref[...], l1b_ref[...])
    h1_ref[...] = h1

    xh, xl = _split_bf16(h1)
    x = (jnp.dot(xh, wrh_ref[...], preferred_element_type=F32) + jnp.dot(xl, wrh_ref[...], preferred_element_type=F32)
         + jnp.dot(xh, wrl_ref[...], preferred_element_type=F32)) + rb_ref[...]
    lane = lax.broadcasted_iota(jnp.int32, x.shape, 1)
    lanef = lane.astype(F32)
    big = float(LANES)
    gmask = lane < N_GROUPS
    gx = jnp.where(gmask, x, NEG)
    gmax = jnp.max(gx, -1, keepdims=True)
    gidx = jnp.min(jnp.where(gx == gmax, lanef, big), -1, keepdims=True)
    gsum = jnp.sum(jnp.where(gmask, jnp.exp(gx - gmax), 0.0), -1, keepdims=True)
    lo = N_GROUPS + EXPERTS_PER_GROUP * gidx
    emask = (lanef >= lo) & (lanef < lo + EXPERTS_PER_GROUP)
    ex = jnp.where(emask, x, NEG)
    m1 = jnp.max(ex, -1, keepdims=True)
    i1 = jnp.min(jnp.where(emask & (ex == m1), lanef, big), -1, keepdims=True)
    ex2 = jnp.where(lanef == i1, NEG, ex)
    m2 = jnp.max(ex2, -1, keepdims=True)
    i2 = jnp.min(jnp.where(emask & (lanef != i1) & (ex2 == m2), lanef, big), -1, keepdims=True)
    esum = jnp.sum(jnp.where(emask, jnp.exp(ex - m1), 0.0), -1, keepdims=True)
    gp = 1.0 / gsum
    gate0 = gp * (1.0 / esum)
    gate1 = gp * (jnp.exp(m2 - m1) / esum)
    e0 = i1 - N_GROUPS
    e1 = i2 - N_GROUPS

    @pl.when(pl.program_id(0) == 0)
    def _():
        cnt_ref[...] = jnp.zeros_like(cnt_ref)

    onehot = jnp.where((lanef == e0) | (lanef == e1), 1.0, 0.0)
    prefix = jnp.dot(tri_ref[...], onehot.astype(BF16), preferred_element_type=F32) + cnt_ref[...]
    r0 = jnp.sum(jnp.where(lanef == e0, prefix, 0.0), -1, keepdims=True)
    r1 = jnp.sum(jnp.where(lanef == e1, prefix, 0.0), -1, keepdims=True)
    cnt_ref[...] += jnp.sum(onehot, 0, keepdims=True)
    rf_ref[...] = jnp.where(lane == 0, gate0, jnp.where(lane == 1, gate1, 0.0))
    ri = jnp.where(lane == 0, e0, jnp.where(lane == 1, e1, jnp.where(lane == 2, r0, jnp.where(lane == 3, r1, 0.0))))
    ri_ref[...] = ri.astype(jnp.int32)


def _post_mixer(yf, yb, g, bvg, att, h, gn_g, gn_b, w_out, ln_g, ln_b, w_group, b_group, w_expert, b_expert, alpha):
    T, D = h.shape
    tm = _row_tile(T)
    vec = lambda a: a.reshape(1, -1)
    zpad = LANES - N_GROUPS - N_EXPERTS
    wr = jnp.concatenate([w_group, w_expert, jnp.zeros((D, zpad), F32)], axis=1)
    wrh = wr.astype(BF16)
    wrl = (wr - wrh.astype(F32)).astype(BF16)
    rb = jnp.concatenate([b_group, b_expert, jnp.zeros((zpad,), F32)]).reshape(1, LANES)
    tri = jnp.asarray(np.arange(tm)[:, None] > np.arange(tm)[None, :], BF16)
    consts = [vec(gn_g), vec(gn_b), w_out.astype(BF16), vec(ln_g), vec(ln_b), wrh, wrl, rb, tri, _seg_ones(RW)]
    row = lambda w: pl.BlockSpec((tm, w), lambda i: (i, 0))
    full = lambda a: pl.BlockSpec(a.shape, lambda i: (0, 0))
    return pl.pallas_call(
        functools.partial(_post_kernel, alpha=alpha, tm=tm),
        out_shape=(jax.ShapeDtypeStruct((T, D), F32),
                   jax.ShapeDtypeStruct((T, LANES), F32),
                   jax.ShapeDtypeStruct((T, LANES), jnp.int32),
                   jax.ShapeDtypeStruct((1, LANES), F32)),
        grid=(T // tm,),
        in_specs=[row(RW)] * 4 + [row(H_A * V_DIM), row(D)] + [full(c) for c in consts],
        out_specs=(row(D), row(LANES), row(LANES), pl.BlockSpec((1, LANES), lambda i: (0, 0))),
        compiler_params=_cparams(("arbitrary",)),
        name="post_mixer",
    )(yf, yb, g, bvg, att, h, *consts)


ROUTE_COLS = 2 * TOP_K


def _row_copies(ps_ref, idx_smem, t, make):
    return [make(s, ps_ref[idx_smem[ROUTE_COLS * t + s]] + idx_smem[ROUTE_COLS * t + TOP_K + s]) for s in range(TOP_K)]


def _dispatch_kernel(ps_ref, idx_hbm, h_hbm, xz_hbm, xb_hbm, idx_smem, isem, sem, *, tm):
    del xz_hbm
    i = pl.program_id(0)
    load = pltpu.make_async_copy(idx_hbm.at[pl.ds(i * tm * ROUTE_COLS, tm * ROUTE_COLS)], idx_smem, isem)
    load.start()
    load.wait()

    def copies(t):
        src = h_hbm.at[pl.ds(i * tm + t, 1)]
        return _row_copies(ps_ref, idx_smem, t, lambda s, dst: pltpu.make_async_copy(src, xb_hbm.at[pl.ds(dst, 1)], sem))

    def start(t, c):
        for cp in copies(t):
            cp.start()
        return c

    def wait(t, c):
        for cp in copies(t):
            cp.wait()
        return c

    lax.fori_loop(0, tm, start, 0, unroll=8)
    lax.fori_loop(0, tm, wait, 0, unroll=8)


def _dispatch(pstart, idx, h, n_rows):
    T, D = h.shape
    tm = _row_tile(T)
    any_spec = pl.BlockSpec(memory_space=pl.ANY)
    grid_spec = pltpu.PrefetchScalarGridSpec(
        num_scalar_prefetch=1,
        grid=(T // tm,),
        in_specs=[any_spec, any_spec, any_spec],
        out_specs=any_spec,
        scratch_shapes=[pltpu.SMEM((tm * ROUTE_COLS,), jnp.int32), pltpu.SemaphoreType.DMA, pltpu.SemaphoreType.DMA],
    )
    return pl.pallas_call(
        functools.partial(_dispatch_kernel, tm=tm),
        out_shape=jax.ShapeDtypeStruct((n_rows, D), F32),
        grid_spec=grid_spec,
        input_output_aliases={3: 0},
        compiler_params=_cparams(("arbitrary",)),
        name="moe_dispatch",
    )(pstart, idx, h, jnp.zeros((n_rows, D), F32))


def _expert_kernel(be_ref, nu_ref, x_ref, w1_ref, w3_ref, w2_ref, o_ref):
    @pl.when(pl.program_id(0) < nu_ref[0])
    def _():
        x = x_ref[...].astype(BF16)
        a = jnp.dot(x, w1_ref[0], preferred_element_type=F32)
        b = jnp.dot(x, w3_ref[0], preferred_element_type=F32)
        hid = (a * jax.nn.sigmoid(a)) * b
        o_ref[...] = jnp.dot(hid.astype(BF16), w2_ref[0], preferred_element_type=F32)


def _expert_ffn(xb, block_e, n_used, w1, w3, w2):
    n_rows, D = xb.shape
    n_blocks = n_rows // MOE_BLOCK
    blk = lambda i, be, nu: jnp.minimum(i, nu[0] - 1)
    wmap = lambda i, be, nu: (be[blk(i, be, nu)], 0, 0)
    grid_spec = pltpu.PrefetchScalarGridSpec(
        num_scalar_prefetch=2,
        grid=(n_blocks,),
        in_specs=[pl.BlockSpec((MOE_BLOCK, D), lambda i, be, nu: (blk(i, be, nu), 0)),
                  pl.BlockSpec((1, D, E_HID), wmap),
                  pl.BlockSpec((1, D, E_HID), wmap),
                  pl.BlockSpec((1, E_HID, D), wmap)],
        out_specs=pl.BlockSpec((MOE_BLOCK, D), lambda i, be, nu: (blk(i, be, nu), 0)),
    )
    return pl.pallas_call(
        _expert_kernel,
        out_shape=jax.ShapeDtypeStruct((n_rows, D), F32),
        grid_spec=grid_spec,
        compiler_params=_cparams(("arbitrary",)),
        name="moe_experts",
    )(block_e, n_used, xb, w1, w3, w2)


def _combine_kernel(ps_ref, idx_hbm, yb_hbm, gate_ref, h_ref, g_ref, b_ref, o_ref, idx_smem, buf, isem, sem,
                    *, tm, alpha):
    i = pl.program_id(0)
    load = pltpu.make_async_copy(idx_hbm.at[pl.ds(i * tm * ROUTE_COLS, tm * ROUTE_COLS)], idx_smem, isem)
    load.start()
    load.wait()

    def copies(t):
        return _row_copies(ps_ref, idx_smem, t, lambda s, src: pltpu.make_async_copy(
            yb_hbm.at[pl.ds(src, 1)], buf.at[s, pl.ds(t, 1)], sem))

    def start(t, c):
        for cp in copies(t):
            cp.start()
        return c

    def wait(t, c):
        for cp in copies(t):
            cp.wait()
        return c

    lax.fori_loop(0, tm, start, 0, unroll=8)
    lax.fori_loop(0, tm, wait, 0, unroll=8)
    gate = gate_ref[...]
    ff = sum(buf[s] * gate[:, s:s + 1] for s in range(TOP_K))
    o_ref[...] = _ln(h_ref[...] * alpha + ff, g_ref[...], b_ref[...])


def _combine(pstart, idx, yb, route_f, h, ln_g, ln_b, alpha):
    T, D = h.shape
    tm = _row_tile(T)
    any_spec = pl.BlockSpec(memory_space=pl.ANY)
    row = lambda w: pl.BlockSpec((tm, w), lambda i, ps: (i, 0))
    vec = pl.BlockSpec((1, D), lambda i, ps: (0, 0))
    grid_spec = pltpu.PrefetchScalarGridSpec(
        num_scalar_prefetch=1,
        grid=(T // tm,),
        in_specs=[any_spec, any_spec, row(LANES), row(D), vec, vec],
        out_specs=row(D),
        scratch_shapes=[pltpu.SMEM((tm * ROUTE_COLS,), jnp.int32), pltpu.VMEM((TOP_K, tm, D), F32),
                        pltpu.SemaphoreType.DMA, pltpu.SemaphoreType.DMA],
    )
    return pl.pallas_call(
        functools.partial(_combine_kernel, tm=tm, alpha=alpha),
        out_shape=jax.ShapeDtypeStruct((T, D), F32),
        grid_spec=grid_spec,
        compiler_params=_cparams(("arbitrary",)),
        name="moe_combine",
    )(pstart, idx, yb, route_f, h, ln_g.reshape(1, D), ln_b.reshape(1, D))


def _hier_moe(h, route_f, route_i, counts, w1, w3, w2, ln_g, ln_b, alpha):
    T, D = h.shape
    counts = counts[0, :N_EXPERTS].astype(jnp.int32)
    padded = (counts + MOE_BLOCK - 1) // MOE_BLOCK * MOE_BLOCK
    pends = jnp.cumsum(padded)
    pstart = pends - padded
    n_rows = (T * TOP_K + N_EXPERTS * (MOE_BLOCK - 1) + MOE_BLOCK - 1) // MOE_BLOCK * MOE_BLOCK
    n_blocks = n_rows // MOE_BLOCK
    block_e = jnp.clip(jnp.searchsorted(pends, jnp.arange(n_blocks) * MOE_BLOCK, side='right'),
                       0, N_EXPERTS - 1).astype(jnp.int32)
    n_used = (pends[-1:] // MOE_BLOCK).astype(jnp.int32)
    idx = route_i[:, :ROUTE_COLS].reshape(-1)
    xb = _dispatch(pstart, idx, h, n_rows)
    yb = _expert_ffn(xb, block_e, n_used, w1, w3, w2)
    return _combine(pstart, idx, yb, route_f, h, ln_g, ln_b, alpha)


def kernel(x, positions, meta_tokens, emb_ln_g, emb_ln_b, w_in, mla_q_norm, mla_kv_norm, mla_w_uq, mla_w_ukv, mla_out_norm, rwkv_mu_prev, rwkv_mu_next, rwkv_w0, rwkv_w2, rwkv_a0, rwkv_a2, rwkv_g2, rwkv_k_k, rwkv_k_a, rwkv_r_k, rwkv_gn_g, rwkv_gn_b, rwkv_v0, rwkv_v1, rwkv_v2, w_out, ln1_g, ln1_b, moe_w_group, moe_b_group, moe_w_expert, moe_b_expert, moe_w1, moe_w3, moe_w2, ln2_g, ln2_b):
    B, seq, D = x.shape
    depth = w_in.shape[0]
    alpha = (2 * depth) ** 0.25
    L = seq + N_META
    Lp = -(-L // LANES) * LANES
    T = B * Lp
    nb = 2 if B % 2 == 0 else 1
    tq = 384 if Lp % 384 == 0 else LANES

    meta = jnp.broadcast_to(meta_tokens.astype(x.dtype)[None], (B, N_META, D))
    h = jnp.concatenate([meta, x, jnp.zeros((B, Lp - L, D), x.dtype)], axis=1).reshape(T, D)
    h = _ln_residual(h, jnp.zeros_like(h), emb_ln_g, emb_ln_b, 1.0)

    key_bias = jnp.where(jnp.arange(Lp) < L, 0.0, NEG).astype(F32)[None, :]
    pos = jnp.concatenate([jnp.broadcast_to(jnp.arange(N_META, dtype=jnp.int32), (B, N_META)),
                           positions + N_META, jnp.zeros((B, Lp - L), jnp.int32)], axis=1)
    inv_freq = ROPE_THETA ** (-jnp.arange(0, QK_ROPE, 2, dtype=F32) / QK_ROPE)
    ang = pos.astype(F32)[..., None] * inv_freq
    cos, sin = jnp.cos(ang).reshape(T, -1), jnp.sin(ang).reshape(T, -1)
    zeros = jnp.zeros((T, LANES - QK_ROPE), F32)
    rope_cos = jnp.concatenate([cos, cos, zeros], axis=1)
    rope_sin = jnp.concatenate([-sin, sin, zeros], axis=1)
    qscale = (QK_NOPE + QK_ROPE) ** -0.5 * math.log2(math.e)

    v_first = None
    for li in range(depth):
        q, k, v, rw = _project(h, w_in[li], mla_q_norm[li], mla_kv_norm[li], mla_w_uq[li], mla_w_ukv[li],
                               rope_cos, rope_sin, qscale)
        y_att = _attention(q.reshape(B, Lp, -1), k.reshape(B, Lp, -1), v.reshape(B, Lp, -1),
                           key_bias, mla_out_norm[li], tq)
        vres = None if li == 0 else (rwkv_v0[li - 1], rwkv_v1[li - 1], rwkv_v2[li - 1])
        outs = _rwkv_prep(rw.reshape(B, Lp, -1), L, rwkv_mu_prev[li], rwkv_mu_next[li], rwkv_w0[li], rwkv_w2[li],
                          rwkv_a0[li], rwkv_a2[li], rwkv_g2[li], rwkv_k_k[li], rwkv_k_a[li], rwkv_r_k[li],
                          v_first, vres)
        r_, kk, vm, wf, kdf, bf, wb, kdb, bb_, g, bvg = outs[:11]
        if li == 0:
            v_first = outs[11]
        y_f, y_b = _wkv_bidir(r_, kk, vm, ((wf, kdf, bf), (wb, kdb, bb_)), nb)
        flat = lambda a: a.reshape(T, -1)
        h, route_f, route_i, counts = _post_mixer(
            flat(y_f), flat(y_b), flat(g), flat(bvg), flat(y_att), h, rwkv_gn_g[li], rwkv_gn_b[li], w_out[li],
            ln1_g[li], ln1_b[li], moe_w_group[li], moe_b_group[li], moe_w_expert[li], moe_b_expert[li], alpha)
        h = _hier_moe(h, route_f, route_i, counts,
                      moe_w1[li].astype(BF16), moe_w3[li].astype(BF16), moe_w2[li].astype(BF16),
                      ln2_g[li], ln2_b[li], alpha)
    return h.reshape(B, Lp, D)[:, N_META:L]
```

```python
import functools
import math

import numpy as np
import jax
import jax.numpy as jnp
from jax import lax
from jax.experimental import pallas as pl
from jax.experimental.pallas import tpu as pltpu

F32 = jnp.float32
BF16 = jnp.bfloat16

N_META = 16
H_A = 4
QK_NOPE = 128
QK_ROPE = 64
V_DIM = 128
Q_LORA = 256
KV_LORA = 128
ROPE_THETA = 10000.0
H_B = 8
N_B = 64
RW = H_B * N_B
DECAY_LORA = 64
AAA_LORA = 64
GATE_LORA = 128
VRES_LORA = 32
RWKV_GN_EPS = 64e-5
MLA_COLS = Q_LORA + KV_LORA + QK_ROPE
N_GROUPS = 4
EXPERTS_PER_GROUP = 8
N_EXPERTS = N_GROUPS * EXPERTS_PER_GROUP
TOP_K = 2
E_HID = 256
MOE_BLOCK = 128
LN_EPS = 1e-5
RMS_EPS = 1e-6

LANES = 128
SUBLANES = 8
TIME_BLOCK = 128
ROW_TILE = 512
VMEM_LIMIT = 56 * 1024 * 1024
HEAD_W = 2 * LANES
NEG = -1e30


def _cparams(sem):
    return pltpu.CompilerParams(dimension_semantics=sem, vmem_limit_bytes=VMEM_LIMIT)


def _row_tile(m):
    return next(t for t in (ROW_TILE, 384, 256, LANES) if m % t == 0)


def _split_bf16(x):
    hi = x.astype(BF16)
    lo = (x - hi.astype(F32)).astype(BF16)
    return hi, lo


def _seg_sum(x, ones):
    hi, lo = _split_bf16(x)
    return jnp.dot(jnp.concatenate([hi, lo], axis=-1), ones, preferred_element_type=F32)


def _seg_ones(width):
    m = np.arange(2 * width)[:, None] % width
    n = np.arange(width)[None, :]
    return jnp.asarray(m // N_B == n // N_B, BF16)


def _ln(x, g, b):
    mu = jnp.mean(x, -1, keepdims=True)
    xc = x - mu
    var = jnp.mean(xc * xc, -1, keepdims=True)
    return xc * lax.rsqrt(var + LN_EPS) * g + b


def _ln_kernel(x_ref, r_ref, g_ref, b_ref, o_ref, *, alpha):
    o_ref[...] = _ln(x_ref[...] * alpha + r_ref[...], g_ref[...], b_ref[...])


def _ln_residual(x, r, g, b, alpha):
    M, D = x.shape
    tm = _row_tile(M)
    row = pl.BlockSpec((tm, D), lambda i: (i, 0))
    vec = pl.BlockSpec((1, D), lambda i: (0, 0))
    return pl.pallas_call(
        functools.partial(_ln_kernel, alpha=alpha),
        out_shape=jax.ShapeDtypeStruct((M, D), F32),
        grid=(M // tm,),
        in_specs=[row, row, vec, vec],
        out_specs=row,
        compiler_params=_cparams(("parallel",)),
        name="layer_norm",
    )(x, r, g.reshape(1, D), b.reshape(1, D))


def _rms(x, g):
    return x * lax.rsqrt(jnp.mean(x * x, -1, keepdims=True) + RMS_EPS) * g


def _proj_kernel(h_ref, win_ref, qn_ref, kvn_ref, wuq_ref, wukv_ref, c_ref, s_ref,
                 q_ref, k_ref, v_ref, rw_ref, *, qscale):
    proj = jnp.dot(h_ref[...].astype(BF16), win_ref[...], preferred_element_type=F32)
    rw_ref[...] = proj[:, MLA_COLS + N_B:]
    q = jnp.dot(_rms(proj[:, :Q_LORA], qn_ref[...]).astype(BF16), wuq_ref[...], preferred_element_type=F32)
    kv = jnp.dot(_rms(proj[:, Q_LORA:Q_LORA + KV_LORA], kvn_ref[...]).astype(BF16), wukv_ref[...],
                 preferred_element_type=F32)
    cos, sin = c_ref[...], s_ref[...]
    lane = lax.broadcasted_iota(jnp.int32, cos.shape, 1)
    half = QK_ROPE // 2

    def rope(x):
        partner = jnp.where(lane < half, pltpu.roll(x, LANES - half, 1), pltpu.roll(x, half, 1))
        return x * cos + partner * sin

    k_pe = rope(proj[:, Q_LORA + KV_LORA:Q_LORA + KV_LORA + LANES]).astype(BF16)
    for h in range(H_A):
        o = h * HEAD_W
        q_ref[:, o:o + LANES] = (q[:, o:o + LANES] * qscale).astype(BF16)
        q_ref[:, o + LANES:o + HEAD_W] = (rope(q[:, o + LANES:o + HEAD_W]) * qscale).astype(BF16)
        k_ref[:, o:o + LANES] = kv[:, o:o + LANES].astype(BF16)
        k_ref[:, o + LANES:o + HEAD_W] = k_pe
        v_ref[:, h * V_DIM:(h + 1) * V_DIM] = kv[:, o + LANES:o + HEAD_W].astype(BF16)


def _project(h, w_in, q_norm, kv_norm, w_uq, w_ukv, rope_cos, rope_sin, qscale):
    T, D = h.shape
    tm = _row_tile(T)
    n_rw = w_in.shape[1] - MLA_COLS
    pad = jnp.zeros((D, N_B), F32)
    win = jnp.concatenate([w_in[:, :MLA_COLS], pad, w_in[:, MLA_COLS:]], axis=1).astype(BF16)
    wq = w_uq.reshape(Q_LORA, H_A, QK_NOPE + QK_ROPE)
    wq = jnp.concatenate([wq, jnp.zeros((Q_LORA, H_A, HEAD_W - QK_NOPE - QK_ROPE), F32)], axis=-1)
    wq = wq.reshape(Q_LORA, H_A * HEAD_W).astype(BF16)
    row = lambda w: pl.BlockSpec((tm, w), lambda i: (i, 0))
    full = lambda a: pl.BlockSpec(a.shape, lambda i: (0, 0))
    args = (h, win, q_norm.reshape(1, -1), kv_norm.reshape(1, -1), wq, w_ukv.astype(BF16), rope_cos, rope_sin)
    return pl.pallas_call(
        functools.partial(_proj_kernel, qscale=qscale),
        out_shape=(jax.ShapeDtypeStruct((T, H_A * HEAD_W), BF16),
                   jax.ShapeDtypeStruct((T, H_A * HEAD_W), BF16),
                   jax.ShapeDtypeStruct((T, H_A * V_DIM), BF16),
                   jax.ShapeDtypeStruct((T, n_rw), F32)),
        grid=(T // tm,),
        in_specs=[row(D)] + [full(a) for a in args[1:6]] + [row(LANES), row(LANES)],
        out_specs=(row(H_A * HEAD_W), row(H_A * HEAD_W), row(H_A * V_DIM), row(n_rw)),
        compiler_params=_cparams(("parallel",)),
        name="in_proj",
    )(*args)


def _attn_kernel(q_ref, k_ref, v_ref, bias_ref, g_ref, o_ref):
    s = lax.dot_general(q_ref[0], k_ref[0], (((1,), (1,)), ((), ())), preferred_element_type=F32)
    s = s + bias_ref[...]
    m = jnp.max(s, -1, keepdims=True)
    p = jnp.exp2(s - m)
    l = jnp.sum(p, -1, keepdims=True)
    o = jnp.dot(p.astype(BF16), v_ref[0], preferred_element_type=F32) / l
    o = o * lax.rsqrt(jnp.mean(o * o, -1, keepdims=True) + RMS_EPS) * g_ref[...]
    o_ref[0] = o.astype(BF16)


def _attention(q, k, v, bias, out_gain, tq):
    B, Lp, _ = q.shape
    return pl.pallas_call(
        _attn_kernel,
        out_shape=jax.ShapeDtypeStruct((B, Lp, H_A * V_DIM), BF16),
        grid=(B, H_A, Lp // tq),
        in_specs=[pl.BlockSpec((1, tq, HEAD_W), lambda b, h, i: (b, i, h)),
                  pl.BlockSpec((1, Lp, HEAD_W), lambda b, h, i: (b, 0, h)),
                  pl.BlockSpec((1, Lp, V_DIM), lambda b, h, i: (b, 0, h)),
                  pl.BlockSpec((1, Lp), lambda b, h, i: (0, 0)),
                  pl.BlockSpec((1, V_DIM), lambda b, h, i: (0, h))],
        out_specs=pl.BlockSpec((1, tq, V_DIM), lambda b, h, i: (b, i, h)),
        compiler_params=_cparams(("parallel", "parallel", "parallel")),
        name="mla_attention",
    )(q, k, v, bias, out_gain.reshape(1, H_A * V_DIM))


def _prep_kernel(*refs, first, seq_len, tm):
    if first:
        (rw_ref, pv_ref, nx_ref, mup_ref, mun_ref, w2_ref, w0_ref, a2_ref, a0_ref, g2_ref, kk_ref, ka_ref, rk_ref,
         ones_ref, r_o, kk_o, v_o, wf_o, kdf_o, bf_o, wb_o, kdb_o, bb_o, g_o, bvg_o, vfirst_o) = refs
    else:
        (rw_ref, pv_ref, nx_ref, mup_ref, mun_ref, w2_ref, w0_ref, a2_ref, a0_ref, g2_ref, kk_ref, ka_ref, rk_ref,
         ones_ref, vf_ref, v0_ref, v1_ref, v2_ref,
         r_o, kk_o, v_o, wf_o, kdf_o, bf_o, wb_o, kdb_o, bb_o, g_o, bvg_o) = refs
    i = pl.program_id(1)
    n_t = pl.num_programs(1)
    row = lax.broadcasted_iota(jnp.int32, (tm, 1), 0)
    t = i * tm + row
    valid = t < seq_len
    rw = jnp.where(valid, rw_ref[0], 0.0)
    prev_row = jnp.where((i > 0) & (i * tm - 1 < seq_len), pv_ref[0, SUBLANES - 1:SUBLANES, :], 0.0)
    next_row = jnp.where((i < n_t - 1) & ((i + 1) * tm < seq_len), nx_ref[0, 0:1, :], 0.0)
    prev = jnp.where(row == 0, prev_row, pltpu.roll(rw, 1, 0))
    nxt = jnp.where(row == tm - 1, next_row, pltpu.roll(rw, tm - 1, 0))
    u = rw + mup_ref[...] * (prev - rw) + mun_ref[...] * (nxt - rw)
    r, k, v = u[:, :RW], u[:, RW:2 * RW], u[:, 2 * RW:3 * RW]
    wd = u[:, 3 * RW:3 * RW + LANES]
    ad = u[:, 3 * RW + LANES:3 * RW + 2 * LANES]
    gd = u[:, 3 * RW + 2 * LANES:]
    ones = ones_ref[...]
    if first:
        vfirst_o[0] = v
    else:
        low = jnp.dot(v.astype(BF16), v1_ref[...], preferred_element_type=F32)
        mix = jax.nn.sigmoid(v0_ref[...] + jnp.dot(low.astype(BF16), v2_ref[...], preferred_element_type=F32))
        v = v + (vf_ref[0] - v) * mix
    g = jnp.dot(jax.nn.sigmoid(gd).astype(BF16), g2_ref[...], preferred_element_type=F32)
    kk = k * kk_ref[...]
    kk = jnp.where(valid, kk * lax.rsqrt(_seg_sum(kk * kk, ones) + 1e-12), 0.0)
    wl = w0_ref[...] + jnp.dot(jnp.tanh(wd).astype(BF16), w2_ref[...], preferred_element_type=F32)
    decay = jnp.exp(-math.exp(-0.5) * jax.nn.sigmoid(wl))
    a = jax.nn.sigmoid(a0_ref[...] + jnp.dot(ad.astype(BF16), a2_ref[...], preferred_element_type=F32))
    ka = ka_ref[...]
    kd_f = jnp.where(valid, k * (1.0 + (a[:, :RW] - 1.0) * ka), 0.0)
    kd_b = jnp.where(valid, k * (1.0 + (a[:, RW:] - 1.0) * ka), 0.0)
    bonus = _seg_sum(r * (kd_f + kd_b) * rk_ref[...], ones)
    r_o[0] = r
    kk_o[0] = kk
    v_o[0] = jnp.where(valid, v, 0.0)
    wf_o[0] = decay[:, :RW]
    wb_o[0] = decay[:, RW:]
    kdf_o[0] = kd_f
    kdb_o[0] = kd_b
    bf_o[0] = kk * a[:, :RW]
    bb_o[0] = kk * a[:, RW:]
    g_o[0] = g
    bvg_o[0] = bonus * v * g


def _block_diag2(a, b):
    z = jnp.zeros_like(a)
    return jnp.concatenate([jnp.concatenate([a, z], 1), jnp.concatenate([z, b], 1)], 0)


def _rwkv_prep(rw, seq_len, mu_prev, mu_next, w0, w2, a0, a2, g2, k_k, k_a, r_k, v_first, vres):
    B, Lp, n_rw = rw.shape
    tm = _row_tile(Lp)
    tpb = tm // SUBLANES
    first = vres is None
    vec = lambda a: a.reshape(1, -1)
    consts = [vec(mu_prev), vec(mu_next), _block_diag2(w2[0], w2[1]).astype(BF16), vec(w0),
              _block_diag2(a2[0], a2[1]).astype(BF16), vec(a0), g2.astype(BF16), vec(k_k), vec(k_a), vec(r_k),
              _seg_ones(RW)]
    tile = lambda w: pl.BlockSpec((1, tm, w), lambda b, i: (b, i, 0))
    full = lambda a: pl.BlockSpec(a.shape, lambda b, i: (0, 0))
    in_specs = [tile(n_rw),
                pl.BlockSpec((1, SUBLANES, n_rw), lambda b, i: (b, jnp.maximum(i * tpb - 1, 0), 0)),
                pl.BlockSpec((1, SUBLANES, n_rw), lambda b, i: (b, jnp.minimum((i + 1) * tpb, Lp // SUBLANES - 1), 0))]
    in_specs += [full(c) for c in consts]
    args = [rw, rw, rw] + consts
    n_out = 11
    if first:
        n_out += 1
    else:
        v0, v1, v2 = vres
        v1p = jnp.concatenate([v1, jnp.zeros((RW, LANES - VRES_LORA), F32)], 1).astype(BF16)
        v2p = jnp.concatenate([v2, jnp.zeros((LANES - VRES_LORA, RW), F32)], 0).astype(BF16)
        extra = [vec(v0), v1p, v2p]
        in_specs += [tile(RW)] + [full(c) for c in extra]
        args += [v_first] + extra
    out = pl.pallas_call(
        functools.partial(_prep_kernel, first=first, seq_len=seq_len, tm=tm),
        out_shape=tuple(jax.ShapeDtypeStruct((B, Lp, RW), F32) for _ in range(n_out)),
        grid=(B, Lp // tm),
        in_specs=in_specs,
        out_specs=tuple(tile(RW) for _ in range(n_out)),
        compiler_params=_cparams(("parallel", "parallel")),
        name="rwkv_prep",
    )(*args)
    return out


def _scan_kernel(rf, kkf, wf, kdf, bf, vf, rb, kkb, wb, kdb, bb_, vb_, ones_ref,
                 yf_ref, yb_ref, s_ref, vt_ref, acc_ref, *, nb):
    j = pl.program_id(1)
    C = nb * 4
    half_t = TIME_BLOCK // 2

    @pl.when(j == 0)
    def _():
        s_ref[...] = jnp.zeros_like(s_ref)

    lane1 = lax.broadcasted_iota(jnp.int32, (N_B, LANES), 1)
    lo_half = lane1 < N_B

    for d, vref in enumerate((vf, vb_)):
        for n in range(nb):
            for hp in range(4):
                c = n * 4 + hp
                xt = vref[n, :, hp * LANES:(hp + 1) * LANES].T
                top, bot = xt[:N_B], xt[N_B:]
                vt_ref[d, 0, c * N_B:(c + 1) * N_B, :] = jnp.where(lo_half, top, pltpu.roll(bot, N_B, 1))
                vt_ref[d, 1, c * N_B:(c + 1) * N_B, :] = jnp.where(lo_half, pltpu.roll(top, N_B, 1), bot)
    acc_ref[...] = jnp.zeros_like(acc_ref)

    ones = ones_ref[...]
    lane_id = lax.broadcasted_iota(jnp.int32, (C * N_B, LANES), 1)
    lane = lane_id % N_B
    head_base = (lane_id // N_B) * N_B
    row_refs = ((rf, kkf, wf, kdf, bf), (rb, kkb, wb, kdb, bb_))

    def step(d, tiles, row, oh, tt):
        def rows(tile):
            return jnp.concatenate(
                [jnp.broadcast_to(tile[n, row:row + 1, hp * LANES:(hp + 1) * LANES], (N_B, LANES))
                 for n in range(nb) for hp in range(4)], axis=0)

        r, kk, w, kd, b = [rows(x) for x in tiles]
        S = s_ref[d]
        sa = _seg_sum(S * kk, ones)
        vb = jnp.take_along_axis(vt_ref[d, oh], head_base + tt, axis=1, mode="promise_in_bounds")
        S2 = S * w - sa * b + vb * kd
        s_ref[d] = S2
        y = _seg_sum(S2 * r, ones)
        acc_ref[d, oh] = jnp.where(lane == tt, y, acc_ref[d, oh])

    def make_body(half):
        def body(g, carry):
            t0f = pl.multiple_of(half * half_t + g * SUBLANES, SUBLANES)
            t0b = pl.multiple_of(TIME_BLOCK - SUBLANES - (half * half_t + g * SUBLANES), SUBLANES)
            tiles_f = [x[:, pl.ds(t0f, SUBLANES), :] for x in row_refs[0]]
            tiles_b = [x[:, pl.ds(t0b, SUBLANES), :] for x in row_refs[1]]
            for i in range(SUBLANES):
                step(0, tiles_f, i, half, g * SUBLANES + i)
                step(1, tiles_b, SUBLANES - 1 - i, 1 - half, half_t - 1 - (g * SUBLANES + i))
            return carry
        return body

    lax.fori_loop(0, half_t // SUBLANES, make_body(0), 0)
    lax.fori_loop(0, half_t // SUBLANES, make_body(1), 0)

    for d, yref in enumerate((yf_ref, yb_ref)):
        for n in range(nb):
            for hp in range(4):
                c = n * 4 + hp
                a = jnp.concatenate([acc_ref[d, 0, c * N_B:(c + 1) * N_B, :],
                                     acc_ref[d, 1, c * N_B:(c + 1) * N_B, :]], axis=0).T
                top, bot = a[:N_B], a[N_B:]
                yref[n, :N_B, hp * LANES:(hp + 1) * LANES] = jnp.where(lo_half, top, pltpu.roll(bot, N_B, 1))
                yref[n, N_B:, hp * LANES:(hp + 1) * LANES] = jnp.where(lo_half, pltpu.roll(top, N_B, 1), bot)


def _wkv_bidir(r, kk, v, per_dir, nb):
    B, Lp, _ = r.shape
    nblk = Lp // TIME_BLOCK
    ones = _seg_ones(LANES)
    fwd = lambda bi, j: (bi, j, 0)
    bwd = lambda bi, j: (bi, nblk - 1 - j, 0)
    blk = (nb, TIME_BLOCK, RW)
    (wf, kdf, bf), (wb, kdb, bb_) = per_dir
    y_shape = jax.ShapeDtypeStruct((B, Lp, RW), F32)
    rows = nb * 4 * N_B
    return pl.pallas_call(
        functools.partial(_scan_kernel, nb=nb),
        out_shape=(y_shape, y_shape),
        grid=(B // nb, nblk),
        in_specs=[pl.BlockSpec(blk, fwd)] * 6 + [pl.BlockSpec(blk, bwd)] * 6
                 + [pl.BlockSpec(ones.shape, lambda bi, j: (0, 0))],
        out_specs=(pl.BlockSpec(blk, fwd), pl.BlockSpec(blk, bwd)),
        scratch_shapes=[pltpu.VMEM((2, rows, LANES), F32),
                        pltpu.VMEM((2, 2, rows, LANES), F32),
                        pltpu.VMEM((2, 2, rows, LANES), F32)],
        compiler_params=_cparams(("parallel", "arbitrary")),
        name="wkv_scan",
    )(r, kk, wf, kdf, bf, v, r, kk, wb, kdb, bb_, v, ones)


def _post_kernel(yf_ref, yb_ref, g_ref, bvg_ref, att_ref, h_ref, gng_ref, gnb_ref, wo_ref, l1g_ref, l1b_ref,
                 wrh_ref, wrl_ref, rb_ref, tri_ref, ones_ref,
                 h1_ref, rf_ref, ri_ref, cnt_ref, *, alpha, tm):
    ones = ones_ref[...]
    y = yf_ref[...] + yb_ref[...]
    mu = _seg_sum(y, ones) * (1.0 / N_B)
    yc = y - mu
    var = _seg_sum(yc * yc, ones) * (1.0 / N_B)
    yr = (yc * lax.rsqrt(var + RWKV_GN_EPS) * gng_ref[...] + gnb_ref[...]) * g_ref[...] + bvg_ref[...]
    half = H_A * V_DIM
    mixed = (jnp.dot(att_ref[...], wo_ref[:half, :], preferred_element_type=F32)
             + jnp.dot(yr.astype(BF16), wo_ref[half:, :], preferred_element_type=F32))
    h1 = _ln(h_ref[...] * alpha + mixed, l1g_ref[...], l1b_ref[...])
    h1_ref[...] = h1

    xh, xl = _split_bf16(h1)
    x = (jnp.dot(xh, wrh_ref[...], preferred_element_type=F32) + jnp.dot(xl, wrh_ref[...], preferred_element_type=F32)
         + jnp.dot(xh, wrl_ref[...], preferred_element_type=F32)) + rb_ref[...]
    lane = lax.broadcasted_iota(jnp.int32, x.shape, 1)
    lanef = lane.astype(F32)
    big = float(LANES)
    gmask = lane < N_GROUPS
    gx = jnp.where(gmask, x, NEG)
    gmax = jnp.max(gx, -1, keepdims=True)
    gidx = jnp.min(jnp.where(gx == gmax, lanef, big), -1, keepdims=True)
    gsum = jnp.sum(jnp.where(gmask, jnp.exp(gx - gmax), 0.0), -1, keepdims=True)
    lo = N_GROUPS + EXPERTS_PER_GROUP * gidx
    emask = (lanef >= lo) & (lanef < lo + EXPERTS_PER_GROUP)
    ex = jnp.where(emask, x, NEG)
    m1 = jnp.max(ex, -1, keepdims=True)
    i1 = jnp.min(jnp.where(emask & (ex == m1), lanef, big), -1, keepdims=True)
    ex2 = jnp.where(lanef == i1, NEG, ex)
    m2 = jnp.max(ex2, -1, keepdims=True)
    i2 = jnp.min(jnp.where(emask & (lanef != i1) & (ex2 == m2), lanef, big), -1, keepdims=True)
    esum = jnp.sum(jnp.where(emask, jnp.exp(ex - m1), 0.0), -1, keepdims=True)
    gp = 1.0 / gsum
    gate0 = gp * (1.0 / esum)
    gate1 = gp * (jnp.exp(m2 - m1) / esum)
    e0 = i1 - N_GROUPS
    e1 = i2 - N_GROUPS

    @pl.when(pl.program_id(0) == 0)
    def _():
        cnt_ref[...] = jnp.zeros_like(cnt_ref)

    onehot = jnp.where((lanef == e0) | (lanef == e1), 1.0, 0.0)
    prefix = jnp.dot(tri_ref[...], onehot.astype(BF16), preferred_element_type=F32) + cnt_ref[...]
    r0 = jnp.sum(jnp.where(lanef == e0, prefix, 0.0), -1, keepdims=True)
    r1 = jnp.sum(jnp.where(lanef == e1, prefix, 0.0), -1, keepdims=True)
    cnt_ref[...] += jnp.sum(onehot, 0, keepdims=True)
    rf_ref[...] = jnp.where(lane == 0, gate0, jnp.where(lane == 1, gate1, 0.0))
    ri = jnp.where(lane == 0, e0, jnp.where(lane == 1, e1, jnp.where(lane == 2, r0, jnp.where(lane == 3, r1, 0.0))))
    ri_ref[...] = ri.astype(jnp.int32)


def _post_mixer(yf, yb, g, bvg, att, h, gn_g, gn_b, w_out, ln_g, ln_b, w_group, b_group, w_expert, b_expert, alpha):
    T, D = h.shape
    tm = _row_tile(T)
    vec = lambda a: a.reshape(1, -1)
    zpad = LANES - N_GROUPS - N_EXPERTS
    wr = jnp.concatenate([w_group, w_expert, jnp.zeros((D, zpad), F32)], axis=1)
    wrh = wr.astype(BF16)
    wrl = (wr - wrh.astype(F32)).astype(BF16)
    rb = jnp.concatenate([b_group, b_expert, jnp.zeros((zpad,), F32)]).reshape(1, LANES)
    tri = jnp.asarray(np.arange(tm)[:, None] > np.arange(tm)[None, :], BF16)
    consts = [vec(gn_g), vec(gn_b), w_out.astype(BF16), vec(ln_g), vec(ln_b), wrh, wrl, rb, tri, _seg_ones(RW)]
    row = lambda w: pl.BlockSpec((tm, w), lambda i: (i, 0))
    full = lambda a: pl.BlockSpec(a.shape, lambda i: (0, 0))
    return pl.pallas_call(
        functools.partial(_post_kernel, alpha=alpha, tm=tm),
        out_shape=(jax.ShapeDtypeStruct((T, D), F32),
                   jax.ShapeDtypeStruct((T, LANES), F32),
                   jax.ShapeDtypeStruct((T, LANES), jnp.int32),
                   jax.ShapeDtypeStruct((1, LANES), F32)),
        grid=(T // tm,),
        in_specs=[row(RW)] * 4 + [row(H_A * V_DIM), row(D)] + [full(c) for c in consts],
        out_specs=(row(D), row(LANES), row(LANES), pl.BlockSpec((1, LANES), lambda i: (0, 0))),
        compiler_params=_cparams(("arbitrary",)),
        name="post_mixer",
    )(yf, yb, g, bvg, att, h, *consts)


ROUTE_COLS = 2 * TOP_K


def _row_copies(ps_ref, idx_smem, t, make):
    return [make(s, ps_ref[idx_smem[ROUTE_COLS * t + s]] + idx_smem[ROUTE_COLS * t + TOP_K + s]) for s in range(TOP_K)]


def _dispatch_kernel(ps_ref, idx_hbm, h_ref, xz_hbm, xb_hbm, idx_smem, isem, sem, *, tm):
    del xz_hbm
    i = pl.program_id(0)
    load = pltpu.make_async_copy(idx_hbm.at[pl.ds(i * tm * ROUTE_COLS, tm * ROUTE_COLS)], idx_smem, isem)
    load.start()
    load.wait()

    def copies(t):
        src = h_ref.at[pl.ds(t, 1)]
        return _row_copies(ps_ref, idx_smem, t, lambda s, dst: pltpu.make_async_copy(src, xb_hbm.at[pl.ds(dst, 1)], sem))

    def start(t, c):
        for cp in copies(t):
            cp.start()
        return c

    def wait(t, c):
        for cp in copies(t):
            cp.wait()
        return c

    lax.fori_loop(0, tm, start, 0, unroll=8)
    lax.fori_loop(0, tm, wait, 0, unroll=8)


def _dispatch(pstart, idx, h, n_rows):
    T, D = h.shape
    tm = _row_tile(T)
    any_spec = pl.BlockSpec(memory_space=pl.ANY)
    grid_spec = pltpu.PrefetchScalarGridSpec(
        num_scalar_prefetch=1,
        grid=(T // tm,),
        in_specs=[any_spec, pl.BlockSpec((tm, D), lambda i, ps: (i, 0)), any_spec],
        out_specs=any_spec,
        scratch_shapes=[pltpu.SMEM((tm * ROUTE_COLS,), jnp.int32), pltpu.SemaphoreType.DMA, pltpu.SemaphoreType.DMA],
    )
    return pl.pallas_call(
        functools.partial(_dispatch_kernel, tm=tm),
        out_shape=jax.ShapeDtypeStruct((n_rows, D), F32),
        grid_spec=grid_spec,
        input_output_aliases={3: 0},
        compiler_params=_cparams(("arbitrary",)),
        name="moe_dispatch",
    )(pstart, idx, h, jnp.zeros((n_rows, D), F32))


def _expert_kernel(be_ref, nu_ref, x_ref, w1_ref, w3_ref, w2_ref, o_ref):
    @pl.when(pl.program_id(0) < nu_ref[0])
    def _():
        x = x_ref[...].astype(BF16)
        a = jnp.dot(x, w1_ref[0], preferred_element_type=F32)
        b = jnp.dot(x, w3_ref[0], preferred_element_type=F32)
        hid = (a * jax.nn.sigmoid(a)) * b
        o_ref[...] = jnp.dot(hid.astype(BF16), w2_ref[0], preferred_element_type=F32)


def _expert_ffn(xb, block_e, n_used, w1, w3, w2):
    n_rows, D = xb.shape
    n_blocks = n_rows // MOE_BLOCK
    blk = lambda i, be, nu: jnp.minimum(i, nu[0] - 1)
    wmap = lambda i, be, nu: (be[blk(i, be, nu)], 0, 0)
    grid_spec = pltpu.PrefetchScalarGridSpec(
        num_scalar_prefetch=2,
        grid=(n_blocks,),
        in_specs=[pl.BlockSpec((MOE_BLOCK, D), lambda i, be, nu: (blk(i, be, nu), 0)),
                  pl.BlockSpec((1, D, E_HID), wmap),
                  pl.BlockSpec((1, D, E_HID), wmap),
                  pl.BlockSpec((1, E_HID, D), wmap)],
        out_specs=pl.BlockSpec((MOE_BLOCK, D), lambda i, be, nu: (blk(i, be, nu), 0)),
    )
    return pl.pallas_call(
        _expert_kernel,
        out_shape=jax.ShapeDtypeStruct((n_rows, D), F32),
        grid_spec=grid_spec,
        compiler_params=_cparams(("arbitrary",)),
        name="moe_experts",
    )(block_e, n_used, xb, w1, w3, w2)


def _combine_kernel(ps_ref, idx_hbm, yb_hbm, gate_ref, h_ref, g_ref, b_ref, o_ref, idx_smem, buf, isem, sem,
                    *, tm, alpha):
    i = pl.program_id(0)
    load = pltpu.make_async_copy(idx_hbm.at[pl.ds(i * tm * ROUTE_COLS, tm * ROUTE_COLS)], idx_smem, isem)
    load.start()
    load.wait()

    def copies(t):
        return _row_copies(ps_ref, idx_smem, t, lambda s, src: pltpu.make_async_copy(
            yb_hbm.at[pl.ds(src, 1)], buf.at[s, pl.ds(t, 1)], sem))

    def start(t, c):
        for cp in copies(t):
            cp.start()
        return c

    def wait(t, c):
        for cp in copies(t):
            cp.wait()
        return c

    lax.fori_loop(0, tm, start, 0, unroll=8)
    lax.fori_loop(0, tm, wait, 0, unroll=8)
    gate = gate_ref[...]
    ff = sum(buf[s] * gate[:, s:s + 1] for s in range(TOP_K))
    o_ref[...] = _ln(h_ref[...] * alpha + ff, g_ref[...], b_ref[...])


def _combine(pstart, idx, yb, route_f, h, ln_g, ln_b, alpha):
    T, D = h.shape
    tm = _row_tile(T)
    any_spec = pl.BlockSpec(memory_space=pl.ANY)
    row = lambda w: pl.BlockSpec((tm, w), lambda i, ps: (i, 0))
    vec = pl.BlockSpec((1, D), lambda i, ps: (0, 0))
    grid_spec = pltpu.PrefetchScalarGridSpec(
        num_scalar_prefetch=1,
        grid=(T // tm,),
        in_specs=[any_spec, any_spec, row(LANES), row(D), vec, vec],
        out_specs=row(D),
        scratch_shapes=[pltpu.SMEM((tm * ROUTE_COLS,), jnp.int32), pltpu.VMEM((TOP_K, tm, D), F32),
                        pltpu.SemaphoreType.DMA, pltpu.SemaphoreType.DMA],
    )
    return pl.pallas_call(
        functools.partial(_combine_kernel, tm=tm, alpha=alpha),
        out_shape=jax.ShapeDtypeStruct((T, D), F32),
        grid_spec=grid_spec,
        compiler_params=_cparams(("arbitrary",)),
        name="moe_combine",
    )(pstart, idx, yb, route_f, h, ln_g.reshape(1, D), ln_b.reshape(1, D))


def _hier_moe(h, route_f, route_i, counts, w1, w3, w2, ln_g, ln_b, alpha):
    T, D = h.shape
    counts = counts[0, :N_EXPERTS].astype(jnp.int32)
    padded = (counts + MOE_BLOCK - 1) // MOE_BLOCK * MOE_BLOCK
    pends = jnp.cumsum(padded)
    pstart = pends - padded
    n_rows = (T * TOP_K + N_EXPERTS * (MOE_BLOCK - 1) + MOE_BLOCK - 1) // MOE_BLOCK * MOE_BLOCK
    n_blocks = n_rows // MOE_BLOCK
    block_start = jnp.arange(n_blocks, dtype=jnp.int32) * MOE_BLOCK
    block_e = jnp.minimum(jnp.sum(pends[None, :] <= block_start[:, None], axis=1), N_EXPERTS - 1).astype(jnp.int32)
    n_used = (pends[-1:] // MOE_BLOCK).astype(jnp.int32)
    idx = route_i[:, :ROUTE_COLS].reshape(-1)
    xb = _dispatch(pstart, idx, h, n_rows)
    yb = _expert_ffn(xb, block_e, n_used, w1, w3, w2)
    return _combine(pstart, idx, yb, route_f, h, ln_g, ln_b, alpha)


def kernel(x, positions, meta_tokens, emb_ln_g, emb_ln_b, w_in, mla_q_norm, mla_kv_norm, mla_w_uq, mla_w_ukv, mla_out_norm, rwkv_mu_prev, rwkv_mu_next, rwkv_w0, rwkv_w2, rwkv_a0, rwkv_a2, rwkv_g2, rwkv_k_k, rwkv_k_a, rwkv_r_k, rwkv_gn_g, rwkv_gn_b, rwkv_v0, rwkv_v1, rwkv_v2, w_out, ln1_g, ln1_b, moe_w_group, moe_b_group, moe_w_expert, moe_b_expert, moe_w1, moe_w3, moe_w2, ln2_g, ln2_b):
    B, seq, D = x.shape
    depth = w_in.shape[0]
    alpha = (2 * depth) ** 0.25
    L = seq + N_META
    Lp = -(-L // LANES) * LANES
    T = B * Lp
    nb = 2 if B % 2 == 0 else 1
    tq = 384 if Lp % 384 == 0 else LANES

    meta = jnp.broadcast_to(meta_tokens.astype(x.dtype)[None], (B, N_META, D))
    h = jnp.concatenate([meta, x, jnp.zeros((B, Lp - L, D), x.dtype)], axis=1).reshape(T, D)
    h = _ln_residual(h, jnp.zeros_like(h), emb_ln_g, emb_ln_b, 1.0)

    key_bias = jnp.where(jnp.arange(Lp) < L, 0.0, NEG).astype(F32)[None, :]
    pos = jnp.concatenate([jnp.broadcast_to(jnp.arange(N_META, dtype=jnp.int32), (B, N_META)),
                           positions + N_META, jnp.zeros((B, Lp - L), jnp.int32)], axis=1)
    inv_freq = ROPE_THETA ** (-jnp.arange(0, QK_ROPE, 2, dtype=F32) / QK_ROPE)
    ang = pos.astype(F32)[..., None] * inv_freq
    cos, sin = jnp.cos(ang).reshape(T, -1), jnp.sin(ang).reshape(T, -1)
    zeros = jnp.zeros((T, LANES - QK_ROPE), F32)
    rope_cos = jnp.concatenate([cos, cos, zeros], axis=1)
    rope_sin = jnp.concatenate([-sin, sin, zeros], axis=1)
    qscale = (QK_NOPE + QK_ROPE) ** -0.5 * math.log2(math.e)

    v_first = None
    for li in range(depth):
        q, k, v, rw = _project(h, w_in[li], mla_q_norm[li], mla_kv_norm[li], mla_w_uq[li], mla_w_ukv[li],
                               rope_cos, rope_sin, qscale)
        y_att = _attention(q.reshape(B, Lp, -1), k.reshape(B, Lp, -1), v.reshape(B, Lp, -1),
                           key_bias, mla_out_norm[li], tq)
        vres = None if li == 0 else (rwkv_v0[li - 1], rwkv_v1[li - 1], rwkv_v2[li - 1])
        outs = _rwkv_prep(rw.reshape(B, Lp, -1), L, rwkv_mu_prev[li], rwkv_mu_next[li], rwkv_w0[li], rwkv_w2[li],
                          rwkv_a0[li], rwkv_a2[li], rwkv_g2[li], rwkv_k_k[li], rwkv_k_a[li], rwkv_r_k[li],
                          v_first, vres)
        r_, kk, vm, wf, kdf, bf, wb, kdb, bb_, g, bvg = outs[:11]
        if li == 0:
            v_first = outs[11]
        y_f, y_b = _wkv_bidir(r_, kk, vm, ((wf, kdf, bf), (wb, kdb, bb_)), nb)
        flat = lambda a: a.reshape(T, -1)
        h, route_f, route_i, counts = _post_mixer(
            flat(y_f), flat(y_b), flat(g), flat(bvg), flat(y_att), h, rwkv_gn_g[li], rwkv_gn_b[li], w_out[li],
            ln1_g[li], ln1_b[li], moe_w_group[li], moe_b_group[li], moe_w_expert[li], moe_b_expert[li], alpha)
        h = _hier_moe(h, route_f, route_i, counts,
                      moe_w1[li].astype(BF16), moe_w3[li].astype(BF16), moe_w2[li].astype(BF16),
                      ln2_g[li], ln2_b[li], alpha)
    return h.reshape(B, Lp, D)[:, N_META:L]
```

```python
import functools
import math

import numpy as np
import jax
import jax.numpy as jnp
from jax import lax
from jax.experimental import pallas as pl
from jax.experimental.pallas import tpu as pltpu

F32 = jnp.float32
BF16 = jnp.bfloat16

N_META = 16
H_A = 4
QK_NOPE = 128
QK_ROPE = 64
V_DIM = 128
Q_LORA = 256
KV_LORA = 128
ROPE_THETA = 10000.0
H_B = 8
N_B = 64
RW = H_B * N_B
DECAY_LORA = 64
AAA_LORA = 64
GATE_LORA = 128
VRES_LORA = 32
RWKV_GN_EPS = 64e-5
MLA_COLS = Q_LORA + KV_LORA + QK_ROPE
N_GROUPS = 4
EXPERTS_PER_GROUP = 8
N_EXPERTS = N_GROUPS * EXPERTS_PER_GROUP
TOP_K = 2
E_HID = 256
MOE_BLOCK = 128
LN_EPS = 1e-5
RMS_EPS = 1e-6

LANES = 128
SUBLANES = 8
TIME_BLOCK = 128
ROW_TILE = 512
VMEM_LIMIT = 56 * 1024 * 1024
HEAD_W = 2 * LANES
NEG = -1e30


def _cparams(sem):
    return pltpu.CompilerParams(dimension_semantics=sem, vmem_limit_bytes=VMEM_LIMIT)


def _row_tile(m):
    return next(t for t in (ROW_TILE, 384, 256, LANES) if m % t == 0)


def _split_bf16(x):
    hi = x.astype(BF16)
    lo = (x - hi.astype(F32)).astype(BF16)
    return hi, lo


def _seg_sum(x, ones):
    hi, lo = _split_bf16(x)
    return jnp.dot(jnp.concatenate([hi, lo], axis=-1), ones, preferred_element_type=F32)


def _seg_ones(width):
    m = np.arange(2 * width)[:, None] % width
    n = np.arange(width)[None, :]
    return jnp.asarray(m // N_B == n // N_B, BF16)


def _ln(x, g, b):
    mu = jnp.mean(x, -1, keepdims=True)
    xc = x - mu
    var = jnp.mean(xc * xc, -1, keepdims=True)
    return xc * lax.rsqrt(var + LN_EPS) * g + b


def _ln_kernel(x_ref, r_ref, g_ref, b_ref, o_ref, *, alpha):
    o_ref[...] = _ln(x_ref[...] * alpha + r_ref[...], g_ref[...], b_ref[...])


def _ln_residual(x, r, g, b, alpha):
    M, D = x.shape
    tm = _row_tile(M)
    row = pl.BlockSpec((tm, D), lambda i: (i, 0))
    vec = pl.BlockSpec((1, D), lambda i: (0, 0))
    return pl.pallas_call(
        functools.partial(_ln_kernel, alpha=alpha),
        out_shape=jax.ShapeDtypeStruct((M, D), F32),
        grid=(M // tm,),
        in_specs=[row, row, vec, vec],
        out_specs=row,
        compiler_params=_cparams(("parallel",)),
        name="layer_norm",
    )(x, r, g.reshape(1, D), b.reshape(1, D))


def _rms(x, g):
    return x * lax.rsqrt(jnp.mean(x * x, -1, keepdims=True) + RMS_EPS) * g


def _proj_kernel(h_ref, win_ref, qn_ref, kvn_ref, wuq_ref, wukv_ref, c_ref, s_ref,
                 q_ref, k_ref, v_ref, rw_ref, *, qscale):
    proj = jnp.dot(h_ref[...].astype(BF16), win_ref[...], preferred_element_type=F32)
    rw_ref[...] = proj[:, MLA_COLS + N_B:]
    q = jnp.dot(_rms(proj[:, :Q_LORA], qn_ref[...]).astype(BF16), wuq_ref[...], preferred_element_type=F32)
    kv = jnp.dot(_rms(proj[:, Q_LORA:Q_LORA + KV_LORA], kvn_ref[...]).astype(BF16), wukv_ref[...],
                 preferred_element_type=F32)
    cos, sin = c_ref[...], s_ref[...]
    lane = lax.broadcasted_iota(jnp.int32, cos.shape, 1)
    half = QK_ROPE // 2

    def rope(x):
        partner = jnp.where(lane < half, pltpu.roll(x, LANES - half, 1), pltpu.roll(x, half, 1))
        return x * cos + partner * sin

    k_pe = rope(proj[:, Q_LORA + KV_LORA:Q_LORA + KV_LORA + LANES]).astype(BF16)
    for h in range(H_A):
        o = h * HEAD_W
        q_ref[:, o:o + LANES] = (q[:, o:o + LANES] * qscale).astype(BF16)
        q_ref[:, o + LANES:o + HEAD_W] = (rope(q[:, o + LANES:o + HEAD_W]) * qscale).astype(BF16)
        k_ref[:, o:o + LANES] = kv[:, o:o + LANES].astype(BF16)
        k_ref[:, o + LANES:o + HEAD_W] = k_pe
        v_ref[:, h * V_DIM:(h + 1) * V_DIM] = kv[:, o + LANES:o + HEAD_W].astype(BF16)


def _project(h, w_in, q_norm, kv_norm, w_uq, w_ukv, rope_cos, rope_sin, qscale):
    T, D = h.shape
    tm = _row_tile(T)
    n_rw = w_in.shape[1] - MLA_COLS
    pad = jnp.zeros((D, N_B), F32)
    win = jnp.concatenate([w_in[:, :MLA_COLS], pad, w_in[:, MLA_COLS:]], axis=1).astype(BF16)
    wq = w_uq.reshape(Q_LORA, H_A, QK_NOPE + QK_ROPE)
    wq = jnp.concatenate([wq, jnp.zeros((Q_LORA, H_A, HEAD_W - QK_NOPE - QK_ROPE), F32)], axis=-1)
    wq = wq.reshape(Q_LORA, H_A * HEAD_W).astype(BF16)
    row = lambda w: pl.BlockSpec((tm, w), lambda i: (i, 0))
    full = lambda a: pl.BlockSpec(a.shape, lambda i: (0, 0))
    args = (h, win, q_norm.reshape(1, -1), kv_norm.reshape(1, -1), wq, w_ukv.astype(BF16), rope_cos, rope_sin)
    return pl.pallas_call(
        functools.partial(_proj_kernel, qscale=qscale),
        out_shape=(jax.ShapeDtypeStruct((T, H_A * HEAD_W), BF16),
                   jax.ShapeDtypeStruct((T, H_A * HEAD_W), BF16),
                   jax.ShapeDtypeStruct((T, H_A * V_DIM), BF16),
                   jax.ShapeDtypeStruct((T, n_rw), F32)),
        grid=(T // tm,),
        in_specs=[row(D)] + [full(a) for a in args[1:6]] + [row(LANES), row(LANES)],
        out_specs=(row(H_A * HEAD_W), row(H_A * HEAD_W), row(H_A * V_DIM), row(n_rw)),
        compiler_params=_cparams(("parallel",)),
        name="in_proj",
    )(*args)


def _attn_kernel(q_ref, k_ref, v_ref, bias_ref, g_ref, o_ref, m_ref, acc_ref, s_ref, p_ref, a_ref, *, tk):
    q = q_ref[0]
    n = k_ref.shape[1] // tk
    one_col = (lax.broadcasted_iota(jnp.int32, (tk, V_DIM), 1) == 0).astype(BF16)

    def scores(c):
        off = pl.multiple_of(c * tk, tk)
        s = lax.dot_general(q, k_ref[0, pl.ds(off, tk), :], (((1,), (1,)), ((), ())), preferred_element_type=F32)
        return s + bias_ref[:, pl.ds(off, tk)]

    def weighted_values(c):
        off = pl.multiple_of(c * tk, tk)
        v1 = jnp.concatenate([v_ref[0, pl.ds(off, tk), :], one_col], axis=1)
        acc_ref[...] = a_ref[...] * acc_ref[...] + jnp.dot(p_ref[...], v1, preferred_element_type=F32)

    m_ref[...] = jnp.full_like(m_ref, NEG)
    acc_ref[...] = jnp.zeros_like(acc_ref)
    p_ref[...] = jnp.zeros_like(p_ref)
    a_ref[...] = jnp.ones_like(a_ref)
    s_ref[...] = scores(0)

    def body(c, carry):
        weighted_values(jnp.maximum(c - 1, 0))
        s = s_ref[...]
        s_ref[...] = scores(jnp.minimum(c + 1, n - 1))
        m_prev = m_ref[...]
        m_new = jnp.maximum(m_prev, jnp.max(s, -1, keepdims=True))
        p_ref[...] = jnp.exp2(s - m_new).astype(BF16)
        a_ref[...] = jnp.exp2(m_prev - m_new)
        m_ref[...] = m_new
        return carry

    lax.fori_loop(0, n, body, 0)
    weighted_values(n - 1)
    acc = acc_ref[...]
    o = acc[:, :V_DIM] / acc[:, V_DIM:V_DIM + 1]
    o = o * lax.rsqrt(jnp.mean(o * o, -1, keepdims=True) + RMS_EPS) * g_ref[...]
    o_ref[0] = o.astype(BF16)


def _attention(q, k, v, bias, out_gain, tq, tk):
    B, Lp, _ = q.shape
    return pl.pallas_call(
        functools.partial(_attn_kernel, tk=tk),
        out_shape=jax.ShapeDtypeStruct((B, Lp, H_A * V_DIM), BF16),
        grid=(B, H_A, Lp // tq),
        in_specs=[pl.BlockSpec((1, tq, HEAD_W), lambda b, h, i: (b, i, h)),
                  pl.BlockSpec((1, Lp, HEAD_W), lambda b, h, i: (b, 0, h)),
                  pl.BlockSpec((1, Lp, V_DIM), lambda b, h, i: (b, 0, h)),
                  pl.BlockSpec((1, Lp), lambda b, h, i: (0, 0)),
                  pl.BlockSpec((1, V_DIM), lambda b, h, i: (0, h))],
        out_specs=pl.BlockSpec((1, tq, V_DIM), lambda b, h, i: (b, i, h)),
        scratch_shapes=[pltpu.VMEM((tq, 1), F32), pltpu.VMEM((tq, 2 * V_DIM), F32),
                        pltpu.VMEM((tq, tk), F32), pltpu.VMEM((tq, tk), BF16), pltpu.VMEM((tq, 1), F32)],
        compiler_params=_cparams(("parallel", "parallel", "parallel")),
        name="mla_attention",
    )(q, k, v, bias, out_gain.reshape(1, H_A * V_DIM))


def _prep_kernel(*refs, first, seq_len, tm):
    if first:
        (rw_ref, pv_ref, nx_ref, mup_ref, mun_ref, w2_ref, w0_ref, a2_ref, a0_ref, g2_ref, kk_ref, ka_ref, rk_ref,
         ones_ref, r_o, kk_o, v_o, wf_o, kdf_o, bf_o, wb_o, kdb_o, bb_o, g_o, bvg_o, vfirst_o) = refs
    else:
        (rw_ref, pv_ref, nx_ref, mup_ref, mun_ref, w2_ref, w0_ref, a2_ref, a0_ref, g2_ref, kk_ref, ka_ref, rk_ref,
         ones_ref, vf_ref, v0_ref, v1_ref, v2_ref,
         r_o, kk_o, v_o, wf_o, kdf_o, bf_o, wb_o, kdb_o, bb_o, g_o, bvg_o) = refs
    i = pl.program_id(1)
    n_t = pl.num_programs(1)
    row = lax.broadcasted_iota(jnp.int32, (tm, 1), 0)
    t = i * tm + row
    valid = t < seq_len
    rw = jnp.where(valid, rw_ref[0], 0.0)
    prev_row = jnp.where((i > 0) & (i * tm - 1 < seq_len), pv_ref[0, SUBLANES - 1:SUBLANES, :], 0.0)
    next_row = jnp.where((i < n_t - 1) & ((i + 1) * tm < seq_len), nx_ref[0, 0:1, :], 0.0)
    prev = jnp.where(row == 0, prev_row, pltpu.roll(rw, 1, 0))
    nxt = jnp.where(row == tm - 1, next_row, pltpu.roll(rw, tm - 1, 0))
    u = rw + mup_ref[...] * (prev - rw) + mun_ref[...] * (nxt - rw)
    r, k, v = u[:, :RW], u[:, RW:2 * RW], u[:, 2 * RW:3 * RW]
    wd = u[:, 3 * RW:3 * RW + LANES]
    ad = u[:, 3 * RW + LANES:3 * RW + 2 * LANES]
    gd = u[:, 3 * RW + 2 * LANES:]
    ones = ones_ref[...]
    if first:
        vfirst_o[0] = v
    else:
        low = jnp.dot(v.astype(BF16), v1_ref[...], preferred_element_type=F32)
        mix = jax.nn.sigmoid(v0_ref[...] + jnp.dot(low.astype(BF16), v2_ref[...], preferred_element_type=F32))
        v = v + (vf_ref[0] - v) * mix
    g = jnp.dot(jax.nn.sigmoid(gd).astype(BF16), g2_ref[...], preferred_element_type=F32)
    kk = k * kk_ref[...]
    kk = jnp.where(valid, kk * lax.rsqrt(_seg_sum(kk * kk, ones) + 1e-12), 0.0)
    wl = w0_ref[...] + jnp.dot(jnp.tanh(wd).astype(BF16), w2_ref[...], preferred_element_type=F32)
    decay = jnp.exp(-math.exp(-0.5) * jax.nn.sigmoid(wl))
    a = jax.nn.sigmoid(a0_ref[...] + jnp.dot(ad.astype(BF16), a2_ref[...], preferred_element_type=F32))
    ka = ka_ref[...]
    kd_f = jnp.where(valid, k * (1.0 + (a[:, :RW] - 1.0) * ka), 0.0)
    kd_b = jnp.where(valid, k * (1.0 + (a[:, RW:] - 1.0) * ka), 0.0)
    bonus = _seg_sum(r * (kd_f + kd_b) * rk_ref[...], ones)
    r_o[0] = r
    kk_o[0] = kk
    v_o[0] = jnp.where(valid, v, 0.0)
    wf_o[0] = decay[:, :RW]
    wb_o[0] = decay[:, RW:]
    kdf_o[0] = kd_f
    kdb_o[0] = kd_b
    bf_o[0] = kk * a[:, :RW]
    bb_o[0] = kk * a[:, RW:]
    g_o[0] = g
    bvg_o[0] = bonus * v * g


def _block_diag2(a, b):
    z = jnp.zeros_like(a)
    return jnp.concatenate([jnp.concatenate([a, z], 1), jnp.concatenate([z, b], 1)], 0)


def _rwkv_prep(rw, seq_len, mu_prev, mu_next, w0, w2, a0, a2, g2, k_k, k_a, r_k, v_first, vres):
    B, Lp, n_rw = rw.shape
    tm = _row_tile(Lp)
    tpb = tm // SUBLANES
    first = vres is None
    vec = lambda a: a.reshape(1, -1)
    consts = [vec(mu_prev), vec(mu_next), _block_diag2(w2[0], w2[1]).astype(BF16), vec(w0),
              _block_diag2(a2[0], a2[1]).astype(BF16), vec(a0), g2.astype(BF16), vec(k_k), vec(k_a), vec(r_k),
              _seg_ones(RW)]
    tile = lambda w: pl.BlockSpec((1, tm, w), lambda b, i: (b, i, 0))
    full = lambda a: pl.BlockSpec(a.shape, lambda b, i: (0, 0))
    in_specs = [tile(n_rw),
                pl.BlockSpec((1, SUBLANES, n_rw), lambda b, i: (b, jnp.maximum(i * tpb - 1, 0), 0)),
                pl.BlockSpec((1, SUBLANES, n_rw), lambda b, i: (b, jnp.minimum((i + 1) * tpb, Lp // SUBLANES - 1), 0))]
    in_specs += [full(c) for c in consts]
    args = [rw, rw, rw] + consts
    n_out = 11
    if first:
        n_out += 1
    else:
        v0, v1, v2 = vres
        v1p = jnp.concatenate([v1, jnp.zeros((RW, LANES - VRES_LORA), F32)], 1).astype(BF16)
        v2p = jnp.concatenate([v2, jnp.zeros((LANES - VRES_LORA, RW), F32)], 0).astype(BF16)
        extra = [vec(v0), v1p, v2p]
        in_specs += [tile(RW)] + [full(c) for c in extra]
        args += [v_first] + extra
    out = pl.pallas_call(
        functools.partial(_prep_kernel, first=first, seq_len=seq_len, tm=tm),
        out_shape=tuple(jax.ShapeDtypeStruct((B, Lp, RW), F32) for _ in range(n_out)),
        grid=(B, Lp // tm),
        in_specs=in_specs,
        out_specs=tuple(tile(RW) for _ in range(n_out)),
        compiler_params=_cparams(("parallel", "parallel")),
        name="rwkv_prep",
    )(*args)
    return out


def _scan_kernel(rf, kkf, wf, kdf, bf, vf, rb, kkb, wb, kdb, bb_, vb_, ones_ref,
                 yf_ref, yb_ref, s_ref, vt_ref, acc_ref, *, nb):
    j = pl.program_id(1)
    C = nb * 4
    half_t = TIME_BLOCK // 2

    @pl.when(j == 0)
    def _():
        s_ref[...] = jnp.zeros_like(s_ref)

    lane1 = lax.broadcasted_iota(jnp.int32, (N_B, LANES), 1)
    lo_half = lane1 < N_B

    for d, vref in enumerate((vf, vb_)):
        for n in range(nb):
            for hp in range(4):
                c = n * 4 + hp
                xt = vref[n, :, hp * LANES:(hp + 1) * LANES].T
                top, bot = xt[:N_B], xt[N_B:]
                vt_ref[d, 0, c * N_B:(c + 1) * N_B, :] = jnp.where(lo_half, top, pltpu.roll(bot, N_B, 1))
                vt_ref[d, 1, c * N_B:(c + 1) * N_B, :] = jnp.where(lo_half, pltpu.roll(top, N_B, 1), bot)
    acc_ref[...] = jnp.zeros_like(acc_ref)

    ones = ones_ref[...]
    lane_id = lax.broadcasted_iota(jnp.int32, (C * N_B, LANES), 1)
    lane = lane_id % N_B
    head_base = (lane_id // N_B) * N_B
    row_refs = ((rf, kkf, wf, kdf, bf), (rb, kkb, wb, kdb, bb_))

    def step(d, tiles, row, oh, tt):
        def rows(tile):
            return jnp.concatenate(
                [jnp.broadcast_to(tile[n, row:row + 1, hp * LANES:(hp + 1) * LANES], (N_B, LANES))
                 for n in range(nb) for hp in range(4)], axis=0)

        r, kk, w, kd, b = [rows(x) for x in tiles]
        S = s_ref[d]
        sa = _seg_sum(S * kk, ones)
        vb = jnp.take_along_axis(vt_ref[d, oh], head_base + tt, axis=1, mode="promise_in_bounds")
        S2 = S * w - sa * b + vb * kd
        s_ref[d] = S2
        y = _seg_sum(S2 * r, ones)
        acc_ref[d, oh] = jnp.where(lane == tt, y, acc_ref[d, oh])

    def make_body(half):
        def body(g, carry):
            t0f = pl.multiple_of(half * half_t + g * SUBLANES, SUBLANES)
            t0b = pl.multiple_of(TIME_BLOCK - SUBLANES - (half * half_t + g * SUBLANES), SUBLANES)
            tiles_f = [x[:, pl.ds(t0f, SUBLANES), :] for x in row_refs[0]]
            tiles_b = [x[:, pl.ds(t0b, SUBLANES), :] for x in row_refs[1]]
            for i in range(SUBLANES):
                step(0, tiles_f, i, half, g * SUBLANES + i)
                step(1, tiles_b, SUBLANES - 1 - i, 1 - half, half_t - 1 - (g * SUBLANES + i))
            return carry
        return body

    lax.fori_loop(0, half_t // SUBLANES, make_body(0), 0)
    lax.fori_loop(0, half_t // SUBLANES, make_body(1), 0)

    for d, yref in enumerate((yf_ref, yb_ref)):
        for n in range(nb):
            for hp in range(4):
                c = n * 4 + hp
                a = jnp.concatenate([acc_ref[d, 0, c * N_B:(c + 1) * N_B, :],
                                     acc_ref[d, 1, c * N_B:(c + 1) * N_B, :]], axis=0).T
                top, bot = a[:N_B], a[N_B:]
                yref[n, :N_B, hp * LANES:(hp + 1) * LANES] = jnp.where(lo_half, top, pltpu.roll(bot, N_B, 1))
                yref[n, N_B:, hp * LANES:(hp + 1) * LANES] = jnp.where(lo_half, pltpu.roll(top, N_B, 1), bot)


def _wkv_bidir(r, kk, v, per_dir, nb):
    B, Lp, _ = r.shape
    nblk = Lp // TIME_BLOCK
    ones = _seg_ones(LANES)
    fwd = lambda bi, j: (bi, j, 0)
    bwd = lambda bi, j: (bi, nblk - 1 - j, 0)
    blk = (nb, TIME_BLOCK, RW)
    (wf, kdf, bf), (wb, kdb, bb_) = per_dir
    y_shape = jax.ShapeDtypeStruct((B, Lp, RW), F32)
    rows = nb * 4 * N_B
    return pl.pallas_call(
        functools.partial(_scan_kernel, nb=nb),
        out_shape=(y_shape, y_shape),
        grid=(B // nb, nblk),
        in_specs=[pl.BlockSpec(blk, fwd)] * 6 + [pl.BlockSpec(blk, bwd)] * 6
                 + [pl.BlockSpec(ones.shape, lambda bi, j: (0, 0))],
        out_specs=(pl.BlockSpec(blk, fwd), pl.BlockSpec(blk, bwd)),
        scratch_shapes=[pltpu.VMEM((2, rows, LANES), F32),
                        pltpu.VMEM((2, 2, rows, LANES), F32),
                        pltpu.VMEM((2, 2, rows, LANES), F32)],
        compiler_params=_cparams(("parallel", "arbitrary")),
        name="wkv_scan",
    )(r, kk, wf, kdf, bf, v, r, kk, wb, kdb, bb_, v, ones)


def _post_kernel(yf_ref, yb_ref, g_ref, bvg_ref, att_ref, h_ref, gng_ref, gnb_ref, wo_ref, l1g_ref, l1b_ref,
                 wrh_ref, wrl_ref, rb_ref, tri_ref, ones_ref,
                 h1_ref, rf_ref, ri_ref, cnt_ref, *, alpha, tm):
    ones = ones_ref[...]
    y = yf_ref[...] + yb_ref[...]
    mu = _seg_sum(y, ones) * (1.0 / N_B)
    yc = y - mu
    var = _seg_sum(yc * yc, ones) * (1.0 / N_B)
    yr = (yc * lax.rsqrt(var + RWKV_GN_EPS) * gng_ref[...] + gnb_ref[...]) * g_ref[...] + bvg_ref[...]
    half = H_A * V_DIM
    mixed = (jnp.dot(att_ref[...], wo_ref[:half, :], preferred_element_type=F32)
             + jnp.dot(yr.astype(BF16), wo_ref[half:, :], preferred_element_type=F32))
    h1 = _ln(h_ref[...] * alpha + mixed, l1g_ref[...], l1b_ref[...])
    h1_ref[...] = h1

    xh, xl = _split_bf16(h1)
    x = (jnp.dot(xh, wrh_ref[...], preferred_element_type=F32) + jnp.dot(xl, wrh_ref[...], preferred_element_type=F32)
         + jnp.dot(xh, wrl_ref[...], preferred_element_type=F32)) + rb_ref[...]
    lane = lax.broadcasted_iota(jnp.int32, x.shape, 1)
    lanef = lane.astype(F32)
    big = float(LANES)
    gmask = lane < N_GROUPS
    gx = jnp.where(gmask, x, NEG)
    gmax = jnp.max(gx, -1, keepdims=True)
    gidx = jnp.min(jnp.where(gx == gmax, lanef, big), -1, keepdims=True)
    gsum = jnp.sum(jnp.where(gmask, jnp.exp(gx - gmax), 0.0), -1, keepdims=True)
    lo = N_GROUPS + EXPERTS_PER_GROUP * gidx
    emask = (lanef >= lo) & (lanef < lo + EXPERTS_PER_GROUP)
    ex = jnp.where(emask, x, NEG)
    m1 = jnp.max(ex, -1, keepdims=True)
    i1 = jnp.min(jnp.where(emask & (ex == m1), lanef, big), -1, keepdims=True)
    ex2 = jnp.where(lanef == i1, NEG, ex)
    m2 = jnp.max(ex2, -1, keepdims=True)
    i2 = jnp.min(jnp.where(emask & (lanef != i1) & (ex2 == m2), lanef, big), -1, keepdims=True)
    esum = jnp.sum(jnp.where(emask, jnp.exp(ex - m1), 0.0), -1, keepdims=True)
    gp = 1.0 / gsum
    gate0 = gp * (1.0 / esum)
    gate1 = gp * (jnp.exp(m2 - m1) / esum)
    e0 = i1 - N_GROUPS
    e1 = i2 - N_GROUPS

    @pl.when(pl.program_id(0) == 0)
    def _():
        cnt_ref[...] = jnp.zeros_like(cnt_ref)

    onehot = jnp.where((lanef == e0) | (lanef == e1), 1.0, 0.0)
    prefix = jnp.dot(tri_ref[...], onehot.astype(BF16), preferred_element_type=F32) + cnt_ref[...]
    r0 = jnp.sum(jnp.where(lanef == e0, prefix, 0.0), -1, keepdims=True)
    r1 = jnp.sum(jnp.where(lanef == e1, prefix, 0.0), -1, keepdims=True)
    cnt_ref[...] += jnp.sum(onehot, 0, keepdims=True)
    rf_ref[...] = jnp.where(lane == 0, gate0, jnp.where(lane == 1, gate1, 0.0))
    ri = jnp.where(lane == 0, e0, jnp.where(lane == 1, e1, jnp.where(lane == 2, r0, jnp.where(lane == 3, r1, 0.0))))
    ri_ref[...] = ri.astype(jnp.int32)


def _post_mixer(yf, yb, g, bvg, att, h, gn_g, gn_b, w_out, ln_g, ln_b, w_group, b_group, w_expert, b_expert, alpha):
    T, D = h.shape
    tm = _row_tile(T)
    vec = lambda a: a.reshape(1, -1)
    zpad = LANES - N_GROUPS - N_EXPERTS
    wr = jnp.concatenate([w_group, w_expert, jnp.zeros((D, zpad), F32)], axis=1)
    wrh = wr.astype(BF16)
    wrl = (wr - wrh.astype(F32)).astype(BF16)
    rb = jnp.concatenate([b_group, b_expert, jnp.zeros((zpad,), F32)]).reshape(1, LANES)
    tri = jnp.asarray(np.arange(tm)[:, None] > np.arange(tm)[None, :], BF16)
    consts = [vec(gn_g), vec(gn_b), w_out.astype(BF16), vec(ln_g), vec(ln_b), wrh, wrl, rb, tri, _seg_ones(RW)]
    row = lambda w: pl.BlockSpec((tm, w), lambda i: (i, 0))
    full = lambda a: pl.BlockSpec(a.shape, lambda i: (0, 0))
    return pl.pallas_call(
        functools.partial(_post_kernel, alpha=alpha, tm=tm),
        out_shape=(jax.ShapeDtypeStruct((T, D), F32),
                   jax.ShapeDtypeStruct((T, LANES), F32),
                   jax.ShapeDtypeStruct((T, LANES), jnp.int32),
                   jax.ShapeDtypeStruct((1, LANES), F32)),
        grid=(T // tm,),
        in_specs=[row(RW)] * 4 + [row(H_A * V_DIM), row(D)] + [full(c) for c in consts],
        out_specs=(row(D), row(LANES), row(LANES), pl.BlockSpec((1, LANES), lambda i: (0, 0))),
        compiler_params=_cparams(("arbitrary",)),
        name="post_mixer",
    )(yf, yb, g, bvg, att, h, *consts)


ROUTE_COLS = 2 * TOP_K


def _row_copies(ps_ref, idx_smem, t, make):
    return [make(s, ps_ref[idx_smem[ROUTE_COLS * t + s]] + idx_smem[ROUTE_COLS * t + TOP_K + s]) for s in range(TOP_K)]


def _dispatch_kernel(ps_ref, idx_hbm, h_ref, xz_hbm, xb_hbm, idx_smem, isem, sem, *, tm):
    del xz_hbm
    i = pl.program_id(0)
    load = pltpu.make_async_copy(idx_hbm.at[pl.ds(i * tm * ROUTE_COLS, tm * ROUTE_COLS)], idx_smem, isem)
    load.start()
    load.wait()

    def copies(t):
        src = h_ref.at[pl.ds(t, 1)]
        return _row_copies(ps_ref, idx_smem, t, lambda s, dst: pltpu.make_async_copy(src, xb_hbm.at[pl.ds(dst, 1)], sem))

    def start(t, c):
        for cp in copies(t):
            cp.start()
        return c

    def wait(t, c):
        for cp in copies(t):
            cp.wait()
        return c

    lax.fori_loop(0, tm, start, 0, unroll=8)
    lax.fori_loop(0, tm, wait, 0, unroll=8)


def _dispatch(pstart, idx, h, n_rows):
    T, D = h.shape
    tm = _row_tile(T)
    any_spec = pl.BlockSpec(memory_space=pl.ANY)
    grid_spec = pltpu.PrefetchScalarGridSpec(
        num_scalar_prefetch=1,
        grid=(T // tm,),
        in_specs=[any_spec, pl.BlockSpec((tm, D), lambda i, ps: (i, 0)), any_spec],
        out_specs=any_spec,
        scratch_shapes=[pltpu.SMEM((tm * ROUTE_COLS,), jnp.int32), pltpu.SemaphoreType.DMA, pltpu.SemaphoreType.DMA],
    )
    return pl.pallas_call(
        functools.partial(_dispatch_kernel, tm=tm),
        out_shape=jax.ShapeDtypeStruct((n_rows, D), F32),
        grid_spec=grid_spec,
        input_output_aliases={3: 0},
        compiler_params=_cparams(("arbitrary",)),
        name="moe_dispatch",
    )(pstart, idx, h, jnp.zeros((n_rows, D), F32))


def _expert_kernel(be_ref, nu_ref, x_ref, w1_ref, w3_ref, w2_ref, o_ref):
    @pl.when(pl.program_id(0) < nu_ref[0])
    def _():
        x = x_ref[...].astype(BF16)
        a = jnp.dot(x, w1_ref[0], preferred_element_type=F32)
        b = jnp.dot(x, w3_ref[0], preferred_element_type=F32)
        hid = (a * jax.nn.sigmoid(a)) * b
        o_ref[...] = jnp.dot(hid.astype(BF16), w2_ref[0], preferred_element_type=F32)


def _expert_ffn(xb, block_e, n_used, w1, w3, w2):
    n_rows, D = xb.shape
    n_blocks = n_rows // MOE_BLOCK
    blk = lambda i, be, nu: jnp.minimum(i, nu[0] - 1)
    wmap = lambda i, be, nu: (be[blk(i, be, nu)], 0, 0)
    grid_spec = pltpu.PrefetchScalarGridSpec(
        num_scalar_prefetch=2,
        grid=(n_blocks,),
        in_specs=[pl.BlockSpec((MOE_BLOCK, D), lambda i, be, nu: (blk(i, be, nu), 0)),
                  pl.BlockSpec((1, D, E_HID), wmap),
                  pl.BlockSpec((1, D, E_HID), wmap),
                  pl.BlockSpec((1, E_HID, D), wmap)],
        out_specs=pl.BlockSpec((MOE_BLOCK, D), lambda i, be, nu: (blk(i, be, nu), 0)),
    )
    return pl.pallas_call(
        _expert_kernel,
        out_shape=jax.ShapeDtypeStruct((n_rows, D), F32),
        grid_spec=grid_spec,
        compiler_params=_cparams(("arbitrary",)),
        name="moe_experts",
    )(block_e, n_used, xb, w1, w3, w2)


def _combine_kernel(ps_ref, idx_hbm, yb_hbm, gate_ref, h_ref, g_ref, b_ref, o_ref, idx_smem, buf, isem, sem,
                    *, tm, alpha):
    i = pl.program_id(0)
    load = pltpu.make_async_copy(idx_hbm.at[pl.ds(i * tm * ROUTE_COLS, tm * ROUTE_COLS)], idx_smem, isem)
    load.start()
    load.wait()

    def copies(t):
        return _row_copies(ps_ref, idx_smem, t, lambda s, src: pltpu.make_async_copy(
            yb_hbm.at[pl.ds(src, 1)], buf.at[s, pl.ds(t, 1)], sem))

    def start(t, c):
        for cp in copies(t):
            cp.start()
        return c

    def wait(t, c):
        for cp in copies(t):
            cp.wait()
        return c

    lax.fori_loop(0, tm, start, 0, unroll=8)
    lax.fori_loop(0, tm, wait, 0, unroll=8)
    gate = gate_ref[...]
    ff = sum(buf[s] * gate[:, s:s + 1] for s in range(TOP_K))
    o_ref[...] = _ln(h_ref[...] * alpha + ff, g_ref[...], b_ref[...])


def _combine(pstart, idx, yb, route_f, h, ln_g, ln_b, alpha):
    T, D = h.shape
    tm = _row_tile(T)
    any_spec = pl.BlockSpec(memory_space=pl.ANY)
    row = lambda w: pl.BlockSpec((tm, w), lambda i, ps: (i, 0))
    vec = pl.BlockSpec((1, D), lambda i, ps: (0, 0))
    grid_spec = pltpu.PrefetchScalarGridSpec(
        num_scalar_prefetch=1,
        grid=(T // tm,),
        in_specs=[any_spec, any_spec, row(LANES), row(D), vec, vec],
        out_specs=row(D),
        scratch_shapes=[pltpu.SMEM((tm * ROUTE_COLS,), jnp.int32), pltpu.VMEM((TOP_K, tm, D), F32),
                        pltpu.SemaphoreType.DMA, pltpu.SemaphoreType.DMA],
    )
    return pl.pallas_call(
        functools.partial(_combine_kernel, tm=tm, alpha=alpha),
        out_shape=jax.ShapeDtypeStruct((T, D), F32),
        grid_spec=grid_spec,
        compiler_params=_cparams(("arbitrary",)),
        name="moe_combine",
    )(pstart, idx, yb, route_f, h, ln_g.reshape(1, D), ln_b.reshape(1, D))


def _hier_moe(h, route_f, route_i, counts, w1, w3, w2, ln_g, ln_b, alpha):
    T, D = h.shape
    counts = counts[0, :N_EXPERTS].astype(jnp.int32)
    padded = (counts + MOE_BLOCK - 1) // MOE_BLOCK * MOE_BLOCK
    pends = jnp.cumsum(padded)
    pstart = pends - padded
    n_rows = (T * TOP_K + N_EXPERTS * (MOE_BLOCK - 1) + MOE_BLOCK - 1) // MOE_BLOCK * MOE_BLOCK
    n_blocks = n_rows // MOE_BLOCK
    block_start = jnp.arange(n_blocks, dtype=jnp.int32) * MOE_BLOCK
    block_e = jnp.minimum(jnp.sum(pends[None, :] <= block_start[:, None], axis=1), N_EXPERTS - 1).astype(jnp.int32)
    n_used = (pends[-1:] // MOE_BLOCK).astype(jnp.int32)
    idx = route_i[:, :ROUTE_COLS].reshape(-1)
    xb = _dispatch(pstart, idx, h, n_rows)
    yb = _expert_ffn(xb, block_e, n_used, w1, w3, w2)
    return _combine(pstart, idx, yb, route_f, h, ln_g, ln_b, alpha)


def kernel(x, positions, meta_tokens, emb_ln_g, emb_ln_b, w_in, mla_q_norm, mla_kv_norm, mla_w_uq, mla_w_ukv, mla_out_norm, rwkv_mu_prev, rwkv_mu_next, rwkv_w0, rwkv_w2, rwkv_a0, rwkv_a2, rwkv_g2, rwkv_k_k, rwkv_k_a, rwkv_r_k, rwkv_gn_g, rwkv_gn_b, rwkv_v0, rwkv_v1, rwkv_v2, w_out, ln1_g, ln1_b, moe_w_group, moe_b_group, moe_w_expert, moe_b_expert, moe_w1, moe_w3, moe_w2, ln2_g, ln2_b):
    B, seq, D = x.shape
    depth = w_in.shape[0]
    alpha = (2 * depth) ** 0.25
    L = seq + N_META
    Lp = -(-L // LANES) * LANES
    T = B * Lp
    nb = 2 if B % 2 == 0 else 1
    tk = 384 if Lp % 384 == 0 else LANES
    tq = next(t for t in (1408, 1152, 768, 384, LANES) if Lp % t == 0)

    meta = jnp.broadcast_to(meta_tokens.astype(x.dtype)[None], (B, N_META, D))
    h = jnp.concatenate([meta, x, jnp.zeros((B, Lp - L, D), x.dtype)], axis=1).reshape(T, D)
    h = _ln_residual(h, jnp.zeros_like(h), emb_ln_g, emb_ln_b, 1.0)

    key_bias = jnp.where(jnp.arange(Lp) < L, 0.0, NEG).astype(F32)[None, :]
    pos = jnp.concatenate([jnp.broadcast_to(jnp.arange(N_META, dtype=jnp.int32), (B, N_META)),
                           positions + N_META, jnp.zeros((B, Lp - L), jnp.int32)], axis=1)
    inv_freq = ROPE_THETA ** (-jnp.arange(0, QK_ROPE, 2, dtype=F32) / QK_ROPE)
    ang = pos.astype(F32)[..., None] * inv_freq
    cos, sin = jnp.cos(ang).reshape(T, -1), jnp.sin(ang).reshape(T, -1)
    zeros = jnp.zeros((T, LANES - QK_ROPE), F32)
    rope_cos = jnp.concatenate([cos, cos, zeros], axis=1)
    rope_sin = jnp.concatenate([-sin, sin, zeros], axis=1)
    qscale = (QK_NOPE + QK_ROPE) ** -0.5 * math.log2(math.e)

    v_first = None
    for li in range(depth):
        q, k, v, rw = _project(h, w_in[li], mla_q_norm[li], mla_kv_norm[li], mla_w_uq[li], mla_w_ukv[li],
                               rope_cos, rope_sin, qscale)
        y_att = _attention(q.reshape(B, Lp, -1), k.reshape(B, Lp, -1), v.reshape(B, Lp, -1),
                           key_bias, mla_out_norm[li], tq, tk)
        vres = None if li == 0 else (rwkv_v0[li - 1], rwkv_v1[li - 1], rwkv_v2[li - 1])
        outs = _rwkv_prep(rw.reshape(B, Lp, -1), L, rwkv_mu_prev[li], rwkv_mu_next[li], rwkv_w0[li], rwkv_w2[li],
                          rwkv_a0[li], rwkv_a2[li], rwkv_g2[li], rwkv_k_k[li], rwkv_k_a[li], rwkv_r_k[li],
                          v_first, vres)
        r_, kk, vm, wf, kdf, bf, wb, kdb, bb_, g, bvg = outs[:11]
        if li == 0:
            v_first = outs[11]
        y_f, y_b = _wkv_bidir(r_, kk, vm, ((wf, kdf, bf), (wb, kdb, bb_)), nb)
        flat = lambda a: a.reshape(T, -1)
        h, route_f, route_i, counts = _post_mixer(
            flat(y_f), flat(y_b), flat(g), flat(bvg), flat(y_att), h, rwkv_gn_g[li], rwkv_gn_b[li], w_out[li],
            ln1_g[li], ln1_b[li], moe_w_group[li], moe_b_group[li], moe_w_expert[li], moe_b_expert[li], alpha)
        h = _hier_moe(h, route_f, route_i, counts,
                      moe_w1[li].astype(BF16), moe_w3[li].astype(BF16), moe_w2[li].astype(BF16),
                      ln2_g[li], ln2_b[li], alpha)
    return h.reshape(B, Lp, D)[:, N_META:L]
```

```python
import functools
import math

import numpy as np
import jax
import jax.numpy as jnp
from jax import lax
from jax.experimental import pallas as pl
from jax.experimental.pallas import tpu as pltpu

F32 = jnp.float32
BF16 = jnp.bfloat16

N_META = 16
H_A = 4
QK_NOPE = 128
QK_ROPE = 64
V_DIM = 128
Q_LORA = 256
KV_LORA = 128
ROPE_THETA = 10000.0
H_B = 8
N_B = 64
RW = H_B * N_B
DECAY_LORA = 64
AAA_LORA = 64
GATE_LORA = 128
VRES_LORA = 32
RWKV_GN_EPS = 64e-5
MLA_COLS = Q_LORA + KV_LORA + QK_ROPE
N_GROUPS = 4
EXPERTS_PER_GROUP = 8
N_EXPERTS = N_GROUPS * EXPERTS_PER_GROUP
TOP_K = 2
E_HID = 256
MOE_BLOCK = 128
LN_EPS = 1e-5
RMS_EPS = 1e-6

LANES = 128
SUBLANES = 8
TIME_BLOCK = 128
SCAN_CHUNK = 64
ROW_TILE = 512
VMEM_LIMIT = 56 * 1024 * 1024
HEAD_W = 2 * LANES
NEG = -1e30


def _cparams(sem):
    return pltpu.CompilerParams(dimension_semantics=sem, vmem_limit_bytes=VMEM_LIMIT)


def _row_tile(m):
    return next(t for t in (ROW_TILE, 384, 256, LANES) if m % t == 0)


def _split_bf16(x):
    hi = x.astype(BF16)
    lo = (x - hi.astype(F32)).astype(BF16)
    return hi, lo


def _seg_sum(x, ones):
    hi, lo = _split_bf16(x)
    return jnp.dot(jnp.concatenate([hi, lo], axis=-1), ones, preferred_element_type=F32)


def _seg_ones(width):
    m = np.arange(2 * width)[:, None] % width
    n = np.arange(width)[None, :]
    return jnp.asarray(m // N_B == n // N_B, BF16)


def _ln(x, g, b):
    mu = jnp.mean(x, -1, keepdims=True)
    xc = x - mu
    var = jnp.mean(xc * xc, -1, keepdims=True)
    return xc * lax.rsqrt(var + LN_EPS) * g + b


def _ln_kernel(x_ref, r_ref, g_ref, b_ref, o_ref, *, alpha):
    o_ref[...] = _ln(x_ref[...] * alpha + r_ref[...], g_ref[...], b_ref[...])


def _ln_residual(x, r, g, b, alpha):
    M, D = x.shape
    tm = _row_tile(M)
    row = pl.BlockSpec((tm, D), lambda i: (i, 0))
    vec = pl.BlockSpec((1, D), lambda i: (0, 0))
    return pl.pallas_call(
        functools.partial(_ln_kernel, alpha=alpha),
        out_shape=jax.ShapeDtypeStruct((M, D), F32),
        grid=(M // tm,),
        in_specs=[row, row, vec, vec],
        out_specs=row,
        compiler_params=_cparams(("parallel",)),
        name="layer_norm",
    )(x, r, g.reshape(1, D), b.reshape(1, D))


def _rms(x, g):
    return x * lax.rsqrt(jnp.mean(x * x, -1, keepdims=True) + RMS_EPS) * g


def _proj_kernel(h_ref, win_ref, qn_ref, kvn_ref, wuq_ref, wukv_ref, c_ref, s_ref,
                 q_ref, k_ref, v_ref, rw_ref, *, qscale):
    proj = jnp.dot(h_ref[...].astype(BF16), win_ref[...], preferred_element_type=F32)
    rw_ref[...] = proj[:, MLA_COLS + N_B:]
    q = jnp.dot(_rms(proj[:, :Q_LORA], qn_ref[...]).astype(BF16), wuq_ref[...], preferred_element_type=F32)
    kv = jnp.dot(_rms(proj[:, Q_LORA:Q_LORA + KV_LORA], kvn_ref[...]).astype(BF16), wukv_ref[...],
                 preferred_element_type=F32)
    cos, sin = c_ref[...], s_ref[...]
    lane = lax.broadcasted_iota(jnp.int32, cos.shape, 1)
    half = QK_ROPE // 2

    def rope(x):
        partner = jnp.where(lane < half, pltpu.roll(x, LANES - half, 1), pltpu.roll(x, half, 1))
        return x * cos + partner * sin

    k_pe = rope(proj[:, Q_LORA + KV_LORA:Q_LORA + KV_LORA + LANES]).astype(BF16)
    for h in range(H_A):
        o = h * HEAD_W
        q_ref[:, o:o + LANES] = (q[:, o:o + LANES] * qscale).astype(BF16)
        q_ref[:, o + LANES:o + HEAD_W] = (rope(q[:, o + LANES:o + HEAD_W]) * qscale).astype(BF16)
        k_ref[:, o:o + LANES] = kv[:, o:o + LANES].astype(BF16)
        k_ref[:, o + LANES:o + HEAD_W] = k_pe
        v_ref[:, h * V_DIM:(h + 1) * V_DIM] = kv[:, o + LANES:o + HEAD_W].astype(BF16)


def _project(h, w_in, q_norm, kv_norm, w_uq, w_ukv, rope_cos, rope_sin, qscale):
    T, D = h.shape
    tm = _row_tile(T)
    n_rw = w_in.shape[1] - MLA_COLS
    pad = jnp.zeros((D, N_B), F32)
    win = jnp.concatenate([w_in[:, :MLA_COLS], pad, w_in[:, MLA_COLS:]], axis=1).astype(BF16)
    wq = w_uq.reshape(Q_LORA, H_A, QK_NOPE + QK_ROPE)
    wq = jnp.concatenate([wq, jnp.zeros((Q_LORA, H_A, HEAD_W - QK_NOPE - QK_ROPE), F32)], axis=-1)
    wq = wq.reshape(Q_LORA, H_A * HEAD_W).astype(BF16)
    row = lambda w: pl.BlockSpec((tm, w), lambda i: (i, 0))
    full = lambda a: pl.BlockSpec(a.shape, lambda i: (0, 0))
    args = (h, win, q_norm.reshape(1, -1), kv_norm.reshape(1, -1), wq, w_ukv.astype(BF16), rope_cos, rope_sin)
    return pl.pallas_call(
        functools.partial(_proj_kernel, qscale=qscale),
        out_shape=(jax.ShapeDtypeStruct((T, H_A * HEAD_W), BF16),
                   jax.ShapeDtypeStruct((T, H_A * HEAD_W), BF16),
                   jax.ShapeDtypeStruct((T, H_A * V_DIM), BF16),
                   jax.ShapeDtypeStruct((T, n_rw), F32)),
        grid=(T // tm,),
        in_specs=[row(D)] + [full(a) for a in args[1:6]] + [row(LANES), row(LANES)],
        out_specs=(row(H_A * HEAD_W), row(H_A * HEAD_W), row(H_A * V_DIM), row(n_rw)),
        compiler_params=_cparams(("parallel",)),
        name="in_proj",
    )(*args)


def _attn_kernel(q_ref, k_ref, v_ref, bias_ref, g_ref, o_ref, m_ref, acc_ref, s_ref, p_ref, a_ref, *, tk):
    q = q_ref[0]
    n = k_ref.shape[1] // tk
    one_col = (lax.broadcasted_iota(jnp.int32, (tk, V_DIM), 1) == 0).astype(BF16)

    def scores(c):
        off = pl.multiple_of(c * tk, tk)
        s = lax.dot_general(q, k_ref[0, pl.ds(off, tk), :], (((1,), (1,)), ((), ())), preferred_element_type=F32)
        return s + bias_ref[:, pl.ds(off, tk)]

    def weighted_values(c):
        off = pl.multiple_of(c * tk, tk)
        v1 = jnp.concatenate([v_ref[0, pl.ds(off, tk), :], one_col], axis=1)
        acc_ref[...] = a_ref[...] * acc_ref[...] + jnp.dot(p_ref[...], v1, preferred_element_type=F32)

    m_ref[...] = jnp.full_like(m_ref, NEG)
    acc_ref[...] = jnp.zeros_like(acc_ref)
    p_ref[...] = jnp.zeros_like(p_ref)
    a_ref[...] = jnp.ones_like(a_ref)
    s_ref[...] = scores(0)

    def body(c, carry):
        weighted_values(jnp.maximum(c - 1, 0))
        s = s_ref[...]
        s_ref[...] = scores(jnp.minimum(c + 1, n - 1))
        m_prev = m_ref[...]
        m_new = jnp.maximum(m_prev, jnp.max(s, -1, keepdims=True))
        p_ref[...] = jnp.exp2(s - m_new).astype(BF16)
        a_ref[...] = jnp.exp2(m_prev - m_new)
        m_ref[...] = m_new
        return carry

    lax.fori_loop(0, n, body, 0)
    weighted_values(n - 1)
    acc = acc_ref[...]
    o = acc[:, :V_DIM] / acc[:, V_DIM:V_DIM + 1]
    o = o * lax.rsqrt(jnp.mean(o * o, -1, keepdims=True) + RMS_EPS) * g_ref[...]
    o_ref[0] = o.astype(BF16)


def _attention(q, k, v, bias, out_gain, tq, tk):
    B, Lp, _ = q.shape
    return pl.pallas_call(
        functools.partial(_attn_kernel, tk=tk),
        out_shape=jax.ShapeDtypeStruct((B, Lp, H_A * V_DIM), BF16),
        grid=(B, H_A, Lp // tq),
        in_specs=[pl.BlockSpec((1, tq, HEAD_W), lambda b, h, i: (b, i, h)),
                  pl.BlockSpec((1, Lp, HEAD_W), lambda b, h, i: (b, 0, h)),
                  pl.BlockSpec((1, Lp, V_DIM), lambda b, h, i: (b, 0, h)),
                  pl.BlockSpec((1, Lp), lambda b, h, i: (0, 0)),
                  pl.BlockSpec((1, V_DIM), lambda b, h, i: (0, h))],
        out_specs=pl.BlockSpec((1, tq, V_DIM), lambda b, h, i: (b, i, h)),
        scratch_shapes=[pltpu.VMEM((tq, 1), F32), pltpu.VMEM((tq, 2 * V_DIM), F32),
                        pltpu.VMEM((tq, tk), F32), pltpu.VMEM((tq, tk), BF16), pltpu.VMEM((tq, 1), F32)],
        compiler_params=_cparams(("parallel", "parallel", "parallel")),
        name="mla_attention",
    )(q, k, v, bias, out_gain.reshape(1, H_A * V_DIM))


def _prep_kernel(*refs, first, seq_len, tm):
    if first:
        (rw_ref, pv_ref, nx_ref, mup_ref, mun_ref, w2_ref, w0_ref, a2_ref, a0_ref, g2_ref, kk_ref, ka_ref, rk_ref,
         ones_ref, tri_ref, kk_o, v_o, wf_o, kdf_o, bf_o, rbf_o, ktf_o, btf_o, wb_o, kdb_o, bb_o, rbb_o, ktb_o, btb_o,
         g_o, bvg_o, vfirst_o) = refs
    else:
        (rw_ref, pv_ref, nx_ref, mup_ref, mun_ref, w2_ref, w0_ref, a2_ref, a0_ref, g2_ref, kk_ref, ka_ref, rk_ref,
         ones_ref, tri_ref, vf_ref, v0_ref, v1_ref, v2_ref,
         kk_o, v_o, wf_o, kdf_o, bf_o, rbf_o, ktf_o, btf_o, wb_o, kdb_o, bb_o, rbb_o, ktb_o, btb_o, g_o, bvg_o) = refs
    i = pl.program_id(1)
    n_t = pl.num_programs(1)
    row = lax.broadcasted_iota(jnp.int32, (tm, 1), 0)
    t = i * tm + row
    valid = t < seq_len
    rw = jnp.where(valid, rw_ref[0], 0.0)
    prev_row = jnp.where((i > 0) & (i * tm - 1 < seq_len), pv_ref[0, SUBLANES - 1:SUBLANES, :], 0.0)
    next_row = jnp.where((i < n_t - 1) & ((i + 1) * tm < seq_len), nx_ref[0, 0:1, :], 0.0)
    prev = jnp.where(row == 0, prev_row, pltpu.roll(rw, 1, 0))
    nxt = jnp.where(row == tm - 1, next_row, pltpu.roll(rw, tm - 1, 0))
    u = rw + mup_ref[...] * (prev - rw) + mun_ref[...] * (nxt - rw)
    r, k, v = u[:, :RW], u[:, RW:2 * RW], u[:, 2 * RW:3 * RW]
    wd = u[:, 3 * RW:3 * RW + LANES]
    ad = u[:, 3 * RW + LANES:3 * RW + 2 * LANES]
    gd = u[:, 3 * RW + 2 * LANES:]
    ones = ones_ref[...]
    if first:
        vfirst_o[0] = v
    else:
        low = jnp.dot(v.astype(BF16), v1_ref[...], preferred_element_type=F32)
        mix = jax.nn.sigmoid(v0_ref[...] + jnp.dot(low.astype(BF16), v2_ref[...], preferred_element_type=F32))
        v = v + (vf_ref[0] - v) * mix
    g = jnp.dot(jax.nn.sigmoid(gd).astype(BF16), g2_ref[...], preferred_element_type=F32)
    kk = k * kk_ref[...]
    kk = jnp.where(valid, kk * lax.rsqrt(_seg_sum(kk * kk, ones) + 1e-12), 0.0)
    wl = w0_ref[...] + jnp.dot(jnp.tanh(wd).astype(BF16), w2_ref[...], preferred_element_type=F32)
    logw = -math.exp(-0.5) * jax.nn.sigmoid(wl)
    decay = jnp.exp(logw)
    a = jax.nn.sigmoid(a0_ref[...] + jnp.dot(ad.astype(BF16), a2_ref[...], preferred_element_type=F32))
    ka = ka_ref[...]
    kd_f = jnp.where(valid, k * (1.0 + (a[:, :RW] - 1.0) * ka), 0.0)
    kd_b = jnp.where(valid, k * (1.0 + (a[:, RW:] - 1.0) * ka), 0.0)
    bonus = _seg_sum(r * (kd_f + kd_b) * rk_ref[...], ones)
    b_f, b_b = kk * a[:, :RW], kk * a[:, RW:]
    cum_f = _cumsum3(tri_ref[0], logw[:, :RW])
    cum_b = _cumsum3(tri_ref[1], logw[:, RW:])
    p_f, ip_f = jnp.exp(cum_f), jnp.exp(-cum_f)
    p_b, ip_b = jnp.exp(cum_b), jnp.exp(-cum_b)
    kk_o[0] = kk
    v_o[0] = jnp.where(valid, v, 0.0)
    wf_o[0] = decay[:, :RW]
    wb_o[0] = decay[:, RW:]
    kdf_o[0] = kd_f
    kdb_o[0] = kd_b
    bf_o[0] = b_f
    bb_o[0] = b_b
    rbf_o[0] = r * p_f
    rbb_o[0] = r * p_b
    ktf_o[0] = kd_f * ip_f
    ktb_o[0] = kd_b * ip_b
    btf_o[0] = b_f * ip_f
    btb_o[0] = b_b * ip_b
    g_o[0] = g
    bvg_o[0] = bonus * v * g


def _cumsum3(tri, x):
    h1 = x.astype(BF16)
    r1 = x - h1.astype(F32)
    h2 = r1.astype(BF16)
    h3 = (r1 - h2.astype(F32)).astype(BF16)
    return (jnp.dot(tri, h1, preferred_element_type=F32) + jnp.dot(tri, h2, preferred_element_type=F32)
            + jnp.dot(tri, h3, preferred_element_type=F32))


def _chunk_tri(tm):
    t = np.arange(tm)
    same = (t[:, None] // SCAN_CHUNK) == (t[None, :] // SCAN_CHUNK)
    return jnp.asarray(np.stack([same & (t[None, :] <= t[:, None]), same & (t[None, :] >= t[:, None])]), BF16)


def _block_diag2(a, b):
    z = jnp.zeros_like(a)
    return jnp.concatenate([jnp.concatenate([a, z], 1), jnp.concatenate([z, b], 1)], 0)


def _rwkv_prep(rw, seq_len, mu_prev, mu_next, w0, w2, a0, a2, g2, k_k, k_a, r_k, v_first, vres):
    B, Lp, n_rw = rw.shape
    tm = _row_tile(Lp)
    tpb = tm // SUBLANES
    first = vres is None
    vec = lambda a: a.reshape(1, -1)
    consts = [vec(mu_prev), vec(mu_next), _block_diag2(w2[0], w2[1]).astype(BF16), vec(w0),
              _block_diag2(a2[0], a2[1]).astype(BF16), vec(a0), g2.astype(BF16), vec(k_k), vec(k_a), vec(r_k),
              _seg_ones(RW), _chunk_tri(tm)]
    tile = lambda w: pl.BlockSpec((1, tm, w), lambda b, i: (b, i, 0))
    full = lambda a: pl.BlockSpec(a.shape, lambda b, i: (0,) * a.ndim)
    in_specs = [tile(n_rw),
                pl.BlockSpec((1, SUBLANES, n_rw), lambda b, i: (b, jnp.maximum(i * tpb - 1, 0), 0)),
                pl.BlockSpec((1, SUBLANES, n_rw), lambda b, i: (b, jnp.minimum((i + 1) * tpb, Lp // SUBLANES - 1), 0))]
    in_specs += [full(c) for c in consts]
    args = [rw, rw, rw] + consts
    n_out = 16
    if first:
        n_out += 1
    else:
        v0, v1, v2 = vres
        v1p = jnp.concatenate([v1, jnp.zeros((RW, LANES - VRES_LORA), F32)], 1).astype(BF16)
        v2p = jnp.concatenate([v2, jnp.zeros((LANES - VRES_LORA, RW), F32)], 0).astype(BF16)
        extra = [vec(v0), v1p, v2p]
        in_specs += [tile(RW)] + [full(c) for c in extra]
        args += [v_first] + extra
    out = pl.pallas_call(
        functools.partial(_prep_kernel, first=first, seq_len=seq_len, tm=tm),
        out_shape=tuple(jax.ShapeDtypeStruct((B, Lp, RW), F32) for _ in range(n_out)),
        grid=(B, Lp // tm),
        in_specs=in_specs,
        out_specs=tuple(tile(RW) for _ in range(n_out)),
        compiler_params=_cparams(("parallel", "parallel")),
        name="rwkv_prep",
    )(*args)
    return out


def _scan_kernel(kkf, wf, kdf, bf, vf, rbf, ktf, btf, kkb, wb, kdb, bb_, vb_, rbb, ktb, btb, ones_ref,
                 yf_ref, yb_ref, s_ref, s0_ref, vt_ref, u_ref, *, nb):
    j = pl.program_id(1)
    C = nb * 4
    half_t = TIME_BLOCK // 2
    assert half_t == SCAN_CHUNK

    @pl.when(j == 0)
    def _():
        s_ref[...] = jnp.zeros_like(s_ref)

    lane1 = lax.broadcasted_iota(jnp.int32, (N_B, LANES), 1)
    lo_half = lane1 < N_B

    for d, vref in enumerate((vf, vb_)):
        for n in range(nb):
            for hp in range(4):
                c = n * 4 + hp
                xt = vref[n, :, hp * LANES:(hp + 1) * LANES].T
                top, bot = xt[:N_B], xt[N_B:]
                vt_ref[d, 0, c * N_B:(c + 1) * N_B, :] = jnp.where(lo_half, top, pltpu.roll(bot, N_B, 1))
                vt_ref[d, 1, c * N_B:(c + 1) * N_B, :] = jnp.where(lo_half, pltpu.roll(top, N_B, 1), bot)

    ones = ones_ref[...]
    lane_id = lax.broadcasted_iota(jnp.int32, (C * N_B, LANES), 1)
    lane = lane_id % N_B
    head_base = (lane_id // N_B) * N_B
    row_refs = ((kkf, wf, kdf, bf), (kkb, wb, kdb, bb_))
    chunk_refs = ((vf, rbf, ktf, btf, yf_ref), (vb_, rbb, ktb, btb, yb_ref))
    tr = lax.broadcasted_iota(jnp.int32, (N_B, LANES), 0)
    tc = lane1 % N_B
    earlier2 = (tc <= tr, tc >= tr)
    lanes_dims = ((1,), (1,))

    def step(d, tiles, row, oh, tt):
        def rows(tile):
            return jnp.concatenate(
                [jnp.broadcast_to(tile[n, row:row + 1, hp * LANES:(hp + 1) * LANES], (N_B, LANES))
                 for n in range(nb) for hp in range(4)], axis=0)

        kk, w, kd, b = [rows(x) for x in tiles]
        S = s_ref[d]
        sa = _seg_sum(S * kk, ones)
        vb = jnp.take_along_axis(vt_ref[d, oh], head_base + tt, axis=1, mode="promise_in_bounds")
        s_ref[d] = S * w - sa * b + vb * kd
        u_ref[d] = jnp.where(lane == tt, sa, u_ref[d])

    def make_body(half):
        def body(g, carry):
            t0f = pl.multiple_of(half * half_t + g * SUBLANES, SUBLANES)
            t0b = pl.multiple_of(TIME_BLOCK - SUBLANES - (half * half_t + g * SUBLANES), SUBLANES)
            tiles_f = [x[:, pl.ds(t0f, SUBLANES), :] for x in row_refs[0]]
            tiles_b = [x[:, pl.ds(t0b, SUBLANES), :] for x in row_refs[1]]
            for i in range(SUBLANES):
                step(0, tiles_f, i, half, g * SUBLANES + i)
                step(1, tiles_b, SUBLANES - 1 - i, 1 - half, half_t - 1 - (g * SUBLANES + i))
            return carry
        return body

    def chunk_outputs(d, oh):
        vref, rref, ktref, btref, yref = chunk_refs[d]
        r0 = oh * half_t
        for n in range(nb):
            for hp in range(4):
                c = n * 4 + hp
                blk = lambda ref: ref[n, r0:r0 + half_t, hp * LANES:(hp + 1) * LANES]
                rb, kt, bt, v = blk(rref), blk(ktref), blk(btref), blk(vref)
                s0 = s0_ref[d, c * N_B:(c + 1) * N_B, :]
                ut = u_ref[d, c * N_B:(c + 1) * N_B, :]
                u2 = jnp.concatenate([ut, ut], axis=0).T
                r2 = jnp.concatenate([jnp.where(lo_half, rb, 0.0), jnp.where(lo_half, 0.0, rb)], axis=0)
                r2h, r2l = _split_bf16(r2)
                rhs = jnp.concatenate([kt, bt, s0, s0], axis=0).astype(BF16)
                gb = lax.dot_general(jnp.concatenate([r2h, r2l], axis=1), jnp.concatenate([rhs, rhs], axis=1),
                                     (lanes_dims, ((), ())), preferred_element_type=F32)
                vh = v.astype(BF16)
                per_head = []
                for h in range(2):
                    gh = gb[h * N_B:(h + 1) * N_B]
                    gh_, gl_ = _split_bf16(jnp.where(earlier2[d], gh[:, :LANES], 0.0))
                    w = jnp.concatenate([vh, (-u2[h * N_B:(h + 1) * N_B]).astype(BF16)], axis=0)
                    corr = jnp.dot(jnp.concatenate([gh_, gl_], axis=1), jnp.concatenate([w, w], axis=0),
                                   preferred_element_type=F32)
                    per_head.append(gh[:, LANES:] + corr)
                yref[n, r0:r0 + half_t, hp * LANES:(hp + 1) * LANES] = jnp.where(lo_half, per_head[0], per_head[1])

    for half in range(2):
        s0_ref[...] = s_ref[...]
        u_ref[...] = jnp.zeros_like(u_ref)
        lax.fori_loop(0, half_t // SUBLANES, make_body(half), 0)
        chunk_outputs(0, half)
        chunk_outputs(1, 1 - half)


def _wkv_bidir(kk, v, per_dir, nb):
    B, Lp, _ = kk.shape
    nblk = Lp // TIME_BLOCK
    ones = _seg_ones(LANES)
    fwd = lambda bi, j: (bi, j, 0)
    bwd = lambda bi, j: (bi, nblk - 1 - j, 0)
    blk = (nb, TIME_BLOCK, RW)
    (wf, kdf, bf, rbf, ktf, btf), (wb, kdb, bb_, rbb, ktb, btb) = per_dir
    y_shape = jax.ShapeDtypeStruct((B, Lp, RW), F32)
    rows = nb * 4 * N_B
    return pl.pallas_call(
        functools.partial(_scan_kernel, nb=nb),
        out_shape=(y_shape, y_shape),
        grid=(B // nb, nblk),
        in_specs=[pl.BlockSpec(blk, fwd)] * 8 + [pl.BlockSpec(blk, bwd)] * 8
                 + [pl.BlockSpec(ones.shape, lambda bi, j: (0, 0))],
        out_specs=(pl.BlockSpec(blk, fwd), pl.BlockSpec(blk, bwd)),
        scratch_shapes=[pltpu.VMEM((2, rows, LANES), F32),
                        pltpu.VMEM((2, rows, LANES), F32),
                        pltpu.VMEM((2, 2, rows, LANES), F32),
                        pltpu.VMEM((2, rows, LANES), F32)],
        compiler_params=_cparams(("parallel", "arbitrary")),
        name="wkv_scan",
    )(kk, wf, kdf, bf, v, rbf, ktf, btf, kk, wb, kdb, bb_, v, rbb, ktb, btb, ones)


def _post_kernel(yf_ref, yb_ref, g_ref, bvg_ref, att_ref, h_ref, gng_ref, gnb_ref, wo_ref, l1g_ref, l1b_ref,
                 wrh_ref, wrl_ref, rb_ref, tri_ref, ones_ref,
                 h1_ref, rf_ref, ri_ref, cnt_ref, *, alpha, tm):
    ones = ones_ref[...]
    y = yf_ref[...] + yb_ref[...]
    mu = _seg_sum(y, ones) * (1.0 / N_B)
    yc = y - mu
    var = _seg_sum(yc * yc, ones) * (1.0 / N_B)
    yr = (yc * lax.rsqrt(var + RWKV_GN_EPS) * gng_ref[...] + gnb_ref[...]) * g_ref[...] + bvg_ref[...]
    half = H_A * V_DIM
    mixed = (jnp.dot(att_ref[...], wo_ref[:half, :], preferred_element_type=F32)
             + jnp.dot(yr.astype(BF16), wo_ref[half:, :], preferred_element_type=F32))
    h1 = _ln(h_ref[...] * alpha + mixed, l1g_ref[...], l1b_ref[...])
    h1_ref[...] = h1

    xh, xl = _split_bf16(h1)
    x = (jnp.dot(xh, wrh_ref[...], preferred_element_type=F32) + jnp.dot(xl, wrh_ref[...], preferred_element_type=F32)
         + jnp.dot(xh, wrl_ref[...], preferred_element_type=F32)) + rb_ref[...]
    lane = lax.broadcasted_iota(jnp.int32, x.shape, 1)
    lanef = lane.astype(F32)
    big = float(LANES)
    gmask = lane < N_GROUPS
    gx = jnp.where(gmask, x, NEG)
    gmax = jnp.max(gx, -1, keepdims=True)
    gidx = jnp.min(jnp.where(gx == gmax, lanef, big), -1, keepdims=True)
    gsum = jnp.sum(jnp.where(gmask, jnp.exp(gx - gmax), 0.0), -1, keepdims=True)
    lo = N_GROUPS + EXPERTS_PER_GROUP * gidx
    emask = (lanef >= lo) & (lanef < lo + EXPERTS_PER_GROUP)
    ex = jnp.where(emask, x, NEG)
    m1 = jnp.max(ex, -1, keepdims=True)
    i1 = jnp.min(jnp.where(emask & (ex == m1), lanef, big), -1, keepdims=True)
    ex2 = jnp.where(lanef == i1, NEG, ex)
    m2 = jnp.max(ex2, -1, keepdims=True)
    i2 = jnp.min(jnp.where(emask & (lanef != i1) & (ex2 == m2), lanef, big), -1, keepdims=True)
    esum = jnp.sum(jnp.where(emask, jnp.exp(ex - m1), 0.0), -1, keepdims=True)
    gp = 1.0 / gsum
    gate0 = gp * (1.0 / esum)
    gate1 = gp * (jnp.exp(m2 - m1) / esum)
    e0 = i1 - N_GROUPS
    e1 = i2 - N_GROUPS

    @pl.when(pl.program_id(0) == 0)
    def _():
        cnt_ref[...] = jnp.zeros_like(cnt_ref)

    onehot = jnp.where((lanef == e0) | (lanef == e1), 1.0, 0.0)
    prefix = jnp.dot(tri_ref[...], onehot.astype(BF16), preferred_element_type=F32) + cnt_ref[...]
    r0 = jnp.sum(jnp.where(lanef == e0, prefix, 0.0), -1, keepdims=True)
    r1 = jnp.sum(jnp.where(lanef == e1, prefix, 0.0), -1, keepdims=True)
    cnt_ref[...] += jnp.sum(onehot, 0, keepdims=True)
    rf_ref[...] = jnp.where(lane == 0, gate0, jnp.where(lane == 1, gate1, 0.0))
    ri = jnp.where(lane == 0, e0, jnp.where(lane == 1, e1, jnp.where(lane == 2, r0, jnp.where(lane == 3, r1, 0.0))))
    ri_ref[...] = ri.astype(jnp.int32)


def _post_mixer(yf, yb, g, bvg, att, h, gn_g, gn_b, w_out, ln_g, ln_b, w_group, b_group, w_expert, b_expert, alpha):
    T, D = h.shape
    tm = _row_tile(T)
    vec = lambda a: a.reshape(1, -1)
    zpad = LANES - N_GROUPS - N_EXPERTS
    wr = jnp.concatenate([w_group, w_expert, jnp.zeros((D, zpad), F32)], axis=1)
    wrh = wr.astype(BF16)
    wrl = (wr - wrh.astype(F32)).astype(BF16)
    rb = jnp.concatenate([b_group, b_expert, jnp.zeros((zpad,), F32)]).reshape(1, LANES)
    tri = jnp.asarray(np.arange(tm)[:, None] > np.arange(tm)[None, :], BF16)
    consts = [vec(gn_g), vec(gn_b), w_out.astype(BF16), vec(ln_g), vec(ln_b), wrh, wrl, rb, tri, _seg_ones(RW)]
    row = lambda w: pl.BlockSpec((tm, w), lambda i: (i, 0))
    full = lambda a: pl.BlockSpec(a.shape, lambda i: (0, 0))
    return pl.pallas_call(
        functools.partial(_post_kernel, alpha=alpha, tm=tm),
        out_shape=(jax.ShapeDtypeStruct((T, D), F32),
                   jax.ShapeDtypeStruct((T, LANES), F32),
                   jax.ShapeDtypeStruct((T, LANES), jnp.int32),
                   jax.ShapeDtypeStruct((1, LANES), F32)),
        grid=(T // tm,),
        in_specs=[row(RW)] * 4 + [row(H_A * V_DIM), row(D)] + [full(c) for c in consts],
        out_specs=(row(D), row(LANES), row(LANES), pl.BlockSpec((1, LANES), lambda i: (0, 0))),
        compiler_params=_cparams(("arbitrary",)),
        name="post_mixer",
    )(yf, yb, g, bvg, att, h, *consts)


ROUTE_COLS = 2 * TOP_K


def _row_copies(ps_ref, idx_smem, t, make):
    return [make(s, ps_ref[idx_smem[ROUTE_COLS * t + s]] + idx_smem[ROUTE_COLS * t + TOP_K + s]) for s in range(TOP_K)]


def _dispatch_kernel(ps_ref, idx_hbm, h_ref, xz_hbm, xb_hbm, idx_smem, isem, sem, *, tm):
    del xz_hbm
    i = pl.program_id(0)
    load = pltpu.make_async_copy(idx_hbm.at[pl.ds(i * tm * ROUTE_COLS, tm * ROUTE_COLS)], idx_smem, isem)
    load.start()
    load.wait()

    def copies(t):
        src = h_ref.at[pl.ds(t, 1)]
        return _row_copies(ps_ref, idx_smem, t, lambda s, dst: pltpu.make_async_copy(src, xb_hbm.at[pl.ds(dst, 1)], sem))

    def start(t, c):
        for cp in copies(t):
            cp.start()
        return c

    def wait(t, c):
        for cp in copies(t):
            cp.wait()
        return c

    lax.fori_loop(0, tm, start, 0, unroll=8)
    lax.fori_loop(0, tm, wait, 0, unroll=8)


def _dispatch(pstart, idx, h, n_rows):
    T, D = h.shape
    tm = _row_tile(T)
    any_spec = pl.BlockSpec(memory_space=pl.ANY)
    grid_spec = pltpu.PrefetchScalarGridSpec(
        num_scalar_prefetch=1,
        grid=(T // tm,),
        in_specs=[any_spec, pl.BlockSpec((tm, D), lambda i, ps: (i, 0)), any_spec],
        out_specs=any_spec,
        scratch_shapes=[pltpu.SMEM((tm * ROUTE_COLS,), jnp.int32), pltpu.SemaphoreType.DMA, pltpu.SemaphoreType.DMA],
    )
    return pl.pallas_call(
        functools.partial(_dispatch_kernel, tm=tm),
        out_shape=jax.ShapeDtypeStruct((n_rows, D), F32),
        grid_spec=grid_spec,
        input_output_aliases={3: 0},
        compiler_params=_cparams(("arbitrary",)),
        name="moe_dispatch",
    )(pstart, idx, h, jnp.zeros((n_rows, D), F32))


def _expert_kernel(be_ref, nu_ref, x_ref, w1_ref, w3_ref, w2_ref, o_ref):
    @pl.when(pl.program_id(0) < nu_ref[0])
    def _():
        x = x_ref[...].astype(BF16)
        a = jnp.dot(x, w1_ref[0], preferred_element_type=F32)
        b = jnp.dot(x, w3_ref[0], preferred_element_type=F32)
        hid = (a * jax.nn.sigmoid(a)) * b
        o_ref[...] = jnp.dot(hid.astype(BF16), w2_ref[0], preferred_element_type=F32)


def _expert_ffn(xb, block_e, n_used, w1, w3, w2):
    n_rows, D = xb.shape
    n_blocks = n_rows // MOE_BLOCK
    blk = lambda i, be, nu: jnp.minimum(i, nu[0] - 1)
    wmap = lambda i, be, nu: (be[blk(i, be, nu)], 0, 0)
    grid_spec = pltpu.PrefetchScalarGridSpec(
        num_scalar_prefetch=2,
        grid=(n_blocks,),
        in_specs=[pl.BlockSpec((MOE_BLOCK, D), lambda i, be, nu: (blk(i, be, nu), 0)),
                  pl.BlockSpec((1, D, E_HID), wmap),
                  pl.BlockSpec((1, D, E_HID), wmap),
                  pl.BlockSpec((1, E_HID, D), wmap)],
        out_specs=pl.BlockSpec((MOE_BLOCK, D), lambda i, be, nu: (blk(i, be, nu), 0)),
    )
    return pl.pallas_call(
        _expert_kernel,
        out_shape=jax.ShapeDtypeStruct((n_rows, D), F32),
        grid_spec=grid_spec,
        compiler_params=_cparams(("arbitrary",)),
        name="moe_experts",
    )(block_e, n_used, xb, w1, w3, w2)


def _combine_kernel(ps_ref, idx_hbm, yb_hbm, gate_ref, h_ref, g_ref, b_ref, o_ref, idx_smem, buf, isem, sem,
                    *, tm, alpha):
    i = pl.program_id(0)
    load = pltpu.make_async_copy(idx_hbm.at[pl.ds(i * tm * ROUTE_COLS, tm * ROUTE_COLS)], idx_smem, isem)
    load.start()
    load.wait()

    def copies(t):
        return _row_copies(ps_ref, idx_smem, t, lambda s, src: pltpu.make_async_copy(
            yb_hbm.at[pl.ds(src, 1)], buf.at[s, pl.ds(t, 1)], sem))

    def start(t, c):
        for cp in copies(t):
            cp.start()
        return c

    def wait(t, c):
        for cp in copies(t):
            cp.wait()
        return c

    lax.fori_loop(0, tm, start, 0, unroll=8)
    lax.fori_loop(0, tm, wait, 0, unroll=8)
    gate = gate_ref[...]
    ff = sum(buf[s] * gate[:, s:s + 1] for s in range(TOP_K))
    o_ref[...] = _ln(h_ref[...] * alpha + ff, g_ref[...], b_ref[...])


def _combine(pstart, idx, yb, route_f, h, ln_g, ln_b, alpha):
    T, D = h.shape
    tm = _row_tile(T)
    any_spec = pl.BlockSpec(memory_space=pl.ANY)
    row = lambda w: pl.BlockSpec((tm, w), lambda i, ps: (i, 0))
    vec = pl.BlockSpec((1, D), lambda i, ps: (0, 0))
    grid_spec = pltpu.PrefetchScalarGridSpec(
        num_scalar_prefetch=1,
        grid=(T // tm,),
        in_specs=[any_spec, any_spec, row(LANES), row(D), vec, vec],
        out_specs=row(D),
        scratch_shapes=[pltpu.SMEM((tm * ROUTE_COLS,), jnp.int32), pltpu.VMEM((TOP_K, tm, D), F32),
                        pltpu.SemaphoreType.DMA, pltpu.SemaphoreType.DMA],
    )
    return pl.pallas_call(
        functools.partial(_combine_kernel, tm=tm, alpha=alpha),
        out_shape=jax.ShapeDtypeStruct((T, D), F32),
        grid_spec=grid_spec,
        compiler_params=_cparams(("arbitrary",)),
        name="moe_combine",
    )(pstart, idx, yb, route_f, h, ln_g.reshape(1, D), ln_b.reshape(1, D))


def _hier_moe(h, route_f, route_i, counts, w1, w3, w2, ln_g, ln_b, alpha):
    T, D = h.shape
    counts = counts[0, :N_EXPERTS].astype(jnp.int32)
    padded = (counts + MOE_BLOCK - 1) // MOE_BLOCK * MOE_BLOCK
    pends = jnp.cumsum(padded)
    pstart = pends - padded
    n_rows = (T * TOP_K + N_EXPERTS * (MOE_BLOCK - 1) + MOE_BLOCK - 1) // MOE_BLOCK * MOE_BLOCK
    n_blocks = n_rows // MOE_BLOCK
    block_start = jnp.arange(n_blocks, dtype=jnp.int32) * MOE_BLOCK
    block_e = jnp.minimum(jnp.sum(pends[None, :] <= block_start[:, None], axis=1), N_EXPERTS - 1).astype(jnp.int32)
    n_used = (pends[-1:] // MOE_BLOCK).astype(jnp.int32)
    idx = route_i[:, :ROUTE_COLS].reshape(-1)
    xb = _dispatch(pstart, idx, h, n_rows)
    yb = _expert_ffn(xb, block_e, n_used, w1, w3, w2)
    return _combine(pstart, idx, yb, route_f, h, ln_g, ln_b, alpha)


def kernel(x, positions, meta_tokens, emb_ln_g, emb_ln_b, w_in, mla_q_norm, mla_kv_norm, mla_w_uq, mla_w_ukv, mla_out_norm, rwkv_mu_prev, rwkv_mu_next, rwkv_w0, rwkv_w2, rwkv_a0, rwkv_a2, rwkv_g2, rwkv_k_k, rwkv_k_a, rwkv_r_k, rwkv_gn_g, rwkv_gn_b, rwkv_v0, rwkv_v1, rwkv_v2, w_out, ln1_g, ln1_b, moe_w_group, moe_b_group, moe_w_expert, moe_b_expert, moe_w1, moe_w3, moe_w2, ln2_g, ln2_b):
    B, seq, D = x.shape
    depth = w_in.shape[0]
    alpha = (2 * depth) ** 0.25
    L = seq + N_META
    Lp = -(-L // LANES) * LANES
    T = B * Lp
    nb = 2 if B % 2 == 0 else 1
    tk = 384 if Lp % 384 == 0 else LANES
    tq = next(t for t in (1408, 1152, 768, 384, LANES) if Lp % t == 0)

    meta = jnp.broadcast_to(meta_tokens.astype(x.dtype)[None], (B, N_META, D))
    h = jnp.concatenate([meta, x, jnp.zeros((B, Lp - L, D), x.dtype)], axis=1).reshape(T, D)
    h = _ln_residual(h, jnp.zeros_like(h), emb_ln_g, emb_ln_b, 1.0)

    key_bias = jnp.where(jnp.arange(Lp) < L, 0.0, NEG).astype(F32)[None, :]
    pos = jnp.concatenate([jnp.broadcast_to(jnp.arange(N_META, dtype=jnp.int32), (B, N_META)),
                           positions + N_META, jnp.zeros((B, Lp - L), jnp.int32)], axis=1)
    inv_freq = ROPE_THETA ** (-jnp.arange(0, QK_ROPE, 2, dtype=F32) / QK_ROPE)
    ang = pos.astype(F32)[..., None] * inv_freq
    cos, sin = jnp.cos(ang).reshape(T, -1), jnp.sin(ang).reshape(T, -1)
    zeros = jnp.zeros((T, LANES - QK_ROPE), F32)
    rope_cos = jnp.concatenate([cos, cos, zeros], axis=1)
    rope_sin = jnp.concatenate([-sin, sin, zeros], axis=1)
    qscale = (QK_NOPE + QK_ROPE) ** -0.5 * math.log2(math.e)

    v_first = None
    for li in range(depth):
        q, k, v, rw = _project(h, w_in[li], mla_q_norm[li], mla_kv_norm[li], mla_w_uq[li], mla_w_ukv[li],
                               rope_cos, rope_sin, qscale)
        y_att = _attention(q.reshape(B, Lp, -1), k.reshape(B, Lp, -1), v.reshape(B, Lp, -1),
                           key_bias, mla_out_norm[li], tq, tk)
        vres = None if li == 0 else (rwkv_v0[li - 1], rwkv_v1[li - 1], rwkv_v2[li - 1])
        outs = _rwkv_prep(rw.reshape(B, Lp, -1), L, rwkv_mu_prev[li], rwkv_mu_next[li], rwkv_w0[li], rwkv_w2[li],
                          rwkv_a0[li], rwkv_a2[li], rwkv_g2[li], rwkv_k_k[li], rwkv_k_a[li], rwkv_r_k[li],
                          v_first, vres)
        kk, vm, g, bvg = outs[0], outs[1], outs[14], outs[15]
        if li == 0:
            v_first = outs[16]
        y_f, y_b = _wkv_bidir(kk, vm, (outs[2:8], outs[8:14]), nb)
        flat = lambda a: a.reshape(T, -1)
        h, route_f, route_i, counts = _post_mixer(
            flat(y_f), flat(y_b), flat(g), flat(bvg), flat(y_att), h, rwkv_gn_g[li], rwkv_gn_b[li], w_out[li],
            ln1_g[li], ln1_b[li], moe_w_group[li], moe_b_group[li], moe_w_expert[li], moe_b_expert[li], alpha)
        h = _hier_moe(h, route_f, route_i, counts,
                      moe_w1[li].astype(BF16), moe_w3[li].astype(BF16), moe_w2[li].astype(BF16),
                      ln2_g[li], ln2_b[li], alpha)
    return h.reshape(B, Lp, D)[:, N_META:L]
```

```python
import functools
import math

import numpy as np
import jax
import jax.numpy as jnp
from jax import lax
from jax.experimental import pallas as pl
from jax.experimental.pallas import tpu as pltpu

F32 = jnp.float32
BF16 = jnp.bfloat16

N_META = 16
H_A = 4
QK_NOPE = 128
QK_ROPE = 64
V_DIM = 128
Q_LORA = 256
KV_LORA = 128
ROPE_THETA = 10000.0
H_B = 8
N_B = 64
RW = H_B * N_B
DECAY_LORA = 64
AAA_LORA = 64
GATE_LORA = 128
VRES_LORA = 32
RWKV_GN_EPS = 64e-5
MLA_COLS = Q_LORA + KV_LORA + QK_ROPE
N_GROUPS = 4
EXPERTS_PER_GROUP = 8
N_EXPERTS = N_GROUPS * EXPERTS_PER_GROUP
TOP_K = 2
E_HID = 256
MOE_BLOCK = 256
LN_EPS = 1e-5
RMS_EPS = 1e-6

LANES = 128
SUBLANES = 8
TIME_BLOCK = 128
SCAN_CHUNK = 64
ROW_TILE = 512
VMEM_LIMIT = 56 * 1024 * 1024
HEAD_W = 2 * LANES
NEG = -1e30


def _cparams(sem):
    return pltpu.CompilerParams(dimension_semantics=sem, vmem_limit_bytes=VMEM_LIMIT)


def _row_tile(m):
    return next(t for t in (ROW_TILE, 384, 256, LANES) if m % t == 0)


def _split_bf16(x):
    hi = x.astype(BF16)
    lo = (x - hi.astype(F32)).astype(BF16)
    return hi, lo


def _seg_sum(x, ones):
    hi, lo = _split_bf16(x)
    return jnp.dot(jnp.concatenate([hi, lo], axis=-1), ones, preferred_element_type=F32)


def _seg_ones(width):
    m = np.arange(2 * width)[:, None] % width
    n = np.arange(width)[None, :]
    return jnp.asarray(m // N_B == n // N_B, BF16)


def _ln(x, g, b):
    mu = jnp.mean(x, -1, keepdims=True)
    xc = x - mu
    var = jnp.mean(xc * xc, -1, keepdims=True)
    return xc * lax.rsqrt(var + LN_EPS) * g + b


def _ln_kernel(x_ref, r_ref, g_ref, b_ref, o_ref, *, alpha):
    o_ref[...] = _ln(x_ref[...] * alpha + r_ref[...], g_ref[...], b_ref[...])


def _ln_residual(x, r, g, b, alpha):
    M, D = x.shape
    tm = _row_tile(M)
    row = pl.BlockSpec((tm, D), lambda i: (i, 0))
    vec = pl.BlockSpec((1, D), lambda i: (0, 0))
    return pl.pallas_call(
        functools.partial(_ln_kernel, alpha=alpha),
        out_shape=jax.ShapeDtypeStruct((M, D), F32),
        grid=(M // tm,),
        in_specs=[row, row, vec, vec],
        out_specs=row,
        compiler_params=_cparams(("parallel",)),
        name="layer_norm",
    )(x, r, g.reshape(1, D), b.reshape(1, D))


def _rms(x, g):
    return x * lax.rsqrt(jnp.mean(x * x, -1, keepdims=True) + RMS_EPS) * g


def _proj_kernel(h_ref, win_ref, qn_ref, kvn_ref, wuq_ref, wukv_ref, c_ref, s_ref,
                 q_ref, k_ref, v_ref, rw_ref, *, qscale):
    proj = jnp.dot(h_ref[...].astype(BF16), win_ref[...], preferred_element_type=F32)
    rw_ref[...] = proj[:, MLA_COLS + N_B:]
    q = jnp.dot(_rms(proj[:, :Q_LORA], qn_ref[...]).astype(BF16), wuq_ref[...], preferred_element_type=F32)
    kv = jnp.dot(_rms(proj[:, Q_LORA:Q_LORA + KV_LORA], kvn_ref[...]).astype(BF16), wukv_ref[...],
                 preferred_element_type=F32)
    cos, sin = c_ref[...], s_ref[...]
    lane = lax.broadcasted_iota(jnp.int32, cos.shape, 1)
    half = QK_ROPE // 2

    def rope(x):
        partner = jnp.where(lane < half, pltpu.roll(x, LANES - half, 1), pltpu.roll(x, half, 1))
        return x * cos + partner * sin

    k_pe = rope(proj[:, Q_LORA + KV_LORA:Q_LORA + KV_LORA + LANES]).astype(BF16)
    for h in range(H_A):
        o = h * HEAD_W
        q_ref[:, o:o + LANES] = (q[:, o:o + LANES] * qscale).astype(BF16)
        q_ref[:, o + LANES:o + HEAD_W] = (rope(q[:, o + LANES:o + HEAD_W]) * qscale).astype(BF16)
        k_ref[:, o:o + LANES] = kv[:, o:o + LANES].astype(BF16)
        k_ref[:, o + LANES:o + HEAD_W] = k_pe
        v_ref[:, h * V_DIM:(h + 1) * V_DIM] = kv[:, o + LANES:o + HEAD_W].astype(BF16)


def _project(h, w_in, q_norm, kv_norm, w_uq, w_ukv, rope_cos, rope_sin, qscale):
    T, D = h.shape
    tm = _row_tile(T)
    n_rw = w_in.shape[1] - MLA_COLS
    pad = jnp.zeros((D, N_B), F32)
    win = jnp.concatenate([w_in[:, :MLA_COLS], pad, w_in[:, MLA_COLS:]], axis=1).astype(BF16)
    wq = w_uq.reshape(Q_LORA, H_A, QK_NOPE + QK_ROPE)
    wq = jnp.concatenate([wq, jnp.zeros((Q_LORA, H_A, HEAD_W - QK_NOPE - QK_ROPE), F32)], axis=-1)
    wq = wq.reshape(Q_LORA, H_A * HEAD_W).astype(BF16)
    row = lambda w: pl.BlockSpec((tm, w), lambda i: (i, 0))
    full = lambda a: pl.BlockSpec(a.shape, lambda i: (0, 0))
    args = (h, win, q_norm.reshape(1, -1), kv_norm.reshape(1, -1), wq, w_ukv.astype(BF16), rope_cos, rope_sin)
    return pl.pallas_call(
        functools.partial(_proj_kernel, qscale=qscale),
        out_shape=(jax.ShapeDtypeStruct((T, H_A * HEAD_W), BF16),
                   jax.ShapeDtypeStruct((T, H_A * HEAD_W), BF16),
                   jax.ShapeDtypeStruct((T, H_A * V_DIM), BF16),
                   jax.ShapeDtypeStruct((T, n_rw), F32)),
        grid=(T // tm,),
        in_specs=[row(D)] + [full(a) for a in args[1:6]] + [row(LANES), row(LANES)],
        out_specs=(row(H_A * HEAD_W), row(H_A * HEAD_W), row(H_A * V_DIM), row(n_rw)),
        compiler_params=_cparams(("parallel",)),
        name="in_proj",
    )(*args)


def _attn_kernel(q_ref, k_ref, v_ref, bias_ref, g_ref, o_ref, m_ref, acc_ref, s_ref, p_ref, a_ref, *, tk):
    q = q_ref[0]
    n = k_ref.shape[1] // tk
    one_col = (lax.broadcasted_iota(jnp.int32, (tk, V_DIM), 1) == 0).astype(BF16)

    def scores(c):
        off = pl.multiple_of(c * tk, tk)
        s = lax.dot_general(q, k_ref[0, pl.ds(off, tk), :], (((1,), (1,)), ((), ())), preferred_element_type=F32)
        return s + bias_ref[:, pl.ds(off, tk)]

    def weighted_values(c):
        off = pl.multiple_of(c * tk, tk)
        v1 = jnp.concatenate([v_ref[0, pl.ds(off, tk), :], one_col], axis=1)
        acc_ref[...] = a_ref[...] * acc_ref[...] + jnp.dot(p_ref[...], v1, preferred_element_type=F32)

    m_ref[...] = jnp.full_like(m_ref, NEG)
    acc_ref[...] = jnp.zeros_like(acc_ref)
    p_ref[...] = jnp.zeros_like(p_ref)
    a_ref[...] = jnp.ones_like(a_ref)
    s_ref[...] = scores(0)

    def body(c, carry):
        weighted_values(jnp.maximum(c - 1, 0))
        s = s_ref[...]
        s_ref[...] = scores(jnp.minimum(c + 1, n - 1))
        m_prev = m_ref[...]
        m_new = jnp.maximum(m_prev, jnp.max(s, -1, keepdims=True))
        p_ref[...] = jnp.exp2(s - m_new).astype(BF16)
        a_ref[...] = jnp.exp2(m_prev - m_new)
        m_ref[...] = m_new
        return carry

    lax.fori_loop(0, n, body, 0)
    weighted_values(n - 1)
    acc = acc_ref[...]
    o = acc[:, :V_DIM] / acc[:, V_DIM:V_DIM + 1]
    o = o * lax.rsqrt(jnp.mean(o * o, -1, keepdims=True) + RMS_EPS) * g_ref[...]
    o_ref[0] = o.astype(BF16)


def _attention(q, k, v, bias, out_gain, tq, tk):
    B, Lp, _ = q.shape
    return pl.pallas_call(
        functools.partial(_attn_kernel, tk=tk),
        out_shape=jax.ShapeDtypeStruct((B, Lp, H_A * V_DIM), BF16),
        grid=(B, H_A, Lp // tq),
        in_specs=[pl.BlockSpec((1, tq, HEAD_W), lambda b, h, i: (b, i, h)),
                  pl.BlockSpec((1, Lp, HEAD_W), lambda b, h, i: (b, 0, h)),
                  pl.BlockSpec((1, Lp, V_DIM), lambda b, h, i: (b, 0, h)),
                  pl.BlockSpec((1, Lp), lambda b, h, i: (0, 0)),
                  pl.BlockSpec((1, V_DIM), lambda b, h, i: (0, h))],
        out_specs=pl.BlockSpec((1, tq, V_DIM), lambda b, h, i: (b, i, h)),
        scratch_shapes=[pltpu.VMEM((tq, 1), F32), pltpu.VMEM((tq, 2 * V_DIM), F32),
                        pltpu.VMEM((tq, tk), F32), pltpu.VMEM((tq, tk), BF16), pltpu.VMEM((tq, 1), F32)],
        compiler_params=_cparams(("parallel", "parallel", "parallel")),
        name="mla_attention",
    )(q, k, v, bias, out_gain.reshape(1, H_A * V_DIM))


def _prep_kernel(*refs, first, seq_len, tm):
    if first:
        (rw_ref, pv_ref, nx_ref, mup_ref, mun_ref, w2_ref, w0_ref, a2_ref, a0_ref, g2_ref, kk_ref, ka_ref, rk_ref,
         ones_ref, tri_ref, kk_o, v_o, wf_o, kdf_o, bf_o, rbf_o, ktf_o, btf_o, wb_o, kdb_o, bb_o, rbb_o, ktb_o, btb_o,
         g_o, bvg_o, vfirst_o) = refs
    else:
        (rw_ref, pv_ref, nx_ref, mup_ref, mun_ref, w2_ref, w0_ref, a2_ref, a0_ref, g2_ref, kk_ref, ka_ref, rk_ref,
         ones_ref, tri_ref, vf_ref, v0_ref, v1_ref, v2_ref,
         kk_o, v_o, wf_o, kdf_o, bf_o, rbf_o, ktf_o, btf_o, wb_o, kdb_o, bb_o, rbb_o, ktb_o, btb_o, g_o, bvg_o) = refs
    i = pl.program_id(1)
    n_t = pl.num_programs(1)
    row = lax.broadcasted_iota(jnp.int32, (tm, 1), 0)
    t = i * tm + row
    valid = t < seq_len
    rw = jnp.where(valid, rw_ref[0], 0.0)
    prev_row = jnp.where((i > 0) & (i * tm - 1 < seq_len), pv_ref[0, SUBLANES - 1:SUBLANES, :], 0.0)
    next_row = jnp.where((i < n_t - 1) & ((i + 1) * tm < seq_len), nx_ref[0, 0:1, :], 0.0)
    prev = jnp.where(row == 0, prev_row, pltpu.roll(rw, 1, 0))
    nxt = jnp.where(row == tm - 1, next_row, pltpu.roll(rw, tm - 1, 0))
    u = rw + mup_ref[...] * (prev - rw) + mun_ref[...] * (nxt - rw)
    r, k, v = u[:, :RW], u[:, RW:2 * RW], u[:, 2 * RW:3 * RW]
    wd = u[:, 3 * RW:3 * RW + LANES]
    ad = u[:, 3 * RW + LANES:3 * RW + 2 * LANES]
    gd = u[:, 3 * RW + 2 * LANES:]
    ones = ones_ref[...]
    if first:
        vfirst_o[0] = v
    else:
        low = jnp.dot(v.astype(BF16), v1_ref[...], preferred_element_type=F32)
        mix = jax.nn.sigmoid(v0_ref[...] + jnp.dot(low.astype(BF16), v2_ref[...], preferred_element_type=F32))
        v = v + (vf_ref[0] - v) * mix
    g = jnp.dot(jax.nn.sigmoid(gd).astype(BF16), g2_ref[...], preferred_element_type=F32)
    kk = k * kk_ref[...]
    kk = jnp.where(valid, kk * lax.rsqrt(_seg_sum(kk * kk, ones) + 1e-12), 0.0)
    wl = w0_ref[...] + jnp.dot(jnp.tanh(wd).astype(BF16), w2_ref[...], preferred_element_type=F32)
    logw = -math.exp(-0.5) * jax.nn.sigmoid(wl)
    decay = jnp.exp(logw)
    a = jax.nn.sigmoid(a0_ref[...] + jnp.dot(ad.astype(BF16), a2_ref[...], preferred_element_type=F32))
    ka = ka_ref[...]
    kd_f = jnp.where(valid, k * (1.0 + (a[:, :RW] - 1.0) * ka), 0.0)
    kd_b = jnp.where(valid, k * (1.0 + (a[:, RW:] - 1.0) * ka), 0.0)
    bonus = _seg_sum(r * (kd_f + kd_b) * rk_ref[...], ones)
    b_f, b_b = kk * a[:, :RW], kk * a[:, RW:]
    cum_f = _cumsum3(tri_ref[0], logw[:, :RW])
    cum_b = _cumsum3(tri_ref[1], logw[:, RW:])
    p_f, ip_f = jnp.exp(cum_f), jnp.exp(-cum_f)
    p_b, ip_b = jnp.exp(cum_b), jnp.exp(-cum_b)
    kk_o[0] = kk
    v_o[0] = jnp.where(valid, v, 0.0)
    wf_o[0] = decay[:, :RW]
    wb_o[0] = decay[:, RW:]
    kdf_o[0] = kd_f
    kdb_o[0] = kd_b
    bf_o[0] = b_f
    bb_o[0] = b_b
    rbf_o[0] = r * p_f
    rbb_o[0] = r * p_b
    ktf_o[0] = kd_f * ip_f
    ktb_o[0] = kd_b * ip_b
    btf_o[0] = b_f * ip_f
    btb_o[0] = b_b * ip_b
    g_o[0] = g
    bvg_o[0] = bonus * v * g


def _cumsum3(tri, x):
    h1 = x.astype(BF16)
    r1 = x - h1.astype(F32)
    h2 = r1.astype(BF16)
    h3 = (r1 - h2.astype(F32)).astype(BF16)
    return (jnp.dot(tri, h1, preferred_element_type=F32) + jnp.dot(tri, h2, preferred_element_type=F32)
            + jnp.dot(tri, h3, preferred_element_type=F32))


def _chunk_tri(tm):
    t = np.arange(tm)
    same = (t[:, None] // SCAN_CHUNK) == (t[None, :] // SCAN_CHUNK)
    return jnp.asarray(np.stack([same & (t[None, :] <= t[:, None]), same & (t[None, :] >= t[:, None])]), BF16)


def _block_diag2(a, b):
    z = jnp.zeros_like(a)
    return jnp.concatenate([jnp.concatenate([a, z], 1), jnp.concatenate([z, b], 1)], 0)


def _rwkv_prep(rw, seq_len, mu_prev, mu_next, w0, w2, a0, a2, g2, k_k, k_a, r_k, v_first, vres):
    B, Lp, n_rw = rw.shape
    tm = _row_tile(Lp)
    tpb = tm // SUBLANES
    first = vres is None
    vec = lambda a: a.reshape(1, -1)
    consts = [vec(mu_prev), vec(mu_next), _block_diag2(w2[0], w2[1]).astype(BF16), vec(w0),
              _block_diag2(a2[0], a2[1]).astype(BF16), vec(a0), g2.astype(BF16), vec(k_k), vec(k_a), vec(r_k),
              _seg_ones(RW), _chunk_tri(tm)]
    tile = lambda w: pl.BlockSpec((1, tm, w), lambda b, i: (b, i, 0))
    full = lambda a: pl.BlockSpec(a.shape, lambda b, i: (0,) * a.ndim)
    in_specs = [tile(n_rw),
                pl.BlockSpec((1, SUBLANES, n_rw), lambda b, i: (b, jnp.maximum(i * tpb - 1, 0), 0)),
                pl.BlockSpec((1, SUBLANES, n_rw), lambda b, i: (b, jnp.minimum((i + 1) * tpb, Lp // SUBLANES - 1), 0))]
    in_specs += [full(c) for c in consts]
    args = [rw, rw, rw] + consts
    n_out = 16
    if first:
        n_out += 1
    else:
        v0, v1, v2 = vres
        v1p = jnp.concatenate([v1, jnp.zeros((RW, LANES - VRES_LORA), F32)], 1).astype(BF16)
        v2p = jnp.concatenate([v2, jnp.zeros((LANES - VRES_LORA, RW), F32)], 0).astype(BF16)
        extra = [vec(v0), v1p, v2p]
        in_specs += [tile(RW)] + [full(c) for c in extra]
        args += [v_first] + extra
    out = pl.pallas_call(
        functools.partial(_prep_kernel, first=first, seq_len=seq_len, tm=tm),
        out_shape=tuple(jax.ShapeDtypeStruct((B, Lp, RW), F32) for _ in range(n_out)),
        grid=(B, Lp // tm),
        in_specs=in_specs,
        out_specs=tuple(tile(RW) for _ in range(n_out)),
        compiler_params=_cparams(("parallel", "parallel")),
        name="rwkv_prep",
    )(*args)
    return out


def _scan_kernel(kkf, wf, kdf, bf, vf, rbf, ktf, btf, kkb, wb, kdb, bb_, vb_, rbb, ktb, btb, ones_ref,
                 yf_ref, yb_ref, s_ref, s0_ref, vt_ref, u_ref, *, nb):
    j = pl.program_id(1)
    C = nb * 4
    half_t = TIME_BLOCK // 2
    assert half_t == SCAN_CHUNK

    @pl.when(j == 0)
    def _():
        s_ref[...] = jnp.zeros_like(s_ref)

    lane1 = lax.broadcasted_iota(jnp.int32, (N_B, LANES), 1)
    lo_half = lane1 < N_B

    for d, vref in enumerate((vf, vb_)):
        for n in range(nb):
            for hp in range(4):
                c = n * 4 + hp
                xt = vref[n, :, hp * LANES:(hp + 1) * LANES].T
                top, bot = xt[:N_B], xt[N_B:]
                vt_ref[d, 0, c * N_B:(c + 1) * N_B, :] = jnp.where(lo_half, top, pltpu.roll(bot, N_B, 1))
                vt_ref[d, 1, c * N_B:(c + 1) * N_B, :] = jnp.where(lo_half, pltpu.roll(top, N_B, 1), bot)

    ones = ones_ref[...]
    lane_id = lax.broadcasted_iota(jnp.int32, (C * N_B, LANES), 1)
    lane = lane_id % N_B
    head_base = (lane_id // N_B) * N_B
    row_refs = ((kkf, wf, kdf, bf), (kkb, wb, kdb, bb_))
    chunk_refs = ((vf, rbf, ktf, btf, yf_ref), (vb_, rbb, ktb, btb, yb_ref))
    tr = lax.broadcasted_iota(jnp.int32, (N_B, LANES), 0)
    tc = lane1 % N_B
    earlier2 = (tc <= tr, tc >= tr)
    lanes_dims = ((1,), (1,))

    def step(d, tiles, row, oh, tt):
        def rows(tile):
            return jnp.concatenate(
                [jnp.broadcast_to(tile[n, row:row + 1, hp * LANES:(hp + 1) * LANES], (N_B, LANES))
                 for n in range(nb) for hp in range(4)], axis=0)

        kk, w, kd, b = [rows(x) for x in tiles]
        S = s_ref[d]
        sa = _seg_sum(S * kk, ones)
        vb = jnp.take_along_axis(vt_ref[d, oh], head_base + tt, axis=1, mode="promise_in_bounds")
        s_ref[d] = S * w - sa * b + vb * kd
        u_ref[d] = jnp.where(lane == tt, sa, u_ref[d])

    def make_body(half):
        def body(g, carry):
            t0f = pl.multiple_of(half * half_t + g * SUBLANES, SUBLANES)
            t0b = pl.multiple_of(TIME_BLOCK - SUBLANES - (half * half_t + g * SUBLANES), SUBLANES)
            tiles_f = [x[:, pl.ds(t0f, SUBLANES), :] for x in row_refs[0]]
            tiles_b = [x[:, pl.ds(t0b, SUBLANES), :] for x in row_refs[1]]
            for i in range(SUBLANES):
                step(0, tiles_f, i, half, g * SUBLANES + i)
                step(1, tiles_b, SUBLANES - 1 - i, 1 - half, half_t - 1 - (g * SUBLANES + i))
            return carry
        return body

    def chunk_outputs(d, oh):
        vref, rref, ktref, btref, yref = chunk_refs[d]
        r0 = oh * half_t
        for n in range(nb):
            for hp in range(4):
                c = n * 4 + hp
                blk = lambda ref: ref[n, r0:r0 + half_t, hp * LANES:(hp + 1) * LANES]
                rb, kt, bt, v = blk(rref), blk(ktref), blk(btref), blk(vref)
                s0 = s0_ref[d, c * N_B:(c + 1) * N_B, :]
                ut = u_ref[d, c * N_B:(c + 1) * N_B, :]
                u2 = jnp.concatenate([ut, ut], axis=0).T
                r2 = jnp.concatenate([jnp.where(lo_half, rb, 0.0), jnp.where(lo_half, 0.0, rb)], axis=0)
                r2h, r2l = _split_bf16(r2)
                rhs = jnp.concatenate([kt, bt, s0, s0], axis=0).astype(BF16)
                gb = lax.dot_general(jnp.concatenate([r2h, r2l], axis=1), jnp.concatenate([rhs, rhs], axis=1),
                                     (lanes_dims, ((), ())), preferred_element_type=F32)
                vh = v.astype(BF16)
                per_head = []
                for h in range(2):
                    gh = gb[h * N_B:(h + 1) * N_B]
                    gh_, gl_ = _split_bf16(jnp.where(earlier2[d], gh[:, :LANES], 0.0))
                    w = jnp.concatenate([vh, (-u2[h * N_B:(h + 1) * N_B]).astype(BF16)], axis=0)
                    corr = jnp.dot(jnp.concatenate([gh_, gl_], axis=1), jnp.concatenate([w, w], axis=0),
                                   preferred_element_type=F32)
                    per_head.append(gh[:, LANES:] + corr)
                yref[n, r0:r0 + half_t, hp * LANES:(hp + 1) * LANES] = jnp.where(lo_half, per_head[0], per_head[1])

    for half in range(2):
        s0_ref[...] = s_ref[...]
        u_ref[...] = jnp.zeros_like(u_ref)
        lax.fori_loop(0, half_t // SUBLANES, make_body(half), 0)
        chunk_outputs(0, half)
        chunk_outputs(1, 1 - half)


def _wkv_bidir(kk, v, per_dir, nb):
    B, Lp, _ = kk.shape
    nblk = Lp // TIME_BLOCK
    ones = _seg_ones(LANES)
    fwd = lambda bi, j: (bi, j, 0)
    bwd = lambda bi, j: (bi, nblk - 1 - j, 0)
    blk = (nb, TIME_BLOCK, RW)
    (wf, kdf, bf, rbf, ktf, btf), (wb, kdb, bb_, rbb, ktb, btb) = per_dir
    y_shape = jax.ShapeDtypeStruct((B, Lp, RW), F32)
    rows = nb * 4 * N_B
    return pl.pallas_call(
        functools.partial(_scan_kernel, nb=nb),
        out_shape=(y_shape, y_shape),
        grid=(B // nb, nblk),
        in_specs=[pl.BlockSpec(blk, fwd)] * 8 + [pl.BlockSpec(blk, bwd)] * 8
                 + [pl.BlockSpec(ones.shape, lambda bi, j: (0, 0))],
        out_specs=(pl.BlockSpec(blk, fwd), pl.BlockSpec(blk, bwd)),
        scratch_shapes=[pltpu.VMEM((2, rows, LANES), F32),
                        pltpu.VMEM((2, rows, LANES), F32),
                        pltpu.VMEM((2, 2, rows, LANES), F32),
                        pltpu.VMEM((2, rows, LANES), F32)],
        compiler_params=_cparams(("parallel", "arbitrary")),
        name="wkv_scan",
    )(kk, wf, kdf, bf, v, rbf, ktf, btf, kk, wb, kdb, bb_, v, rbb, ktb, btb, ones)


def _post_kernel(yf_ref, yb_ref, g_ref, bvg_ref, att_ref, h_ref, gng_ref, gnb_ref, wo_ref, l1g_ref, l1b_ref,
                 wrh_ref, wrl_ref, rb_ref, tri_ref, ones_ref,
                 h1_ref, rf_ref, ri_ref, cnt_ref, *, alpha, tm):
    ones = ones_ref[...]
    y = yf_ref[...] + yb_ref[...]
    mu = _seg_sum(y, ones) * (1.0 / N_B)
    yc = y - mu
    var = _seg_sum(yc * yc, ones) * (1.0 / N_B)
    yr = (yc * lax.rsqrt(var + RWKV_GN_EPS) * gng_ref[...] + gnb_ref[...]) * g_ref[...] + bvg_ref[...]
    half = H_A * V_DIM
    mixed = (jnp.dot(att_ref[...], wo_ref[:half, :], preferred_element_type=F32)
             + jnp.dot(yr.astype(BF16), wo_ref[half:, :], preferred_element_type=F32))
    h1 = _ln(h_ref[...] * alpha + mixed, l1g_ref[...], l1b_ref[...])
    h1_ref[...] = h1

    xh, xl = _split_bf16(h1)
    x = (jnp.dot(xh, wrh_ref[...], preferred_element_type=F32) + jnp.dot(xl, wrh_ref[...], preferred_element_type=F32)
         + jnp.dot(xh, wrl_ref[...], preferred_element_type=F32)) + rb_ref[...]
    lane = lax.broadcasted_iota(jnp.int32, x.shape, 1)
    lanef = lane.astype(F32)
    big = float(LANES)
    gmask = lane < N_GROUPS
    gx = jnp.where(gmask, x, NEG)
    gmax = jnp.max(gx, -1, keepdims=True)
    gidx = jnp.min(jnp.where(gx == gmax, lanef, big), -1, keepdims=True)
    gsum = jnp.sum(jnp.where(gmask, jnp.exp(gx - gmax), 0.0), -1, keepdims=True)
    lo = N_GROUPS + EXPERTS_PER_GROUP * gidx
    emask = (lanef >= lo) & (lanef < lo + EXPERTS_PER_GROUP)
    ex = jnp.where(emask, x, NEG)
    m1 = jnp.max(ex, -1, keepdims=True)
    i1 = jnp.min(jnp.where(emask & (ex == m1), lanef, big), -1, keepdims=True)
    ex2 = jnp.where(lanef == i1, NEG, ex)
    m2 = jnp.max(ex2, -1, keepdims=True)
    i2 = jnp.min(jnp.where(emask & (lanef != i1) & (ex2 == m2), lanef, big), -1, keepdims=True)
    esum = jnp.sum(jnp.where(emask, jnp.exp(ex - m1), 0.0), -1, keepdims=True)
    gp = 1.0 / gsum
    gate0 = gp * (1.0 / esum)
    gate1 = gp * (jnp.exp(m2 - m1) / esum)
    e0 = i1 - N_GROUPS
    e1 = i2 - N_GROUPS

    @pl.when(pl.program_id(0) == 0)
    def _():
        cnt_ref[...] = jnp.zeros_like(cnt_ref)

    onehot = jnp.where((lanef == e0) | (lanef == e1), 1.0, 0.0)
    prefix = jnp.dot(tri_ref[...], onehot.astype(BF16), preferred_element_type=F32) + cnt_ref[...]
    r0 = jnp.sum(jnp.where(lanef == e0, prefix, 0.0), -1, keepdims=True)
    r1 = jnp.sum(jnp.where(lanef == e1, prefix, 0.0), -1, keepdims=True)
    cnt_ref[...] += jnp.sum(onehot, 0, keepdims=True)
    rf_ref[...] = jnp.where(lane == 0, gate0, jnp.where(lane == 1, gate1, 0.0))
    ri = jnp.where(lane == 0, e0, jnp.where(lane == 1, e1, jnp.where(lane == 2, r0, jnp.where(lane == 3, r1, 0.0))))
    ri_ref[...] = ri.astype(jnp.int32)


def _post_mixer(yf, yb, g, bvg, att, h, gn_g, gn_b, w_out, ln_g, ln_b, w_group, b_group, w_expert, b_expert, alpha):
    T, D = h.shape
    tm = _row_tile(T)
    vec = lambda a: a.reshape(1, -1)
    zpad = LANES - N_GROUPS - N_EXPERTS
    wr = jnp.concatenate([w_group, w_expert, jnp.zeros((D, zpad), F32)], axis=1)
    wrh = wr.astype(BF16)
    wrl = (wr - wrh.astype(F32)).astype(BF16)
    rb = jnp.concatenate([b_group, b_expert, jnp.zeros((zpad,), F32)]).reshape(1, LANES)
    tri = jnp.asarray(np.arange(tm)[:, None] > np.arange(tm)[None, :], BF16)
    consts = [vec(gn_g), vec(gn_b), w_out.astype(BF16), vec(ln_g), vec(ln_b), wrh, wrl, rb, tri, _seg_ones(RW)]
    row = lambda w: pl.BlockSpec((tm, w), lambda i: (i, 0))
    full = lambda a: pl.BlockSpec(a.shape, lambda i: (0, 0))
    return pl.pallas_call(
        functools.partial(_post_kernel, alpha=alpha, tm=tm),
        out_shape=(jax.ShapeDtypeStruct((T, D), F32),
                   jax.ShapeDtypeStruct((T, LANES), F32),
                   jax.ShapeDtypeStruct((T, LANES), jnp.int32),
                   jax.ShapeDtypeStruct((1, LANES), F32)),
        grid=(T // tm,),
        in_specs=[row(RW)] * 4 + [row(H_A * V_DIM), row(D)] + [full(c) for c in consts],
        out_specs=(row(D), row(LANES), row(LANES), pl.BlockSpec((1, LANES), lambda i: (0, 0))),
        compiler_params=_cparams(("arbitrary",)),
        name="post_mixer",
    )(yf, yb, g, bvg, att, h, *consts)


def _row_copies(dest_smem, t, make):
    return [make(s, dest_smem[TOP_K * t + s]) for s in range(TOP_K)]


def _load_dest(dest_hbm, dest_smem, isem, tm):
    load = pltpu.make_async_copy(dest_hbm.at[pl.ds(pl.program_id(0) * tm * TOP_K, tm * TOP_K)], dest_smem, isem)
    load.start()
    load.wait()


def _dispatch_kernel(dest_hbm, h_ref, xz_hbm, xb_hbm, dest_smem, isem, sem, *, tm):
    del xz_hbm
    _load_dest(dest_hbm, dest_smem, isem, tm)

    def copies(t):
        src = h_ref.at[pl.ds(t, 1)]
        return _row_copies(dest_smem, t, lambda s, dst: pltpu.make_async_copy(src, xb_hbm.at[pl.ds(dst, 1)], sem))

    def start(t, c):
        for cp in copies(t):
            cp.start()
        return c

    def wait(t, c):
        for cp in copies(t):
            cp.wait()
        return c

    lax.fori_loop(0, tm, start, 0, unroll=8)
    lax.fori_loop(0, tm, wait, 0, unroll=8)


def _dispatch(dest, h, n_rows):
    T, D = h.shape
    tm = _row_tile(T)
    any_spec = pl.BlockSpec(memory_space=pl.ANY)
    return pl.pallas_call(
        functools.partial(_dispatch_kernel, tm=tm),
        out_shape=jax.ShapeDtypeStruct((n_rows, D), F32),
        grid=(T // tm,),
        in_specs=[any_spec, pl.BlockSpec((tm, D), lambda i: (i, 0)), any_spec],
        out_specs=any_spec,
        scratch_shapes=[pltpu.SMEM((tm * TOP_K,), jnp.int32), pltpu.SemaphoreType.DMA, pltpu.SemaphoreType.DMA],
        input_output_aliases={2: 0},
        compiler_params=_cparams(("arbitrary",)),
        name="moe_dispatch",
    )(dest, h, jnp.zeros((n_rows, D), F32))


def _expert_kernel(be_ref, nu_ref, x_ref, w1_ref, w3_ref, w2_ref, o_ref):
    @pl.when(pl.program_id(0) < nu_ref[0])
    def _():
        x = x_ref[...].astype(BF16)
        a = jnp.dot(x, w1_ref[0], preferred_element_type=F32)
        b = jnp.dot(x, w3_ref[0], preferred_element_type=F32)
        hid = (a * jax.nn.sigmoid(a)) * b
        o_ref[...] = jnp.dot(hid.astype(BF16), w2_ref[0], preferred_element_type=F32)


def _expert_ffn(xb, block_e, n_used, w1, w3, w2):
    n_rows, D = xb.shape
    n_blocks = n_rows // MOE_BLOCK
    blk = lambda i, be, nu: jnp.minimum(i, nu[0] - 1)
    wmap = lambda i, be, nu: (be[blk(i, be, nu)], 0, 0)
    grid_spec = pltpu.PrefetchScalarGridSpec(
        num_scalar_prefetch=2,
        grid=(n_blocks,),
        in_specs=[pl.BlockSpec((MOE_BLOCK, D), lambda i, be, nu: (blk(i, be, nu), 0)),
                  pl.BlockSpec((1, D, E_HID), wmap),
                  pl.BlockSpec((1, D, E_HID), wmap),
                  pl.BlockSpec((1, E_HID, D), wmap)],
        out_specs=pl.BlockSpec((MOE_BLOCK, D), lambda i, be, nu: (blk(i, be, nu), 0)),
    )
    return pl.pallas_call(
        _expert_kernel,
        out_shape=jax.ShapeDtypeStruct((n_rows, D), F32),
        grid_spec=grid_spec,
        compiler_params=_cparams(("arbitrary",)),
        name="moe_experts",
    )(block_e, n_used, xb, w1, w3, w2)


def _combine_kernel(dest_hbm, yb_hbm, gate_ref, h_ref, g_ref, b_ref, o_ref, dest_smem, buf, isem, sem, *, tm, alpha):
    _load_dest(dest_hbm, dest_smem, isem, tm)

    def copies(t):
        return _row_copies(dest_smem, t, lambda s, src: pltpu.make_async_copy(
            yb_hbm.at[pl.ds(src, 1)], buf.at[s, pl.ds(t, 1)], sem))

    def start(t, c):
        for cp in copies(t):
            cp.start()
        return c

    def wait(t, c):
        for cp in copies(t):
            cp.wait()
        return c

    lax.fori_loop(0, tm, start, 0, unroll=8)
    lax.fori_loop(0, tm, wait, 0, unroll=8)
    gate = gate_ref[...]
    ff = sum(buf[s] * gate[:, s:s + 1] for s in range(TOP_K))
    o_ref[...] = _ln(h_ref[...] * alpha + ff, g_ref[...], b_ref[...])


def _combine(dest, yb, route_f, h, ln_g, ln_b, alpha):
    T, D = h.shape
    tm = _row_tile(T)
    any_spec = pl.BlockSpec(memory_space=pl.ANY)
    row = lambda w: pl.BlockSpec((tm, w), lambda i: (i, 0))
    vec = pl.BlockSpec((1, D), lambda i: (0, 0))
    return pl.pallas_call(
        functools.partial(_combine_kernel, tm=tm, alpha=alpha),
        out_shape=jax.ShapeDtypeStruct((T, D), F32),
        grid=(T // tm,),
        in_specs=[any_spec, any_spec, row(LANES), row(D), vec, vec],
        out_specs=row(D),
        scratch_shapes=[pltpu.SMEM((tm * TOP_K,), jnp.int32), pltpu.VMEM((TOP_K, tm, D), F32),
                        pltpu.SemaphoreType.DMA, pltpu.SemaphoreType.DMA],
        compiler_params=_cparams(("arbitrary",)),
        name="moe_combine",
    )(dest, yb, route_f, h, ln_g.reshape(1, D), ln_b.reshape(1, D))


def _hier_moe(h, route_f, route_i, counts, w1, w3, w2, ln_g, ln_b, alpha):
    T, D = h.shape
    counts = counts[0, :N_EXPERTS].astype(jnp.int32)
    padded = (counts + MOE_BLOCK - 1) // MOE_BLOCK * MOE_BLOCK
    pends = jnp.cumsum(padded)
    pstart = pends - padded
    n_rows = (T * TOP_K + N_EXPERTS * (MOE_BLOCK - 1) + MOE_BLOCK - 1) // MOE_BLOCK * MOE_BLOCK
    n_blocks = n_rows // MOE_BLOCK
    block_start = jnp.arange(n_blocks, dtype=jnp.int32) * MOE_BLOCK
    block_e = jnp.minimum(jnp.sum(pends[None, :] <= block_start[:, None], axis=1), N_EXPERTS - 1).astype(jnp.int32)
    n_used = (pends[-1:] // MOE_BLOCK).astype(jnp.int32)
    dest = (jnp.take(pstart, route_i[:, :TOP_K]) + route_i[:, TOP_K:2 * TOP_K]).reshape(-1)
    xb = _dispatch(dest, h, n_rows)
    yb = _expert_ffn(xb, block_e, n_used, w1, w3, w2)
    return _combine(dest, yb, route_f, h, ln_g, ln_b, alpha)


def kernel(x, positions, meta_tokens, emb_ln_g, emb_ln_b, w_in, mla_q_norm, mla_kv_norm, mla_w_uq, mla_w_ukv, mla_out_norm, rwkv_mu_prev, rwkv_mu_next, rwkv_w0, rwkv_w2, rwkv_a0, rwkv_a2, rwkv_g2, rwkv_k_k, rwkv_k_a, rwkv_r_k, rwkv_gn_g, rwkv_gn_b, rwkv_v0, rwkv_v1, rwkv_v2, w_out, ln1_g, ln1_b, moe_w_group, moe_b_group, moe_w_expert, moe_b_expert, moe_w1, moe_w3, moe_w2, ln2_g, ln2_b):
    B, seq, D = x.shape
    depth = w_in.shape[0]
    alpha = (2 * depth) ** 0.25
    L = seq + N_META
    Lp = -(-L // LANES) * LANES
    T = B * Lp
    nb = next(n for n in (4, 2, 1) if B % n == 0)
    tk = 384 if Lp % 384 == 0 else LANES
    tq = next(t for t in (1408, 1152, 768, 384, LANES) if Lp % t == 0)

    meta = jnp.broadcast_to(meta_tokens.astype(x.dtype)[None], (B, N_META, D))
    h = jnp.concatenate([meta, x, jnp.zeros((B, Lp - L, D), x.dtype)], axis=1).reshape(T, D)
    h = _ln_residual(h, jnp.zeros_like(h), emb_ln_g, emb_ln_b, 1.0)

    key_bias = jnp.where(jnp.arange(Lp) < L, 0.0, NEG).astype(F32)[None, :]
    pos = jnp.concatenate([jnp.broadcast_to(jnp.arange(N_META, dtype=jnp.int32), (B, N_META)),
                           positions + N_META, jnp.zeros((B, Lp - L), jnp.int32)], axis=1)
    inv_freq = ROPE_THETA ** (-jnp.arange(0, QK_ROPE, 2, dtype=F32) / QK_ROPE)
    ang = pos.astype(F32)[..., None] * inv_freq
    cos, sin = jnp.cos(ang).reshape(T, -1), jnp.sin(ang).reshape(T, -1)
    zeros = jnp.zeros((T, LANES - QK_ROPE), F32)
    rope_cos = jnp.concatenate([cos, cos, zeros], axis=1)
    rope_sin = jnp.concatenate([-sin, sin, zeros], axis=1)
    qscale = (QK_NOPE + QK_ROPE) ** -0.5 * math.log2(math.e)

    v_first = None
    for li in range(depth):
        q, k, v, rw = _project(h, w_in[li], mla_q_norm[li], mla_kv_norm[li], mla_w_uq[li], mla_w_ukv[li],
                               rope_cos, rope_sin, qscale)
        y_att = _attention(q.reshape(B, Lp, -1), k.reshape(B, Lp, -1), v.reshape(B, Lp, -1),
                           key_bias, mla_out_norm[li], tq, tk)
        vres = None if li == 0 else (rwkv_v0[li - 1], rwkv_v1[li - 1], rwkv_v2[li - 1])
        outs = _rwkv_prep(rw.reshape(B, Lp, -1), L, rwkv_mu_prev[li], rwkv_mu_next[li], rwkv_w0[li], rwkv_w2[li],
                          rwkv_a0[li], rwkv_a2[li], rwkv_g2[li], rwkv_k_k[li], rwkv_k_a[li], rwkv_r_k[li],
                          v_first, vres)
        kk, vm, g, bvg = outs[0], outs[1], outs[14], outs[15]
        if li == 0:
            v_first = outs[16]
        y_f, y_b = _wkv_bidir(kk, vm, (outs[2:8], outs[8:14]), nb)
        flat = lambda a: a.reshape(T, -1)
        h, route_f, route_i, counts = _post_mixer(
            flat(y_f), flat(y_b), flat(g), flat(bvg), flat(y_att), h, rwkv_gn_g[li], rwkv_gn_b[li], w_out[li],
            ln1_g[li], ln1_b[li], moe_w_group[li], moe_b_group[li], moe_w_expert[li], moe_b_expert[li], alpha)
        h = _hier_moe(h, route_f, route_i, counts,
                      moe_w1[li].astype(BF16), moe_w3[li].astype(BF16), moe_w2[li].astype(BF16),
                      ln2_g[li], ln2_b[li], alpha)
    return h.reshape(B, Lp, D)[:, N_META:L]
```

```python
import functools
import math

import numpy as np
import jax
import jax.numpy as jnp
from jax import lax
from jax.experimental import pallas as pl
from jax.experimental.pallas import tpu as pltpu

F32 = jnp.float32
BF16 = jnp.bfloat16

N_META = 16
H_A = 4
QK_NOPE = 128
QK_ROPE = 64
V_DIM = 128
Q_LORA = 256
KV_LORA = 128
ROPE_THETA = 10000.0
H_B = 8
N_B = 64
RW = H_B * N_B
DECAY_LORA = 64
AAA_LORA = 64
GATE_LORA = 128
VRES_LORA = 32
RWKV_GN_EPS = 64e-5
MLA_COLS = Q_LORA + KV_LORA + QK_ROPE
N_GROUPS = 4
EXPERTS_PER_GROUP = 8
N_EXPERTS = N_GROUPS * EXPERTS_PER_GROUP
TOP_K = 2
E_HID = 256
MOE_BLOCK = 256
LN_EPS = 1e-5
RMS_EPS = 1e-6

LANES = 128
SUBLANES = 8
TIME_BLOCK = 128
SCAN_CHUNK = 64
ROW_TILE = 512
VMEM_LIMIT = 56 * 1024 * 1024
HEAD_W = 2 * LANES
NEG = -1e30


def _cparams(sem):
    return pltpu.CompilerParams(dimension_semantics=sem, vmem_limit_bytes=VMEM_LIMIT)


def _row_tile(m):
    return next(t for t in (ROW_TILE, 384, 256, LANES) if m % t == 0)


def _split_bf16(x):
    hi = x.astype(BF16)
    lo = (x - hi.astype(F32)).astype(BF16)
    return hi, lo


def _seg_sum(x, ones):
    hi, lo = _split_bf16(x)
    return jnp.dot(jnp.concatenate([hi, lo], axis=-1), ones, preferred_element_type=F32)


def _seg_ones(width):
    m = np.arange(2 * width)[:, None] % width
    n = np.arange(width)[None, :]
    return jnp.asarray(m // N_B == n // N_B, BF16)


def _ln(x, g, b):
    mu = jnp.mean(x, -1, keepdims=True)
    xc = x - mu
    var = jnp.mean(xc * xc, -1, keepdims=True)
    return xc * lax.rsqrt(var + LN_EPS) * g + b


def _ln_kernel(x_ref, r_ref, g_ref, b_ref, o_ref, *, alpha):
    o_ref[...] = _ln(x_ref[...] * alpha + r_ref[...], g_ref[...], b_ref[...])


def _ln_residual(x, r, g, b, alpha):
    M, D = x.shape
    tm = _row_tile(M)
    row = pl.BlockSpec((tm, D), lambda i: (i, 0))
    vec = pl.BlockSpec((1, D), lambda i: (0, 0))
    return pl.pallas_call(
        functools.partial(_ln_kernel, alpha=alpha),
        out_shape=jax.ShapeDtypeStruct((M, D), F32),
        grid=(M // tm,),
        in_specs=[row, row, vec, vec],
        out_specs=row,
        compiler_params=_cparams(("parallel",)),
        name="layer_norm",
    )(x, r, g.reshape(1, D), b.reshape(1, D))


def _rms(x, g):
    return x * lax.rsqrt(jnp.mean(x * x, -1, keepdims=True) + RMS_EPS) * g


def _proj_kernel(h_ref, win_ref, qn_ref, kvn_ref, wuq_ref, wukv_ref, c_ref, s_ref,
                 q_ref, k_ref, v_ref, rw_ref, *, qscale):
    proj = jnp.dot(h_ref[...].astype(BF16), win_ref[...], preferred_element_type=F32)
    rw_ref[...] = proj[:, MLA_COLS + N_B:]
    q = jnp.dot(_rms(proj[:, :Q_LORA], qn_ref[...]).astype(BF16), wuq_ref[...], preferred_element_type=F32)
    kv = jnp.dot(_rms(proj[:, Q_LORA:Q_LORA + KV_LORA], kvn_ref[...]).astype(BF16), wukv_ref[...],
                 preferred_element_type=F32)
    cos, sin = c_ref[...], s_ref[...]
    lane = lax.broadcasted_iota(jnp.int32, cos.shape, 1)
    half = QK_ROPE // 2

    def rope(x):
        partner = jnp.where(lane < half, pltpu.roll(x, LANES - half, 1), pltpu.roll(x, half, 1))
        return x * cos + partner * sin

    k_pe = rope(proj[:, Q_LORA + KV_LORA:Q_LORA + KV_LORA + LANES]).astype(BF16)
    for h in range(H_A):
        o = h * HEAD_W
        q_ref[:, o:o + LANES] = (q[:, o:o + LANES] * qscale).astype(BF16)
        q_ref[:, o + LANES:o + HEAD_W] = (rope(q[:, o + LANES:o + HEAD_W]) * qscale).astype(BF16)
        k_ref[:, o:o + LANES] = kv[:, o:o + LANES].astype(BF16)
        k_ref[:, o + LANES:o + HEAD_W] = k_pe
        v_ref[:, h * V_DIM:(h + 1) * V_DIM] = kv[:, o + LANES:o + HEAD_W].astype(BF16)


def _project(h, w_in, q_norm, kv_norm, w_uq, w_ukv, rope_cos, rope_sin, qscale):
    T, D = h.shape
    tm = _row_tile(T)
    n_rw = w_in.shape[1] - MLA_COLS
    pad = jnp.zeros((D, N_B), F32)
    win = jnp.concatenate([w_in[:, :MLA_COLS], pad, w_in[:, MLA_COLS:]], axis=1).astype(BF16)
    wq = w_uq.reshape(Q_LORA, H_A, QK_NOPE + QK_ROPE)
    wq = jnp.concatenate([wq, jnp.zeros((Q_LORA, H_A, HEAD_W - QK_NOPE - QK_ROPE), F32)], axis=-1)
    wq = wq.reshape(Q_LORA, H_A * HEAD_W).astype(BF16)
    row = lambda w: pl.BlockSpec((tm, w), lambda i: (i, 0))
    full = lambda a: pl.BlockSpec(a.shape, lambda i: (0, 0))
    args = (h, win, q_norm.reshape(1, -1), kv_norm.reshape(1, -1), wq, w_ukv.astype(BF16), rope_cos, rope_sin)
    return pl.pallas_call(
        functools.partial(_proj_kernel, qscale=qscale),
        out_shape=(jax.ShapeDtypeStruct((T, H_A * HEAD_W), BF16),
                   jax.ShapeDtypeStruct((T, H_A * HEAD_W), BF16),
                   jax.ShapeDtypeStruct((T, H_A * V_DIM), BF16),
                   jax.ShapeDtypeStruct((T, n_rw), F32)),
        grid=(T // tm,),
        in_specs=[row(D)] + [full(a) for a in args[1:6]] + [row(LANES), row(LANES)],
        out_specs=(row(H_A * HEAD_W), row(H_A * HEAD_W), row(H_A * V_DIM), row(n_rw)),
        compiler_params=_cparams(("parallel",)),
        name="in_proj",
    )(*args)


def _attn_kernel(q_ref, k_ref, v_ref, bias_ref, g_ref, o_ref, m_ref, acc_ref, s_ref, p_ref, a_ref, *, tk):
    q = q_ref[0]
    n = k_ref.shape[1] // tk
    one_col = (lax.broadcasted_iota(jnp.int32, (tk, V_DIM), 1) == 0).astype(BF16)

    def scores(c):
        off = pl.multiple_of(c * tk, tk)
        s = lax.dot_general(q, k_ref[0, pl.ds(off, tk), :], (((1,), (1,)), ((), ())), preferred_element_type=F32)
        return s + bias_ref[:, pl.ds(off, tk)]

    def weighted_values(c):
        off = pl.multiple_of(c * tk, tk)
        v1 = jnp.concatenate([v_ref[0, pl.ds(off, tk), :], one_col], axis=1)
        acc_ref[...] = a_ref[...] * acc_ref[...] + jnp.dot(p_ref[...], v1, preferred_element_type=F32)

    m_ref[...] = jnp.full_like(m_ref, NEG)
    acc_ref[...] = jnp.zeros_like(acc_ref)
    p_ref[...] = jnp.zeros_like(p_ref)
    a_ref[...] = jnp.ones_like(a_ref)
    s_ref[...] = scores(0)

    def body(c, carry):
        weighted_values(jnp.maximum(c - 1, 0))
        s = s_ref[...]
        s_ref[...] = scores(jnp.minimum(c + 1, n - 1))
        m_prev = m_ref[...]
        m_new = jnp.maximum(m_prev, jnp.max(s, -1, keepdims=True))
        p_ref[...] = jnp.exp2(s - m_new).astype(BF16)
        a_ref[...] = jnp.exp2(m_prev - m_new)
        m_ref[...] = m_new
        return carry

    lax.fori_loop(0, n, body, 0)
    weighted_values(n - 1)
    acc = acc_ref[...]
    o = acc[:, :V_DIM] / acc[:, V_DIM:V_DIM + 1]
    o = o * lax.rsqrt(jnp.mean(o * o, -1, keepdims=True) + RMS_EPS) * g_ref[...]
    o_ref[0] = o.astype(BF16)


def _attention(q, k, v, bias, out_gain, tq, tk):
    B, Lp, _ = q.shape
    return pl.pallas_call(
        functools.partial(_attn_kernel, tk=tk),
        out_shape=jax.ShapeDtypeStruct((B, Lp, H_A * V_DIM), BF16),
        grid=(B, H_A, Lp // tq),
        in_specs=[pl.BlockSpec((1, tq, HEAD_W), lambda b, h, i: (b, i, h)),
                  pl.BlockSpec((1, Lp, HEAD_W), lambda b, h, i: (b, 0, h)),
                  pl.BlockSpec((1, Lp, V_DIM), lambda b, h, i: (b, 0, h)),
                  pl.BlockSpec((1, Lp), lambda b, h, i: (0, 0)),
                  pl.BlockSpec((1, V_DIM), lambda b, h, i: (0, h))],
        out_specs=pl.BlockSpec((1, tq, V_DIM), lambda b, h, i: (b, i, h)),
        scratch_shapes=[pltpu.VMEM((tq, 1), F32), pltpu.VMEM((tq, 2 * V_DIM), F32),
                        pltpu.VMEM((tq, tk), F32), pltpu.VMEM((tq, tk), BF16), pltpu.VMEM((tq, 1), F32)],
        compiler_params=_cparams(("parallel", "parallel", "parallel")),
        name="mla_attention",
    )(q, k, v, bias, out_gain.reshape(1, H_A * V_DIM))


def _prep_kernel(*refs, first, seq_len, tm):
    if first:
        (rw_ref, pv_ref, nx_ref, mup_ref, mun_ref, w2_ref, w0_ref, a2_ref, a0_ref, g2_ref, kk_ref, ka_ref, rk_ref,
         ones_ref, tri_ref, kk_o, v_o, wf_o, kdf_o, bf_o, rbf_o, ktf_o, btf_o, wb_o, kdb_o, bb_o, rbb_o, ktb_o, btb_o,
         g_o, bvg_o, vfirst_o) = refs
    else:
        (rw_ref, pv_ref, nx_ref, mup_ref, mun_ref, w2_ref, w0_ref, a2_ref, a0_ref, g2_ref, kk_ref, ka_ref, rk_ref,
         ones_ref, tri_ref, vf_ref, v0_ref, v1_ref, v2_ref,
         kk_o, v_o, wf_o, kdf_o, bf_o, rbf_o, ktf_o, btf_o, wb_o, kdb_o, bb_o, rbb_o, ktb_o, btb_o, g_o, bvg_o) = refs
    i = pl.program_id(1)
    n_t = pl.num_programs(1)
    row = lax.broadcasted_iota(jnp.int32, (tm, 1), 0)
    t = i * tm + row
    valid = t < seq_len
    rw = jnp.where(valid, rw_ref[0], 0.0)
    prev_row = jnp.where((i > 0) & (i * tm - 1 < seq_len), pv_ref[0, SUBLANES - 1:SUBLANES, :], 0.0)
    next_row = jnp.where((i < n_t - 1) & ((i + 1) * tm < seq_len), nx_ref[0, 0:1, :], 0.0)
    prev = jnp.where(row == 0, prev_row, pltpu.roll(rw, 1, 0))
    nxt = jnp.where(row == tm - 1, next_row, pltpu.roll(rw, tm - 1, 0))
    u = rw + mup_ref[...] * (prev - rw) + mun_ref[...] * (nxt - rw)
    r, k, v = u[:, :RW], u[:, RW:2 * RW], u[:, 2 * RW:3 * RW]
    wd = u[:, 3 * RW:3 * RW + LANES]
    ad = u[:, 3 * RW + LANES:3 * RW + 2 * LANES]
    gd = u[:, 3 * RW + 2 * LANES:]
    ones = ones_ref[...]
    if first:
        vfirst_o[0] = v
    else:
        low = jnp.dot(v.astype(BF16), v1_ref[...], preferred_element_type=F32)
        mix = jax.nn.sigmoid(v0_ref[...] + jnp.dot(low.astype(BF16), v2_ref[...], preferred_element_type=F32))
        v = v + (vf_ref[0] - v) * mix
    g = jnp.dot(jax.nn.sigmoid(gd).astype(BF16), g2_ref[...], preferred_element_type=F32)
    kk = k * kk_ref[...]
    kk = jnp.where(valid, kk * lax.rsqrt(_seg_sum(kk * kk, ones) + 1e-12), 0.0)
    wl = w0_ref[...] + jnp.dot(jnp.tanh(wd).astype(BF16), w2_ref[...], preferred_element_type=F32)
    logw = -math.exp(-0.5) * jax.nn.sigmoid(wl)
    decay = jnp.exp(logw)
    a = jax.nn.sigmoid(a0_ref[...] + jnp.dot(ad.astype(BF16), a2_ref[...], preferred_element_type=F32))
    ka = ka_ref[...]
    kd_f = jnp.where(valid, k * (1.0 + (a[:, :RW] - 1.0) * ka), 0.0)
    kd_b = jnp.where(valid, k * (1.0 + (a[:, RW:] - 1.0) * ka), 0.0)
    bonus = _seg_sum(r * (kd_f + kd_b) * rk_ref[...], ones)
    b_f, b_b = kk * a[:, :RW], kk * a[:, RW:]
    cum_f = _cumsum3(tri_ref[0], logw[:, :RW])
    cum_b = _cumsum3(tri_ref[1], logw[:, RW:])
    p_f, ip_f = jnp.exp(cum_f), jnp.exp(-cum_f)
    p_b, ip_b = jnp.exp(cum_b), jnp.exp(-cum_b)
    kk_o[0] = kk
    v_o[0] = jnp.where(valid, v, 0.0)
    wf_o[0] = decay[:, :RW]
    wb_o[0] = decay[:, RW:]
    kdf_o[0] = kd_f
    kdb_o[0] = kd_b
    bf_o[0] = b_f
    bb_o[0] = b_b
    rbf_o[0] = r * p_f
    rbb_o[0] = r * p_b
    ktf_o[0] = kd_f * ip_f
    ktb_o[0] = kd_b * ip_b
    btf_o[0] = b_f * ip_f
    btb_o[0] = b_b * ip_b
    g_o[0] = g
    bvg_o[0] = bonus * v * g


def _cumsum3(tri, x):
    h1 = x.astype(BF16)
    r1 = x - h1.astype(F32)
    h2 = r1.astype(BF16)
    h3 = (r1 - h2.astype(F32)).astype(BF16)
    return (jnp.dot(tri, h1, preferred_element_type=F32) + jnp.dot(tri, h2, preferred_element_type=F32)
            + jnp.dot(tri, h3, preferred_element_type=F32))


def _chunk_tri(tm):
    t = np.arange(tm)
    same = (t[:, None] // SCAN_CHUNK) == (t[None, :] // SCAN_CHUNK)
    return jnp.asarray(np.stack([same & (t[None, :] <= t[:, None]), same & (t[None, :] >= t[:, None])]), BF16)


def _block_diag2(a, b):
    z = jnp.zeros_like(a)
    return jnp.concatenate([jnp.concatenate([a, z], 1), jnp.concatenate([z, b], 1)], 0)


def _rwkv_prep(rw, seq_len, mu_prev, mu_next, w0, w2, a0, a2, g2, k_k, k_a, r_k, v_first, vres):
    B, Lp, n_rw = rw.shape
    tm = _row_tile(Lp)
    tpb = tm // SUBLANES
    first = vres is None
    vec = lambda a: a.reshape(1, -1)
    consts = [vec(mu_prev), vec(mu_next), _block_diag2(w2[0], w2[1]).astype(BF16), vec(w0),
              _block_diag2(a2[0], a2[1]).astype(BF16), vec(a0), g2.astype(BF16), vec(k_k), vec(k_a), vec(r_k),
              _seg_ones(RW), _chunk_tri(tm)]
    tile = lambda w: pl.BlockSpec((1, tm, w), lambda b, i: (b, i, 0))
    full = lambda a: pl.BlockSpec(a.shape, lambda b, i: (0,) * a.ndim)
    in_specs = [tile(n_rw),
                pl.BlockSpec((1, SUBLANES, n_rw), lambda b, i: (b, jnp.maximum(i * tpb - 1, 0), 0)),
                pl.BlockSpec((1, SUBLANES, n_rw), lambda b, i: (b, jnp.minimum((i + 1) * tpb, Lp // SUBLANES - 1), 0))]
    in_specs += [full(c) for c in consts]
    args = [rw, rw, rw] + consts
    n_out = 16
    if first:
        n_out += 1
    else:
        v0, v1, v2 = vres
        v1p = jnp.concatenate([v1, jnp.zeros((RW, LANES - VRES_LORA), F32)], 1).astype(BF16)
        v2p = jnp.concatenate([v2, jnp.zeros((LANES - VRES_LORA, RW), F32)], 0).astype(BF16)
        extra = [vec(v0), v1p, v2p]
        in_specs += [tile(RW)] + [full(c) for c in extra]
        args += [v_first] + extra
    out = pl.pallas_call(
        functools.partial(_prep_kernel, first=first, seq_len=seq_len, tm=tm),
        out_shape=tuple(jax.ShapeDtypeStruct((B, Lp, RW), F32) for _ in range(n_out)),
        grid=(B, Lp // tm),
        in_specs=in_specs,
        out_specs=tuple(tile(RW) for _ in range(n_out)),
        compiler_params=_cparams(("parallel", "parallel")),
        name="rwkv_prep",
    )(*args)
    return out


def _scan_kernel(kkf, wf, kdf, bf, vf, rbf, ktf, btf, kkb, wb, kdb, bb_, vb_, rbb, ktb, btb, ones_ref,
                 yf_ref, yb_ref, s_ref, s0_ref, vt_ref, u_ref, *, nb):
    j = pl.program_id(1)
    C = nb * 4
    half_t = TIME_BLOCK // 2
    assert half_t == SCAN_CHUNK

    @pl.when(j == 0)
    def _():
        s_ref[...] = jnp.zeros_like(s_ref)

    lane1 = lax.broadcasted_iota(jnp.int32, (N_B, LANES), 1)
    lo_half = lane1 < N_B

    for d, vref in enumerate((vf, vb_)):
        for n in range(nb):
            for hp in range(4):
                c = n * 4 + hp
                xt = vref[n, :, hp * LANES:(hp + 1) * LANES].T
                top, bot = xt[:N_B], xt[N_B:]
                vt_ref[d, 0, c * N_B:(c + 1) * N_B, :] = jnp.where(lo_half, top, pltpu.roll(bot, N_B, 1))
                vt_ref[d, 1, c * N_B:(c + 1) * N_B, :] = jnp.where(lo_half, pltpu.roll(top, N_B, 1), bot)

    ones = ones_ref[...]
    lane_id = lax.broadcasted_iota(jnp.int32, (C * N_B, LANES), 1)
    lane = lane_id % N_B
    head_base = (lane_id // N_B) * N_B
    row_refs = ((kkf, wf, kdf, bf), (kkb, wb, kdb, bb_))
    chunk_refs = ((vf, rbf, ktf, btf, yf_ref), (vb_, rbb, ktb, btb, yb_ref))
    tr = lax.broadcasted_iota(jnp.int32, (N_B, LANES), 0)
    tc = lane1 % N_B
    earlier2 = (tc <= tr, tc >= tr)
    lanes_dims = ((1,), (1,))

    def step(d, tiles, row, oh, tt):
        def rows(tile):
            return jnp.concatenate(
                [jnp.broadcast_to(tile[n, row:row + 1, hp * LANES:(hp + 1) * LANES], (N_B, LANES))
                 for n in range(nb) for hp in range(4)], axis=0)

        kk, w, kd, b = [rows(x) for x in tiles]
        S = s_ref[d]
        sa = jnp.dot((S * kk).astype(BF16), ones[:LANES], preferred_element_type=F32)
        vb = jnp.take_along_axis(vt_ref[d, oh], head_base + tt, axis=1, mode="promise_in_bounds")
        s_ref[d] = S * w - sa * b + vb * kd
        u_ref[d] = jnp.where(lane == tt, sa, u_ref[d])

    def make_body(half):
        def body(g, carry):
            t0f = pl.multiple_of(half * half_t + g * SUBLANES, SUBLANES)
            t0b = pl.multiple_of(TIME_BLOCK - SUBLANES - (half * half_t + g * SUBLANES), SUBLANES)
            tiles_f = [x[:, pl.ds(t0f, SUBLANES), :] for x in row_refs[0]]
            tiles_b = [x[:, pl.ds(t0b, SUBLANES), :] for x in row_refs[1]]
            for i in range(SUBLANES):
                step(0, tiles_f, i, half, g * SUBLANES + i)
                step(1, tiles_b, SUBLANES - 1 - i, 1 - half, half_t - 1 - (g * SUBLANES + i))
            return carry
        return body

    def chunk_outputs(d, oh):
        vref, rref, ktref, btref, yref = chunk_refs[d]
        r0 = oh * half_t
        for n in range(nb):
            for hp in range(4):
                c = n * 4 + hp
                blk = lambda ref: ref[n, r0:r0 + half_t, hp * LANES:(hp + 1) * LANES]
                rb, kt, bt, v = blk(rref), blk(ktref), blk(btref), blk(vref)
                s0 = s0_ref[d, c * N_B:(c + 1) * N_B, :]
                ut = u_ref[d, c * N_B:(c + 1) * N_B, :]
                u2 = jnp.concatenate([ut, ut], axis=0).T
                r2 = jnp.concatenate([jnp.where(lo_half, rb, 0.0), jnp.where(lo_half, 0.0, rb)], axis=0)
                r2h, r2l = _split_bf16(r2)
                rhs = jnp.concatenate([kt, bt, s0, s0], axis=0).astype(BF16)
                gb = lax.dot_general(jnp.concatenate([r2h, r2l], axis=1), jnp.concatenate([rhs, rhs], axis=1),
                                     (lanes_dims, ((), ())), preferred_element_type=F32)
                vh = v.astype(BF16)
                per_head = []
                for h in range(2):
                    gh = gb[h * N_B:(h + 1) * N_B]
                    gh_, gl_ = _split_bf16(jnp.where(earlier2[d], gh[:, :LANES], 0.0))
                    w = jnp.concatenate([vh, (-u2[h * N_B:(h + 1) * N_B]).astype(BF16)], axis=0)
                    corr = jnp.dot(jnp.concatenate([gh_, gl_], axis=1), jnp.concatenate([w, w], axis=0),
                                   preferred_element_type=F32)
                    per_head.append(gh[:, LANES:] + corr)
                yref[n, r0:r0 + half_t, hp * LANES:(hp + 1) * LANES] = jnp.where(lo_half, per_head[0], per_head[1])

    for half in range(2):
        s0_ref[...] = s_ref[...]
        u_ref[...] = jnp.zeros_like(u_ref)
        lax.fori_loop(0, half_t // SUBLANES, make_body(half), 0)
        chunk_outputs(0, half)
        chunk_outputs(1, 1 - half)


def _wkv_bidir(kk, v, per_dir, nb):
    B, Lp, _ = kk.shape
    nblk = Lp // TIME_BLOCK
    ones = _seg_ones(LANES)
    fwd = lambda bi, j: (bi, j, 0)
    bwd = lambda bi, j: (bi, nblk - 1 - j, 0)
    blk = (nb, TIME_BLOCK, RW)
    (wf, kdf, bf, rbf, ktf, btf), (wb, kdb, bb_, rbb, ktb, btb) = per_dir
    y_shape = jax.ShapeDtypeStruct((B, Lp, RW), F32)
    rows = nb * 4 * N_B
    return pl.pallas_call(
        functools.partial(_scan_kernel, nb=nb),
        out_shape=(y_shape, y_shape),
        grid=(B // nb, nblk),
        in_specs=[pl.BlockSpec(blk, fwd)] * 8 + [pl.BlockSpec(blk, bwd)] * 8
                 + [pl.BlockSpec(ones.shape, lambda bi, j: (0, 0))],
        out_specs=(pl.BlockSpec(blk, fwd), pl.BlockSpec(blk, bwd)),
        scratch_shapes=[pltpu.VMEM((2, rows, LANES), F32),
                        pltpu.VMEM((2, rows, LANES), F32),
                        pltpu.VMEM((2, 2, rows, LANES), F32),
                        pltpu.VMEM((2, rows, LANES), F32)],
        compiler_params=_cparams(("parallel", "arbitrary")),
        name="wkv_scan",
    )(kk, wf, kdf, bf, v, rbf, ktf, btf, kk, wb, kdb, bb_, v, rbb, ktb, btb, ones)


def _post_kernel(yf_ref, yb_ref, g_ref, bvg_ref, att_ref, h_ref, gng_ref, gnb_ref, wo_ref, l1g_ref, l1b_ref,
                 wrh_ref, wrl_ref, rb_ref, tri_ref, ones_ref,
                 h1_ref, rf_ref, ri_ref, cnt_ref, *, alpha, tm):
    ones = ones_ref[...]
    y = yf_ref[...] + yb_ref[...]
    mu = _seg_sum(y, ones) * (1.0 / N_B)
    yc = y - mu
    var = _seg_sum(yc * yc, ones) * (1.0 / N_B)
    yr = (yc * lax.rsqrt(var + RWKV_GN_EPS) * gng_ref[...] + gnb_ref[...]) * g_ref[...] + bvg_ref[...]
    half = H_A * V_DIM
    mixed = (jnp.dot(att_ref[...], wo_ref[:half, :], preferred_element_type=F32)
             + jnp.dot(yr.astype(BF16), wo_ref[half:, :], preferred_element_type=F32))
    h1 = _ln(h_ref[...] * alpha + mixed, l1g_ref[...], l1b_ref[...])
    h1_ref[...] = h1

    xh, xl = _split_bf16(h1)
    x = (jnp.dot(xh, wrh_ref[...], preferred_element_type=F32) + jnp.dot(xl, wrh_ref[...], preferred_element_type=F32)
         + jnp.dot(xh, wrl_ref[...], preferred_element_type=F32)) + rb_ref[...]
    lane = lax.broadcasted_iota(jnp.int32, x.shape, 1)
    lanef = lane.astype(F32)
    big = float(LANES)
    gmask = lane < N_GROUPS
    gx = jnp.where(gmask, x, NEG)
    gmax = jnp.max(gx, -1, keepdims=True)
    gidx = jnp.min(jnp.where(gx == gmax, lanef, big), -1, keepdims=True)
    gsum = jnp.sum(jnp.where(gmask, jnp.exp(gx - gmax), 0.0), -1, keepdims=True)
    lo = N_GROUPS + EXPERTS_PER_GROUP * gidx
    emask = (lanef >= lo) & (lanef < lo + EXPERTS_PER_GROUP)
    ex = jnp.where(emask, x, NEG)
    m1 = jnp.max(ex, -1, keepdims=True)
    i1 = jnp.min(jnp.where(emask & (ex == m1), lanef, big), -1, keepdims=True)
    ex2 = jnp.where(lanef == i1, NEG, ex)
    m2 = jnp.max(ex2, -1, keepdims=True)
    i2 = jnp.min(jnp.where(emask & (lanef != i1) & (ex2 == m2), lanef, big), -1, keepdims=True)
    esum = jnp.sum(jnp.where(emask, jnp.exp(ex - m1), 0.0), -1, keepdims=True)
    gp = 1.0 / gsum
    gate0 = gp * (1.0 / esum)
    gate1 = gp * (jnp.exp(m2 - m1) / esum)
    e0 = i1 - N_GROUPS
    e1 = i2 - N_GROUPS

    @pl.when(pl.program_id(0) == 0)
    def _():
        cnt_ref[...] = jnp.zeros_like(cnt_ref)

    onehot = jnp.where((lanef == e0) | (lanef == e1), 1.0, 0.0)
    prefix = jnp.dot(tri_ref[...], onehot.astype(BF16), preferred_element_type=F32) + cnt_ref[...]
    r0 = jnp.sum(jnp.where(lanef == e0, prefix, 0.0), -1, keepdims=True)
    r1 = jnp.sum(jnp.where(lanef == e1, prefix, 0.0), -1, keepdims=True)
    cnt_ref[...] += jnp.sum(onehot, 0, keepdims=True)
    rf_ref[...] = jnp.where(lane == 0, gate0, jnp.where(lane == 1, gate1, 0.0))
    ri = jnp.where(lane == 0, e0, jnp.where(lane == 1, e1, jnp.where(lane == 2, r0, jnp.where(lane == 3, r1, 0.0))))
    ri_ref[...] = ri.astype(jnp.int32)


def _post_mixer(yf, yb, g, bvg, att, h, gn_g, gn_b, w_out, ln_g, ln_b, w_group, b_group, w_expert, b_expert, alpha):
    T, D = h.shape
    tm = _row_tile(T)
    vec = lambda a: a.reshape(1, -1)
    zpad = LANES - N_GROUPS - N_EXPERTS
    wr = jnp.concatenate([w_group, w_expert, jnp.zeros((D, zpad), F32)], axis=1)
    wrh = wr.astype(BF16)
    wrl = (wr - wrh.astype(F32)).astype(BF16)
    rb = jnp.concatenate([b_group, b_expert, jnp.zeros((zpad,), F32)]).reshape(1, LANES)
    tri = jnp.asarray(np.arange(tm)[:, None] > np.arange(tm)[None, :], BF16)
    consts = [vec(gn_g), vec(gn_b), w_out.astype(BF16), vec(ln_g), vec(ln_b), wrh, wrl, rb, tri, _seg_ones(RW)]
    row = lambda w: pl.BlockSpec((tm, w), lambda i: (i, 0))
    full = lambda a: pl.BlockSpec(a.shape, lambda i: (0, 0))
    return pl.pallas_call(
        functools.partial(_post_kernel, alpha=alpha, tm=tm),
        out_shape=(jax.ShapeDtypeStruct((T, D), F32),
                   jax.ShapeDtypeStruct((T, LANES), F32),
                   jax.ShapeDtypeStruct((T, LANES), jnp.int32),
                   jax.ShapeDtypeStruct((1, LANES), F32)),
        grid=(T // tm,),
        in_specs=[row(RW)] * 4 + [row(H_A * V_DIM), row(D)] + [full(c) for c in consts],
        out_specs=(row(D), row(LANES), row(LANES), pl.BlockSpec((1, LANES), lambda i: (0, 0))),
        compiler_params=_cparams(("arbitrary",)),
        name="post_mixer",
    )(yf, yb, g, bvg, att, h, *consts)


def _row_copies(dest_smem, t, make):
    return [make(s, dest_smem[TOP_K * t + s]) for s in range(TOP_K)]


def _load_dest(dest_hbm, dest_smem, isem, tm):
    load = pltpu.make_async_copy(dest_hbm.at[pl.ds(pl.program_id(0) * tm * TOP_K, tm * TOP_K)], dest_smem, isem)
    load.start()
    load.wait()


def _dispatch_kernel(dest_hbm, h_ref, xz_hbm, xb_hbm, dest_smem, isem, sem, *, tm):
    del xz_hbm
    _load_dest(dest_hbm, dest_smem, isem, tm)

    def copies(t):
        src = h_ref.at[pl.ds(t, 1)]
        return _row_copies(dest_smem, t, lambda s, dst: pltpu.make_async_copy(src, xb_hbm.at[pl.ds(dst, 1)], sem))

    def start(t, c):
        for cp in copies(t):
            cp.start()
        return c

    def wait(t, c):
        for cp in copies(t):
            cp.wait()
        return c

    lax.fori_loop(0, tm, start, 0, unroll=8)
    lax.fori_loop(0, tm, wait, 0, unroll=8)


def _dispatch(dest, h, n_rows):
    T, D = h.shape
    tm = _row_tile(T)
    any_spec = pl.BlockSpec(memory_space=pl.ANY)
    return pl.pallas_call(
        functools.partial(_dispatch_kernel, tm=tm),
        out_shape=jax.ShapeDtypeStruct((n_rows, D), F32),
        grid=(T // tm,),
        in_specs=[any_spec, pl.BlockSpec((tm, D), lambda i: (i, 0)), any_spec],
        out_specs=any_spec,
        scratch_shapes=[pltpu.SMEM((tm * TOP_K,), jnp.int32), pltpu.SemaphoreType.DMA, pltpu.SemaphoreType.DMA],
        input_output_aliases={2: 0},
        compiler_params=_cparams(("arbitrary",)),
        name="moe_dispatch",
    )(dest, h, jnp.zeros((n_rows, D), F32))


def _expert_kernel(be_ref, nu_ref, x_ref, w1_ref, w3_ref, w2_ref, o_ref):
    @pl.when(pl.program_id(0) < nu_ref[0])
    def _():
        x = x_ref[...].astype(BF16)
        a = jnp.dot(x, w1_ref[0], preferred_element_type=F32)
        b = jnp.dot(x, w3_ref[0], preferred_element_type=F32)
        hid = (a * jax.nn.sigmoid(a)) * b
        o_ref[...] = jnp.dot(hid.astype(BF16), w2_ref[0], preferred_element_type=F32)


def _expert_ffn(xb, block_e, n_used, w1, w3, w2):
    n_rows, D = xb.shape
    n_blocks = n_rows // MOE_BLOCK
    blk = lambda i, be, nu: jnp.minimum(i, nu[0] - 1)
    wmap = lambda i, be, nu: (be[blk(i, be, nu)], 0, 0)
    grid_spec = pltpu.PrefetchScalarGridSpec(
        num_scalar_prefetch=2,
        grid=(n_blocks,),
        in_specs=[pl.BlockSpec((MOE_BLOCK, D), lambda i, be, nu: (blk(i, be, nu), 0)),
                  pl.BlockSpec((1, D, E_HID), wmap),
                  pl.BlockSpec((1, D, E_HID), wmap),
                  pl.BlockSpec((1, E_HID, D), wmap)],
        out_specs=pl.BlockSpec((MOE_BLOCK, D), lambda i, be, nu: (blk(i, be, nu), 0)),
    )
    return pl.pallas_call(
        _expert_kernel,
        out_shape=jax.ShapeDtypeStruct((n_rows, D), F32),
        grid_spec=grid_spec,
        compiler_params=_cparams(("arbitrary",)),
        name="moe_experts",
    )(block_e, n_used, xb, w1, w3, w2)


def _combine_kernel(dest_hbm, yb_hbm, gate_ref, h_ref, g_ref, b_ref, o_ref, dest_smem, buf, isem, sem, *, tm, alpha):
    _load_dest(dest_hbm, dest_smem, isem, tm)

    def copies(t):
        return _row_copies(dest_smem, t, lambda s, src: pltpu.make_async_copy(
            yb_hbm.at[pl.ds(src, 1)], buf.at[s, pl.ds(t, 1)], sem))

    def start(t, c):
        for cp in copies(t):
            cp.start()
        return c

    def wait(t, c):
        for cp in copies(t):
            cp.wait()
        return c

    lax.fori_loop(0, tm, start, 0, unroll=8)
    lax.fori_loop(0, tm, wait, 0, unroll=8)
    gate = gate_ref[...]
    ff = sum(buf[s] * gate[:, s:s + 1] for s in range(TOP_K))
    o_ref[...] = _ln(h_ref[...] * alpha + ff, g_ref[...], b_ref[...])


def _combine(dest, yb, route_f, h, ln_g, ln_b, alpha):
    T, D = h.shape
    tm = _row_tile(T)
    any_spec = pl.BlockSpec(memory_space=pl.ANY)
    row = lambda w: pl.BlockSpec((tm, w), lambda i: (i, 0))
    vec = pl.BlockSpec((1, D), lambda i: (0, 0))
    return pl.pallas_call(
        functools.partial(_combine_kernel, tm=tm, alpha=alpha),
        out_shape=jax.ShapeDtypeStruct((T, D), F32),
        grid=(T // tm,),
        in_specs=[any_spec, any_spec, row(LANES), row(D), vec, vec],
        out_specs=row(D),
        scratch_shapes=[pltpu.SMEM((tm * TOP_K,), jnp.int32), pltpu.VMEM((TOP_K, tm, D), F32),
                        pltpu.SemaphoreType.DMA, pltpu.SemaphoreType.DMA],
        compiler_params=_cparams(("arbitrary",)),
        name="moe_combine",
    )(dest, yb, route_f, h, ln_g.reshape(1, D), ln_b.reshape(1, D))


def _hier_moe(h, route_f, route_i, counts, w1, w3, w2, ln_g, ln_b, alpha):
    T, D = h.shape
    counts = counts[0, :N_EXPERTS].astype(jnp.int32)
    padded = (counts + MOE_BLOCK - 1) // MOE_BLOCK * MOE_BLOCK
    pends = jnp.cumsum(padded)
    pstart = pends - padded
    n_rows = (T * TOP_K + N_EXPERTS * (MOE_BLOCK - 1) + MOE_BLOCK - 1) // MOE_BLOCK * MOE_BLOCK
    n_blocks = n_rows // MOE_BLOCK
    block_start = jnp.arange(n_blocks, dtype=jnp.int32) * MOE_BLOCK
    block_e = jnp.minimum(jnp.sum(pends[None, :] <= block_start[:, None], axis=1), N_EXPERTS - 1).astype(jnp.int32)
    n_used = (pends[-1:] // MOE_BLOCK).astype(jnp.int32)
    dest = (jnp.take(pstart, route_i[:, :TOP_K]) + route_i[:, TOP_K:2 * TOP_K]).reshape(-1)
    xb = _dispatch(dest, h, n_rows)
    yb = _expert_ffn(xb, block_e, n_used, w1, w3, w2)
    return _combine(dest, yb, route_f, h, ln_g, ln_b, alpha)


def kernel(x, positions, meta_tokens, emb_ln_g, emb_ln_b, w_in, mla_q_norm, mla_kv_norm, mla_w_uq, mla_w_ukv, mla_out_norm, rwkv_mu_prev, rwkv_mu_next, rwkv_w0, rwkv_w2, rwkv_a0, rwkv_a2, rwkv_g2, rwkv_k_k, rwkv_k_a, rwkv_r_k, rwkv_gn_g, rwkv_gn_b, rwkv_v0, rwkv_v1, rwkv_v2, w_out, ln1_g, ln1_b, moe_w_group, moe_b_group, moe_w_expert, moe_b_expert, moe_w1, moe_w3, moe_w2, ln2_g, ln2_b):
    B, seq, D = x.shape
    depth = w_in.shape[0]
    alpha = (2 * depth) ** 0.25
    L = seq + N_META
    Lp = -(-L // LANES) * LANES
    T = B * Lp
    nb = next(n for n in (4, 2, 1) if B % n == 0)
    tk = 384 if Lp % 384 == 0 else LANES
    tq = next(t for t in (1408, 1152, 768, 384, LANES) if Lp % t == 0)

    meta = jnp.broadcast_to(meta_tokens.astype(x.dtype)[None], (B, N_META, D))
    h = jnp.concatenate([meta, x, jnp.zeros((B, Lp - L, D), x.dtype)], axis=1).reshape(T, D)
    h = _ln_residual(h, jnp.zeros_like(h), emb_ln_g, emb_ln_b, 1.0)

    key_bias = jnp.where(jnp.arange(Lp) < L, 0.0, NEG).astype(F32)[None, :]
    pos = jnp.concatenate([jnp.broadcast_to(jnp.arange(N_META, dtype=jnp.int32), (B, N_META)),
                           positions + N_META, jnp.zeros((B, Lp - L), jnp.int32)], axis=1)
    inv_freq = ROPE_THETA ** (-jnp.arange(0, QK_ROPE, 2, dtype=F32) / QK_ROPE)
    ang = pos.astype(F32)[..., None] * inv_freq
    cos, sin = jnp.cos(ang).reshape(T, -1), jnp.sin(ang).reshape(T, -1)
    zeros = jnp.zeros((T, LANES - QK_ROPE), F32)
    rope_cos = jnp.concatenate([cos, cos, zeros], axis=1)
    rope_sin = jnp.concatenate([-sin, sin, zeros], axis=1)
    qscale = (QK_NOPE + QK_ROPE) ** -0.5 * math.log2(math.e)

    v_first = None
    for li in range(depth):
        q, k, v, rw = _project(h, w_in[li], mla_q_norm[li], mla_kv_norm[li], mla_w_uq[li], mla_w_ukv[li],
                               rope_cos, rope_sin, qscale)
        y_att = _attention(q.reshape(B, Lp, -1), k.reshape(B, Lp, -1), v.reshape(B, Lp, -1),
                           key_bias, mla_out_norm[li], tq, tk)
        vres = None if li == 0 else (rwkv_v0[li - 1], rwkv_v1[li - 1], rwkv_v2[li - 1])
        outs = _rwkv_prep(rw.reshape(B, Lp, -1), L, rwkv_mu_prev[li], rwkv_mu_next[li], rwkv_w0[li], rwkv_w2[li],
                          rwkv_a0[li], rwkv_a2[li], rwkv_g2[li], rwkv_k_k[li], rwkv_k_a[li], rwkv_r_k[li],
                          v_first, vres)
        kk, vm, g, bvg = outs[0], outs[1], outs[14], outs[15]
        if li == 0:
            v_first = outs[16]
        y_f, y_b = _wkv_bidir(kk, vm, (outs[2:8], outs[8:14]), nb)
        flat = lambda a: a.reshape(T, -1)
        h, route_f, route_i, counts = _post_mixer(
            flat(y_f), flat(y_b), flat(g), flat(bvg), flat(y_att), h, rwkv_gn_g[li], rwkv_gn_b[li], w_out[li],
            ln1_g[li], ln1_b[li], moe_w_group[li], moe_b_group[li], moe_w_expert[li], moe_b_expert[li], alpha)
        h = _hier_moe(h, route_f, route_i, counts,
                      moe_w1[li].astype(BF16), moe_w3[li].astype(BF16), moe_w2[li].astype(BF16),
                      ln2_g[li], ln2_b[li], alpha)
    return h.reshape(B, Lp, D)[:, N_META:L]
```

```python
import functools
import math

import numpy as np
import jax
import jax.numpy as jnp
from jax import lax
from jax.experimental import pallas as pl
from jax.experimental.pallas import tpu as pltpu

F32 = jnp.float32
BF16 = jnp.bfloat16

N_META = 16
H_A = 4
QK_NOPE = 128
QK_ROPE = 64
V_DIM = 128
Q_LORA = 256
KV_LORA = 128
ROPE_THETA = 10000.0
H_B = 8
N_B = 64
RW = H_B * N_B
DECAY_LORA = 64
AAA_LORA = 64
GATE_LORA = 128
VRES_LORA = 32
RWKV_GN_EPS = 64e-5
MLA_COLS = Q_LORA + KV_LORA + QK_ROPE
N_GROUPS = 4
EXPERTS_PER_GROUP = 8
N_EXPERTS = N_GROUPS * EXPERTS_PER_GROUP
TOP_K = 2
E_HID = 256
MOE_BLOCK = 256
LN_EPS = 1e-5
RMS_EPS = 1e-6

LANES = 128
SUBLANES = 8
TIME_BLOCK = 128
SCAN_CHUNK = 64
ROW_TILE = 512
VMEM_LIMIT = 56 * 1024 * 1024
HEAD_W = 2 * LANES
NEG = -1e30


def _cparams(sem):
    return pltpu.CompilerParams(dimension_semantics=sem, vmem_limit_bytes=VMEM_LIMIT)


def _row_tile(m):
    return next(t for t in (ROW_TILE, 384, 256, LANES) if m % t == 0)


def _split_bf16(x):
    hi = x.astype(BF16)
    lo = (x - hi.astype(F32)).astype(BF16)
    return hi, lo


def _seg_sum(x, ones):
    hi, lo = _split_bf16(x)
    return jnp.dot(jnp.concatenate([hi, lo], axis=-1), ones, preferred_element_type=F32)


def _seg_ones(width):
    m = np.arange(2 * width)[:, None] % width
    n = np.arange(width)[None, :]
    return jnp.asarray(m // N_B == n // N_B, BF16)


def _ln(x, g, b):
    mu = jnp.mean(x, -1, keepdims=True)
    xc = x - mu
    var = jnp.mean(xc * xc, -1, keepdims=True)
    return xc * lax.rsqrt(var + LN_EPS) * g + b


def _ln_kernel(x_ref, r_ref, g_ref, b_ref, o_ref, *, alpha):
    o_ref[...] = _ln(x_ref[...] * alpha + r_ref[...], g_ref[...], b_ref[...])


def _ln_residual(x, r, g, b, alpha):
    M, D = x.shape
    tm = _row_tile(M)
    row = pl.BlockSpec((tm, D), lambda i: (i, 0))
    vec = pl.BlockSpec((1, D), lambda i: (0, 0))
    return pl.pallas_call(
        functools.partial(_ln_kernel, alpha=alpha),
        out_shape=jax.ShapeDtypeStruct((M, D), F32),
        grid=(M // tm,),
        in_specs=[row, row, vec, vec],
        out_specs=row,
        compiler_params=_cparams(("parallel",)),
        name="layer_norm",
    )(x, r, g.reshape(1, D), b.reshape(1, D))


def _rms(x, g):
    return x * lax.rsqrt(jnp.mean(x * x, -1, keepdims=True) + RMS_EPS) * g


def _proj_kernel(h_ref, win_ref, qn_ref, kvn_ref, wuq_ref, wukv_ref, c_ref, s_ref,
                 q_ref, k_ref, v_ref, rw_ref, *, qscale):
    proj = jnp.dot(h_ref[...].astype(BF16), win_ref[...], preferred_element_type=F32)
    rw_ref[...] = proj[:, MLA_COLS + N_B:]
    q = jnp.dot(_rms(proj[:, :Q_LORA], qn_ref[...]).astype(BF16), wuq_ref[...], preferred_element_type=F32)
    kv = jnp.dot(_rms(proj[:, Q_LORA:Q_LORA + KV_LORA], kvn_ref[...]).astype(BF16), wukv_ref[...],
                 preferred_element_type=F32)
    cos, sin = c_ref[...], s_ref[...]
    lane = lax.broadcasted_iota(jnp.int32, cos.shape, 1)
    half = QK_ROPE // 2

    def rope(x):
        partner = jnp.where(lane < half, pltpu.roll(x, LANES - half, 1), pltpu.roll(x, half, 1))
        return x * cos + partner * sin

    k_pe = rope(proj[:, Q_LORA + KV_LORA:Q_LORA + KV_LORA + LANES]).astype(BF16)
    for h in range(H_A):
        o = h * HEAD_W
        q_ref[:, o:o + LANES] = (q[:, o:o + LANES] * qscale).astype(BF16)
        q_ref[:, o + LANES:o + HEAD_W] = (rope(q[:, o + LANES:o + HEAD_W]) * qscale).astype(BF16)
        k_ref[:, o:o + LANES] = kv[:, o:o + LANES].astype(BF16)
        k_ref[:, o + LANES:o + HEAD_W] = k_pe
        v_ref[:, h * V_DIM:(h + 1) * V_DIM] = kv[:, o + LANES:o + HEAD_W].astype(BF16)


def _project(h, w_in, q_norm, kv_norm, w_uq, w_ukv, rope_cos, rope_sin, qscale):
    T, D = h.shape
    tm = _row_tile(T)
    n_rw = w_in.shape[1] - MLA_COLS
    pad = jnp.zeros((D, N_B), F32)
    win = jnp.concatenate([w_in[:, :MLA_COLS], pad, w_in[:, MLA_COLS:]], axis=1).astype(BF16)
    wq = w_uq.reshape(Q_LORA, H_A, QK_NOPE + QK_ROPE)
    wq = jnp.concatenate([wq, jnp.zeros((Q_LORA, H_A, HEAD_W - QK_NOPE - QK_ROPE), F32)], axis=-1)
    wq = wq.reshape(Q_LORA, H_A * HEAD_W).astype(BF16)
    row = lambda w: pl.BlockSpec((tm, w), lambda i: (i, 0))
    full = lambda a: pl.BlockSpec(a.shape, lambda i: (0, 0))
    args = (h, win, q_norm.reshape(1, -1), kv_norm.reshape(1, -1), wq, w_ukv.astype(BF16), rope_cos, rope_sin)
    return pl.pallas_call(
        functools.partial(_proj_kernel, qscale=qscale),
        out_shape=(jax.ShapeDtypeStruct((T, H_A * HEAD_W), BF16),
                   jax.ShapeDtypeStruct((T, H_A * HEAD_W), BF16),
                   jax.ShapeDtypeStruct((T, H_A * V_DIM), BF16),
                   jax.ShapeDtypeStruct((T, n_rw), F32)),
        grid=(T // tm,),
        in_specs=[row(D)] + [full(a) for a in args[1:6]] + [row(LANES), row(LANES)],
        out_specs=(row(H_A * HEAD_W), row(H_A * HEAD_W), row(H_A * V_DIM), row(n_rw)),
        compiler_params=_cparams(("parallel",)),
        name="in_proj",
    )(*args)


def _attn_kernel(q_ref, k_ref, v_ref, bias_ref, g_ref, o_ref, m_ref, acc_ref, s_ref, p_ref, a_ref, *, tk):
    q = q_ref[0]
    n = k_ref.shape[1] // tk
    one_col = (lax.broadcasted_iota(jnp.int32, (tk, V_DIM), 1) == 0).astype(BF16)

    def scores(c):
        off = pl.multiple_of(c * tk, tk)
        s = lax.dot_general(q, k_ref[0, pl.ds(off, tk), :], (((1,), (1,)), ((), ())), preferred_element_type=F32)
        return s + bias_ref[:, pl.ds(off, tk)]

    def weighted_values(c):
        off = pl.multiple_of(c * tk, tk)
        v1 = jnp.concatenate([v_ref[0, pl.ds(off, tk), :], one_col], axis=1)
        acc_ref[...] = a_ref[...] * acc_ref[...] + jnp.dot(p_ref[...], v1, preferred_element_type=F32)

    m_ref[...] = jnp.full_like(m_ref, NEG)
    acc_ref[...] = jnp.zeros_like(acc_ref)
    p_ref[...] = jnp.zeros_like(p_ref)
    a_ref[...] = jnp.ones_like(a_ref)
    s_ref[...] = scores(0)

    def body(c, carry):
        weighted_values(jnp.maximum(c - 1, 0))
        s = s_ref[...]
        s_ref[...] = scores(jnp.minimum(c + 1, n - 1))
        m_prev = m_ref[...]
        m_new = jnp.maximum(m_prev, jnp.max(s, -1, keepdims=True))
        p_ref[...] = jnp.exp2(s - m_new).astype(BF16)
        a_ref[...] = jnp.exp2(m_prev - m_new)
        m_ref[...] = m_new
        return carry

    lax.fori_loop(0, n, body, 0)
    weighted_values(n - 1)
    acc = acc_ref[...]
    o = acc[:, :V_DIM] / acc[:, V_DIM:V_DIM + 1]
    o = o * lax.rsqrt(jnp.mean(o * o, -1, keepdims=True) + RMS_EPS) * g_ref[...]
    o_ref[0] = o.astype(BF16)


def _attention(q, k, v, bias, out_gain, tq, tk):
    B, Lp, _ = q.shape
    return pl.pallas_call(
        functools.partial(_attn_kernel, tk=tk),
        out_shape=jax.ShapeDtypeStruct((B, Lp, H_A * V_DIM), BF16),
        grid=(B, H_A, Lp // tq),
        in_specs=[pl.BlockSpec((1, tq, HEAD_W), lambda b, h, i: (b, i, h)),
                  pl.BlockSpec((1, Lp, HEAD_W), lambda b, h, i: (b, 0, h)),
                  pl.BlockSpec((1, Lp, V_DIM), lambda b, h, i: (b, 0, h)),
                  pl.BlockSpec((1, Lp), lambda b, h, i: (0, 0)),
                  pl.BlockSpec((1, V_DIM), lambda b, h, i: (0, h))],
        out_specs=pl.BlockSpec((1, tq, V_DIM), lambda b, h, i: (b, i, h)),
        scratch_shapes=[pltpu.VMEM((tq, 1), F32), pltpu.VMEM((tq, 2 * V_DIM), F32),
                        pltpu.VMEM((tq, tk), F32), pltpu.VMEM((tq, tk), BF16), pltpu.VMEM((tq, 1), F32)],
        compiler_params=_cparams(("parallel", "parallel", "parallel")),
        name="mla_attention",
    )(q, k, v, bias, out_gain.reshape(1, H_A * V_DIM))


def _prep_kernel(*refs, first, seq_len, tm):
    if first:
        (rw_ref, pv_ref, nx_ref, mup_ref, mun_ref, w2_ref, w0_ref, a2_ref, a0_ref, g2_ref, kk_ref, ka_ref, rk_ref,
         ones_ref, tri_ref, kk_o, v_o, wf_o, kdf_o, bf_o, rbf_o, ktf_o, btf_o, wb_o, kdb_o, bb_o, rbb_o, ktb_o, btb_o,
         g_o, bvg_o, vt_o, vfirst_o) = refs
    else:
        (rw_ref, pv_ref, nx_ref, mup_ref, mun_ref, w2_ref, w0_ref, a2_ref, a0_ref, g2_ref, kk_ref, ka_ref, rk_ref,
         ones_ref, tri_ref, vf_ref, v0_ref, v1_ref, v2_ref,
         kk_o, v_o, wf_o, kdf_o, bf_o, rbf_o, ktf_o, btf_o, wb_o, kdb_o, bb_o, rbb_o, ktb_o, btb_o, g_o, bvg_o,
         vt_o) = refs
    i = pl.program_id(1)
    n_t = pl.num_programs(1)
    row = lax.broadcasted_iota(jnp.int32, (tm, 1), 0)
    t = i * tm + row
    valid = t < seq_len
    rw = jnp.where(valid, rw_ref[0], 0.0)
    prev_row = jnp.where((i > 0) & (i * tm - 1 < seq_len), pv_ref[0, SUBLANES - 1:SUBLANES, :], 0.0)
    next_row = jnp.where((i < n_t - 1) & ((i + 1) * tm < seq_len), nx_ref[0, 0:1, :], 0.0)
    prev = jnp.where(row == 0, prev_row, pltpu.roll(rw, 1, 0))
    nxt = jnp.where(row == tm - 1, next_row, pltpu.roll(rw, tm - 1, 0))
    u = rw + mup_ref[...] * (prev - rw) + mun_ref[...] * (nxt - rw)
    r, k, v = u[:, :RW], u[:, RW:2 * RW], u[:, 2 * RW:3 * RW]
    wd = u[:, 3 * RW:3 * RW + LANES]
    ad = u[:, 3 * RW + LANES:3 * RW + 2 * LANES]
    gd = u[:, 3 * RW + 2 * LANES:]
    ones = ones_ref[...]
    if first:
        vfirst_o[0] = v
    else:
        low = jnp.dot(v.astype(BF16), v1_ref[...], preferred_element_type=F32)
        mix = jax.nn.sigmoid(v0_ref[...] + jnp.dot(low.astype(BF16), v2_ref[...], preferred_element_type=F32))
        v = v + (vf_ref[0] - v) * mix
    g = jnp.dot(jax.nn.sigmoid(gd).astype(BF16), g2_ref[...], preferred_element_type=F32)
    kk = k * kk_ref[...]
    kk = jnp.where(valid, kk * lax.rsqrt(_seg_sum(kk * kk, ones) + 1e-12), 0.0)
    wl = w0_ref[...] + jnp.dot(jnp.tanh(wd).astype(BF16), w2_ref[...], preferred_element_type=F32)
    logw = -math.exp(-0.5) * jax.nn.sigmoid(wl)
    decay = jnp.exp(logw)
    a = jax.nn.sigmoid(a0_ref[...] + jnp.dot(ad.astype(BF16), a2_ref[...], preferred_element_type=F32))
    ka = ka_ref[...]
    kd_f = jnp.where(valid, k * (1.0 + (a[:, :RW] - 1.0) * ka), 0.0)
    kd_b = jnp.where(valid, k * (1.0 + (a[:, RW:] - 1.0) * ka), 0.0)
    bonus = _seg_sum(r * (kd_f + kd_b) * rk_ref[...], ones)
    b_f, b_b = kk * a[:, :RW], kk * a[:, RW:]
    cum_f = _cumsum3(tri_ref[0], logw[:, :RW])
    cum_b = _cumsum3(tri_ref[1], logw[:, RW:])
    p_f, ip_f = jnp.exp(cum_f), jnp.exp(-cum_f)
    p_b, ip_b = jnp.exp(cum_b), jnp.exp(-cum_b)
    kk_o[0] = kk
    vm = jnp.where(valid, v, 0.0)
    v_o[0] = vm.astype(BF16)
    lo_half = lax.broadcasted_iota(jnp.int32, (N_B, LANES), 1) < N_B
    for tb in range(tm // TIME_BLOCK):
        for hp in range(H_B // 2):
            xt = vm[tb * TIME_BLOCK:(tb + 1) * TIME_BLOCK, hp * LANES:(hp + 1) * LANES].T
            top, bot = xt[:N_B], xt[N_B:]
            vt_o[0, tb, hp, 0] = jnp.where(lo_half, top, pltpu.roll(bot, N_B, 1))
            vt_o[0, tb, hp, 1] = jnp.where(lo_half, pltpu.roll(top, N_B, 1), bot)
    wf_o[0] = decay[:, :RW]
    wb_o[0] = decay[:, RW:]
    kdf_o[0] = kd_f
    kdb_o[0] = kd_b
    bf_o[0] = b_f
    bb_o[0] = b_b
    rbf_o[0] = r * p_f
    rbb_o[0] = r * p_b
    ktf_o[0] = (kd_f * ip_f).astype(BF16)
    ktb_o[0] = (kd_b * ip_b).astype(BF16)
    btf_o[0] = (b_f * ip_f).astype(BF16)
    btb_o[0] = (b_b * ip_b).astype(BF16)
    g_o[0] = g
    bvg_o[0] = bonus * v * g


def _cumsum3(tri, x):
    h1 = x.astype(BF16)
    r1 = x - h1.astype(F32)
    h2 = r1.astype(BF16)
    h3 = (r1 - h2.astype(F32)).astype(BF16)
    return (jnp.dot(tri, h1, preferred_element_type=F32) + jnp.dot(tri, h2, preferred_element_type=F32)
            + jnp.dot(tri, h3, preferred_element_type=F32))


def _chunk_tri(tm):
    t = np.arange(tm)
    same = (t[:, None] // SCAN_CHUNK) == (t[None, :] // SCAN_CHUNK)
    return jnp.asarray(np.stack([same & (t[None, :] <= t[:, None]), same & (t[None, :] >= t[:, None])]), BF16)


def _block_diag2(a, b):
    z = jnp.zeros_like(a)
    return jnp.concatenate([jnp.concatenate([a, z], 1), jnp.concatenate([z, b], 1)], 0)


def _rwkv_prep(rw, seq_len, mu_prev, mu_next, w0, w2, a0, a2, g2, k_k, k_a, r_k, v_first, vres):
    B, Lp, n_rw = rw.shape
    tm = _row_tile(Lp)
    tpb = tm // SUBLANES
    first = vres is None
    vec = lambda a: a.reshape(1, -1)
    consts = [vec(mu_prev), vec(mu_next), _block_diag2(w2[0], w2[1]).astype(BF16), vec(w0),
              _block_diag2(a2[0], a2[1]).astype(BF16), vec(a0), g2.astype(BF16), vec(k_k), vec(k_a), vec(r_k),
              _seg_ones(RW), _chunk_tri(tm)]
    tile = lambda w: pl.BlockSpec((1, tm, w), lambda b, i: (b, i, 0))
    full = lambda a: pl.BlockSpec(a.shape, lambda b, i: (0,) * a.ndim)
    in_specs = [tile(n_rw),
                pl.BlockSpec((1, SUBLANES, n_rw), lambda b, i: (b, jnp.maximum(i * tpb - 1, 0), 0)),
                pl.BlockSpec((1, SUBLANES, n_rw), lambda b, i: (b, jnp.minimum((i + 1) * tpb, Lp // SUBLANES - 1), 0))]
    in_specs += [full(c) for c in consts]
    args = [rw, rw, rw] + consts
    if not first:
        v0, v1, v2 = vres
        v1p = jnp.concatenate([v1, jnp.zeros((RW, LANES - VRES_LORA), F32)], 1).astype(BF16)
        v2p = jnp.concatenate([v2, jnp.zeros((LANES - VRES_LORA, RW), F32)], 0).astype(BF16)
        extra = [vec(v0), v1p, v2p]
        in_specs += [tile(RW)] + [full(c) for c in extra]
        args += [v_first] + extra
    vt_shape = (B, Lp // TIME_BLOCK, H_B // 2, 2, N_B, LANES)
    vt_spec = pl.BlockSpec((1, tm // TIME_BLOCK) + vt_shape[2:], lambda b, i: (b, i, 0, 0, 0, 0))
    shapes = [jax.ShapeDtypeStruct((B, Lp, RW), BF16 if i in (1, 6, 7, 12, 13) else F32) for i in range(16)]
    shapes.append(jax.ShapeDtypeStruct(vt_shape, F32))
    specs = [tile(RW)] * 16 + [vt_spec]
    if first:
        shapes.append(jax.ShapeDtypeStruct((B, Lp, RW), F32))
        specs.append(tile(RW))
    out = pl.pallas_call(
        functools.partial(_prep_kernel, first=first, seq_len=seq_len, tm=tm),
        out_shape=tuple(shapes),
        grid=(B, Lp // tm),
        in_specs=in_specs,
        out_specs=tuple(specs),
        compiler_params=_cparams(("parallel", "parallel")),
        name="rwkv_prep",
    )(*args)
    return out


def _scan_kernel(kkf, wf, kdf, bf, vf, rbf, ktf, btf, vtf, kkb, wb, kdb, bb_, vb_, rbb, ktb, btb, vtb, ones_ref,
                 yf_ref, yb_ref, s_ref, s0_ref, u_ref, *, nb):
    j = pl.program_id(1)
    C = nb * 4
    half_t = TIME_BLOCK // 2
    assert half_t == SCAN_CHUNK

    @pl.when(j == 0)
    def _():
        s_ref[...] = jnp.zeros_like(s_ref)

    lane1 = lax.broadcasted_iota(jnp.int32, (N_B, LANES), 1)
    lo_half = lane1 < N_B

    ones = ones_ref[...]
    lane_id = lax.broadcasted_iota(jnp.int32, (C * N_B, LANES), 1)
    lane = lane_id % N_B
    head_base = (lane_id // N_B) * N_B
    row_refs = ((kkf, wf, kdf, bf), (kkb, wb, kdb, bb_))
    chunk_refs = ((vf, rbf, ktf, btf, yf_ref), (vb_, rbb, ktb, btb, yb_ref))
    vt_refs = (vtf, vtb)
    tr = lax.broadcasted_iota(jnp.int32, (N_B, LANES), 0)
    tc = lane1 % N_B
    earlier2 = (tc <= tr, tc >= tr)
    lanes_dims = ((1,), (1,))

    def step(d, tiles, row, oh, tt):
        def rows(tile):
            return jnp.concatenate(
                [jnp.broadcast_to(tile[n, row:row + 1, hp * LANES:(hp + 1) * LANES], (N_B, LANES))
                 for n in range(nb) for hp in range(4)], axis=0)

        kk, w, kd, b = [rows(x) for x in tiles]
        S = s_ref[d]
        sa = jnp.dot((S * kk).astype(BF16), ones[:LANES], preferred_element_type=F32)
        vt = jnp.concatenate([vt_refs[d][n, 0, hp, oh] for n in range(nb) for hp in range(4)], axis=0)
        vb = jnp.take_along_axis(vt, head_base + tt, axis=1, mode="promise_in_bounds")
        s_ref[d] = S * w - sa * b + vb * kd
        u_ref[d] = jnp.where(lane == tt, sa, u_ref[d])

    def make_body(half):
        def body(g, carry):
            t0f = pl.multiple_of(half * half_t + g * SUBLANES, SUBLANES)
            t0b = pl.multiple_of(TIME_BLOCK - SUBLANES - (half * half_t + g * SUBLANES), SUBLANES)
            tiles_f = [x[:, pl.ds(t0f, SUBLANES), :] for x in row_refs[0]]
            tiles_b = [x[:, pl.ds(t0b, SUBLANES), :] for x in row_refs[1]]
            for i in range(SUBLANES):
                step(0, tiles_f, i, half, g * SUBLANES + i)
                step(1, tiles_b, SUBLANES - 1 - i, 1 - half, half_t - 1 - (g * SUBLANES + i))
            return carry
        return body

    def chunk_outputs(d, oh):
        vref, rref, ktref, btref, yref = chunk_refs[d]
        r0 = oh * half_t
        for n in range(nb):
            for hp in range(4):
                c = n * 4 + hp
                blk = lambda ref: ref[n, r0:r0 + half_t, hp * LANES:(hp + 1) * LANES]
                rb, kt, bt, v = blk(rref), blk(ktref), blk(btref), blk(vref)
                s0 = s0_ref[d, c * N_B:(c + 1) * N_B, :]
                ut = u_ref[d, c * N_B:(c + 1) * N_B, :]
                u2 = jnp.concatenate([ut, ut], axis=0).T
                r2 = jnp.concatenate([jnp.where(lo_half, rb, 0.0), jnp.where(lo_half, 0.0, rb)], axis=0)
                r2h, r2l = _split_bf16(r2)
                s0h = s0.astype(BF16)
                rhs = jnp.concatenate([kt, bt, s0h, s0h], axis=0)
                gb = lax.dot_general(jnp.concatenate([r2h, r2l], axis=1), jnp.concatenate([rhs, rhs], axis=1),
                                     (lanes_dims, ((), ())), preferred_element_type=F32)
                vh = v
                per_head = []
                for h in range(2):
                    gh = gb[h * N_B:(h + 1) * N_B]
                    gh_, gl_ = _split_bf16(jnp.where(earlier2[d], gh[:, :LANES], 0.0))
                    w = jnp.concatenate([vh, (-u2[h * N_B:(h + 1) * N_B]).astype(BF16)], axis=0)
                    corr = jnp.dot(jnp.concatenate([gh_, gl_], axis=1), jnp.concatenate([w, w], axis=0),
                                   preferred_element_type=F32)
                    per_head.append(gh[:, LANES:] + corr)
                yref[n, r0:r0 + half_t, hp * LANES:(hp + 1) * LANES] = jnp.where(lo_half, per_head[0], per_head[1])

    for half in range(2):
        s0_ref[...] = s_ref[...]
        u_ref[...] = jnp.zeros_like(u_ref)
        lax.fori_loop(0, half_t // SUBLANES, make_body(half), 0)
        chunk_outputs(0, half)
        chunk_outputs(1, 1 - half)


def _wkv_bidir(kk, v, vt, per_dir, nb):
    B, Lp, _ = kk.shape
    nblk = Lp // TIME_BLOCK
    ones = _seg_ones(LANES)
    fwd = lambda bi, j: (bi, j, 0)
    bwd = lambda bi, j: (bi, nblk - 1 - j, 0)
    blk = (nb, TIME_BLOCK, RW)
    vt_blk = (nb, 1) + vt.shape[2:]
    (wf, kdf, bf, rbf, ktf, btf), (wb, kdb, bb_, rbb, ktb, btb) = per_dir
    y_shape = jax.ShapeDtypeStruct((B, Lp, RW), F32)
    rows = nb * 4 * N_B
    return pl.pallas_call(
        functools.partial(_scan_kernel, nb=nb),
        out_shape=(y_shape, y_shape),
        grid=(B // nb, nblk),
        in_specs=[pl.BlockSpec(blk, fwd)] * 8 + [pl.BlockSpec(vt_blk, lambda bi, j: (bi, j, 0, 0, 0, 0))]
                 + [pl.BlockSpec(blk, bwd)] * 8 + [pl.BlockSpec(vt_blk, lambda bi, j: (bi, nblk - 1 - j, 0, 0, 0, 0))]
                 + [pl.BlockSpec(ones.shape, lambda bi, j: (0, 0))],
        out_specs=(pl.BlockSpec(blk, fwd), pl.BlockSpec(blk, bwd)),
        scratch_shapes=[pltpu.VMEM((2, rows, LANES), F32),
                        pltpu.VMEM((2, rows, LANES), F32),
                        pltpu.VMEM((2, rows, LANES), F32)],
        compiler_params=_cparams(("parallel", "arbitrary")),
        name="wkv_scan",
    )(kk, wf, kdf, bf, v, rbf, ktf, btf, vt, kk, wb, kdb, bb_, v, rbb, ktb, btb, vt, ones)


def _post_kernel(yf_ref, yb_ref, g_ref, bvg_ref, att_ref, h_ref, gng_ref, gnb_ref, wo_ref, l1g_ref, l1b_ref,
                 wrh_ref, wrl_ref, rb_ref, tri_ref, ones_ref,
                 h1_ref, rf_ref, ri_ref, cnt_ref, *, alpha, tm):
    ones = ones_ref[...]
    y = yf_ref[...] + yb_ref[...]
    mu = _seg_sum(y, ones) * (1.0 / N_B)
    yc = y - mu
    var = _seg_sum(yc * yc, ones) * (1.0 / N_B)
    yr = (yc * lax.rsqrt(var + RWKV_GN_EPS) * gng_ref[...] + gnb_ref[...]) * g_ref[...] + bvg_ref[...]
    half = H_A * V_DIM
    mixed = (jnp.dot(att_ref[...], wo_ref[:half, :], preferred_element_type=F32)
             + jnp.dot(yr.astype(BF16), wo_ref[half:, :], preferred_element_type=F32))
    h1 = _ln(h_ref[...] * alpha + mixed, l1g_ref[...], l1b_ref[...])
    h1_ref[...] = h1

    xh, xl = _split_bf16(h1)
    x = (jnp.dot(xh, wrh_ref[...], preferred_element_type=F32) + jnp.dot(xl, wrh_ref[...], preferred_element_type=F32)
         + jnp.dot(xh, wrl_ref[...], preferred_element_type=F32)) + rb_ref[...]
    lane = lax.broadcasted_iota(jnp.int32, x.shape, 1)
    lanef = lane.astype(F32)
    big = float(LANES)
    gmask = lane < N_GROUPS
    gx = jnp.where(gmask, x, NEG)
    gmax = jnp.max(gx, -1, keepdims=True)
    gidx = jnp.min(jnp.where(gx == gmax, lanef, big), -1, keepdims=True)
    gsum = jnp.sum(jnp.where(gmask, jnp.exp(gx - gmax), 0.0), -1, keepdims=True)
    lo = N_GROUPS + EXPERTS_PER_GROUP * gidx
    emask = (lanef >= lo) & (lanef < lo + EXPERTS_PER_GROUP)
    ex = jnp.where(emask, x, NEG)
    m1 = jnp.max(ex, -1, keepdims=True)
    i1 = jnp.min(jnp.where(emask & (ex == m1), lanef, big), -1, keepdims=True)
    ex2 = jnp.where(lanef == i1, NEG, ex)
    m2 = jnp.max(ex2, -1, keepdims=True)
    i2 = jnp.min(jnp.where(emask & (lanef != i1) & (ex2 == m2), lanef, big), -1, keepdims=True)
    esum = jnp.sum(jnp.where(emask, jnp.exp(ex - m1), 0.0), -1, keepdims=True)
    gp = 1.0 / gsum
    gate0 = gp * (1.0 / esum)
    gate1 = gp * (jnp.exp(m2 - m1) / esum)
    e0 = i1 - N_GROUPS
    e1 = i2 - N_GROUPS

    @pl.when(pl.program_id(0) == 0)
    def _():
        cnt_ref[...] = jnp.zeros_like(cnt_ref)

    onehot = jnp.where((lanef == e0) | (lanef == e1), 1.0, 0.0)
    prefix = jnp.dot(tri_ref[...], onehot.astype(BF16), preferred_element_type=F32) + cnt_ref[...]
    r0 = jnp.sum(jnp.where(lanef == e0, prefix, 0.0), -1, keepdims=True)
    r1 = jnp.sum(jnp.where(lanef == e1, prefix, 0.0), -1, keepdims=True)
    cnt_ref[...] += jnp.sum(onehot, 0, keepdims=True)
    rf_ref[...] = jnp.where(lane == 0, gate0, jnp.where(lane == 1, gate1, 0.0))
    ri = jnp.where(lane == 0, e0, jnp.where(lane == 1, e1, jnp.where(lane == 2, r0, jnp.where(lane == 3, r1, 0.0))))
    ri_ref[...] = ri.astype(jnp.int32)


def _post_mixer(yf, yb, g, bvg, att, h, gn_g, gn_b, w_out, ln_g, ln_b, w_group, b_group, w_expert, b_expert, alpha):
    T, D = h.shape
    tm = _row_tile(T)
    vec = lambda a: a.reshape(1, -1)
    zpad = LANES - N_GROUPS - N_EXPERTS
    wr = jnp.concatenate([w_group, w_expert, jnp.zeros((D, zpad), F32)], axis=1)
    wrh = wr.astype(BF16)
    wrl = (wr - wrh.astype(F32)).astype(BF16)
    rb = jnp.concatenate([b_group, b_expert, jnp.zeros((zpad,), F32)]).reshape(1, LANES)
    tri = jnp.asarray(np.arange(tm)[:, None] > np.arange(tm)[None, :], BF16)
    consts = [vec(gn_g), vec(gn_b), w_out.astype(BF16), vec(ln_g), vec(ln_b), wrh, wrl, rb, tri, _seg_ones(RW)]
    row = lambda w: pl.BlockSpec((tm, w), lambda i: (i, 0))
    full = lambda a: pl.BlockSpec(a.shape, lambda i: (0, 0))
    return pl.pallas_call(
        functools.partial(_post_kernel, alpha=alpha, tm=tm),
        out_shape=(jax.ShapeDtypeStruct((T, D), F32),
                   jax.ShapeDtypeStruct((T, LANES), F32),
                   jax.ShapeDtypeStruct((T, LANES), jnp.int32),
                   jax.ShapeDtypeStruct((1, LANES), F32)),
        grid=(T // tm,),
        in_specs=[row(RW)] * 4 + [row(H_A * V_DIM), row(D)] + [full(c) for c in consts],
        out_specs=(row(D), row(LANES), row(LANES), pl.BlockSpec((1, LANES), lambda i: (0, 0))),
        compiler_params=_cparams(("arbitrary",)),
        name="post_mixer",
    )(yf, yb, g, bvg, att, h, *consts)


def _row_copies(dest_smem, t, make):
    return [make(s, dest_smem[TOP_K * t + s]) for s in range(TOP_K)]


def _load_dest(dest_hbm, dest_smem, isem, tm):
    load = pltpu.make_async_copy(dest_hbm.at[pl.ds(pl.program_id(0) * tm * TOP_K, tm * TOP_K)], dest_smem, isem)
    load.start()
    load.wait()


def _dispatch_kernel(dest_hbm, h_ref, xz_hbm, xb_hbm, dest_smem, isem, sem, *, tm):
    del xz_hbm
    _load_dest(dest_hbm, dest_smem, isem, tm)

    def copies(t):
        src = h_ref.at[pl.ds(t, 1)]
        return _row_copies(dest_smem, t, lambda s, dst: pltpu.make_async_copy(src, xb_hbm.at[pl.ds(dst, 1)], sem))

    def start(t, c):
        for cp in copies(t):
            cp.start()
        return c

    def wait(t, c):
        for cp in copies(t):
            cp.wait()
        return c

    lax.fori_loop(0, tm, start, 0, unroll=8)
    lax.fori_loop(0, tm, wait, 0, unroll=8)


def _dispatch(dest, h, n_rows):
    T, D = h.shape
    tm = _row_tile(T)
    any_spec = pl.BlockSpec(memory_space=pl.ANY)
    return pl.pallas_call(
        functools.partial(_dispatch_kernel, tm=tm),
        out_shape=jax.ShapeDtypeStruct((n_rows, D), F32),
        grid=(T // tm,),
        in_specs=[any_spec, pl.BlockSpec((tm, D), lambda i: (i, 0)), any_spec],
        out_specs=any_spec,
        scratch_shapes=[pltpu.SMEM((tm * TOP_K,), jnp.int32), pltpu.SemaphoreType.DMA, pltpu.SemaphoreType.DMA],
        input_output_aliases={2: 0},
        compiler_params=_cparams(("arbitrary",)),
        name="moe_dispatch",
    )(dest, h, jnp.zeros((n_rows, D), F32))


def _expert_kernel(be_ref, nu_ref, x_ref, w1_ref, w3_ref, w2_ref, o_ref):
    @pl.when(pl.program_id(0) < nu_ref[0])
    def _():
        x = x_ref[...].astype(BF16)
        a = jnp.dot(x, w1_ref[0], preferred_element_type=F32)
        b = jnp.dot(x, w3_ref[0], preferred_element_type=F32)
        hid = (a * jax.nn.sigmoid(a)) * b
        o_ref[...] = jnp.dot(hid.astype(BF16), w2_ref[0], preferred_element_type=F32)


def _expert_ffn(xb, block_e, n_used, w1, w3, w2):
    n_rows, D = xb.shape
    n_blocks = n_rows // MOE_BLOCK
    blk = lambda i, be, nu: jnp.minimum(i, nu[0] - 1)
    wmap = lambda i, be, nu: (be[blk(i, be, nu)], 0, 0)
    grid_spec = pltpu.PrefetchScalarGridSpec(
        num_scalar_prefetch=2,
        grid=(n_blocks,),
        in_specs=[pl.BlockSpec((MOE_BLOCK, D), lambda i, be, nu: (blk(i, be, nu), 0)),
                  pl.BlockSpec((1, D, E_HID), wmap),
                  pl.BlockSpec((1, D, E_HID), wmap),
                  pl.BlockSpec((1, E_HID, D), wmap)],
        out_specs=pl.BlockSpec((MOE_BLOCK, D), lambda i, be, nu: (blk(i, be, nu), 0)),
    )
    return pl.pallas_call(
        _expert_kernel,
        out_shape=jax.ShapeDtypeStruct((n_rows, D), F32),
        grid_spec=grid_spec,
        compiler_params=_cparams(("arbitrary",)),
        name="moe_experts",
    )(block_e, n_used, xb, w1, w3, w2)


def _combine_kernel(dest_hbm, yb_hbm, gate_ref, h_ref, g_ref, b_ref, o_ref, dest_smem, buf, isem, sem, *, tm, alpha):
    _load_dest(dest_hbm, dest_smem, isem, tm)

    def copies(t):
        return _row_copies(dest_smem, t, lambda s, src: pltpu.make_async_copy(
            yb_hbm.at[pl.ds(src, 1)], buf.at[s, pl.ds(t, 1)], sem))

    def start(t, c):
        for cp in copies(t):
            cp.start()
        return c

    def wait(t, c):
        for cp in copies(t):
            cp.wait()
        return c

    lax.fori_loop(0, tm, start, 0, unroll=8)
    lax.fori_loop(0, tm, wait, 0, unroll=8)
    gate = gate_ref[...]
    ff = sum(buf[s] * gate[:, s:s + 1] for s in range(TOP_K))
    o_ref[...] = _ln(h_ref[...] * alpha + ff, g_ref[...], b_ref[...])


def _combine(dest, yb, route_f, h, ln_g, ln_b, alpha):
    T, D = h.shape
    tm = _row_tile(T)
    any_spec = pl.BlockSpec(memory_space=pl.ANY)
    row = lambda w: pl.BlockSpec((tm, w), lambda i: (i, 0))
    vec = pl.BlockSpec((1, D), lambda i: (0, 0))
    return pl.pallas_call(
        functools.partial(_combine_kernel, tm=tm, alpha=alpha),
        out_shape=jax.ShapeDtypeStruct((T, D), F32),
        grid=(T // tm,),
        in_specs=[any_spec, any_spec, row(LANES), row(D), vec, vec],
        out_specs=row(D),
        scratch_shapes=[pltpu.SMEM((tm * TOP_K,), jnp.int32), pltpu.VMEM((TOP_K, tm, D), F32),
                        pltpu.SemaphoreType.DMA, pltpu.SemaphoreType.DMA],
        compiler_params=_cparams(("arbitrary",)),
        name="moe_combine",
    )(dest, yb, route_f, h, ln_g.reshape(1, D), ln_b.reshape(1, D))


def _hier_moe(h, route_f, route_i, counts, w1, w3, w2, ln_g, ln_b, alpha):
    T, D = h.shape
    counts = counts[0, :N_EXPERTS].astype(jnp.int32)
    padded = (counts + MOE_BLOCK - 1) // MOE_BLOCK * MOE_BLOCK
    pends = jnp.cumsum(padded)
    pstart = pends - padded
    n_rows = (T * TOP_K + N_EXPERTS * (MOE_BLOCK - 1) + MOE_BLOCK - 1) // MOE_BLOCK * MOE_BLOCK
    n_blocks = n_rows // MOE_BLOCK
    block_start = jnp.arange(n_blocks, dtype=jnp.int32) * MOE_BLOCK
    block_e = jnp.minimum(jnp.sum(pends[None, :] <= block_start[:, None], axis=1), N_EXPERTS - 1).astype(jnp.int32)
    n_used = (pends[-1:] // MOE_BLOCK).astype(jnp.int32)
    dest = (jnp.take(pstart, route_i[:, :TOP_K]) + route_i[:, TOP_K:2 * TOP_K]).reshape(-1)
    xb = _dispatch(dest, h, n_rows)
    yb = _expert_ffn(xb, block_e, n_used, w1, w3, w2)
    return _combine(dest, yb, route_f, h, ln_g, ln_b, alpha)


def kernel(x, positions, meta_tokens, emb_ln_g, emb_ln_b, w_in, mla_q_norm, mla_kv_norm, mla_w_uq, mla_w_ukv, mla_out_norm, rwkv_mu_prev, rwkv_mu_next, rwkv_w0, rwkv_w2, rwkv_a0, rwkv_a2, rwkv_g2, rwkv_k_k, rwkv_k_a, rwkv_r_k, rwkv_gn_g, rwkv_gn_b, rwkv_v0, rwkv_v1, rwkv_v2, w_out, ln1_g, ln1_b, moe_w_group, moe_b_group, moe_w_expert, moe_b_expert, moe_w1, moe_w3, moe_w2, ln2_g, ln2_b):
    B, seq, D = x.shape
    depth = w_in.shape[0]
    alpha = (2 * depth) ** 0.25
    L = seq + N_META
    Lp = -(-L // LANES) * LANES
    T = B * Lp
    nb = next(n for n in (4, 2, 1) if B % n == 0)
    tk = 384 if Lp % 384 == 0 else LANES
    tq = next(t for t in (1408, 1152, 768, 384, LANES) if Lp % t == 0)

    meta = jnp.broadcast_to(meta_tokens.astype(x.dtype)[None], (B, N_META, D))
    h = jnp.concatenate([meta, x, jnp.zeros((B, Lp - L, D), x.dtype)], axis=1).reshape(T, D)
    h = _ln_residual(h, jnp.zeros_like(h), emb_ln_g, emb_ln_b, 1.0)

    key_bias = jnp.where(jnp.arange(Lp) < L, 0.0, NEG).astype(F32)[None, :]
    pos = jnp.concatenate([jnp.broadcast_to(jnp.arange(N_META, dtype=jnp.int32), (B, N_META)),
                           positions + N_META, jnp.zeros((B, Lp - L), jnp.int32)], axis=1)
    inv_freq = ROPE_THETA ** (-jnp.arange(0, QK_ROPE, 2, dtype=F32) / QK_ROPE)
    ang = pos.astype(F32)[..., None] * inv_freq
    cos, sin = jnp.cos(ang).reshape(T, -1), jnp.sin(ang).reshape(T, -1)
    zeros = jnp.zeros((T, LANES - QK_ROPE), F32)
    rope_cos = jnp.concatenate([cos, cos, zeros], axis=1)
    rope_sin = jnp.concatenate([-sin, sin, zeros], axis=1)
    qscale = (QK_NOPE + QK_ROPE) ** -0.5 * math.log2(math.e)

    v_first = None
    for li in range(depth):
        q, k, v, rw = _project(h, w_in[li], mla_q_norm[li], mla_kv_norm[li], mla_w_uq[li], mla_w_ukv[li],
                               rope_cos, rope_sin, qscale)
        y_att = _attention(q.reshape(B, Lp, -1), k.reshape(B, Lp, -1), v.reshape(B, Lp, -1),
                           key_bias, mla_out_norm[li], tq, tk)
        vres = None if li == 0 else (rwkv_v0[li - 1], rwkv_v1[li - 1], rwkv_v2[li - 1])
        outs = _rwkv_prep(rw.reshape(B, Lp, -1), L, rwkv_mu_prev[li], rwkv_mu_next[li], rwkv_w0[li], rwkv_w2[li],
                          rwkv_a0[li], rwkv_a2[li], rwkv_g2[li], rwkv_k_k[li], rwkv_k_a[li], rwkv_r_k[li],
                          v_first, vres)
        kk, vm, g, bvg, vt = outs[0], outs[1], outs[14], outs[15], outs[16]
        if li == 0:
            v_first = outs[17]
        y_f, y_b = _wkv_bidir(kk, vm, vt, (outs[2:8], outs[8:14]), nb)
        flat = lambda a: a.reshape(T, -1)
        h, route_f, route_i, counts = _post_mixer(
            flat(y_f), flat(y_b), flat(g), flat(bvg), flat(y_att), h, rwkv_gn_g[li], rwkv_gn_b[li], w_out[li],
            ln1_g[li], ln1_b[li], moe_w_group[li], moe_b_group[li], moe_w_expert[li], moe_b_expert[li], alpha)
        h = _hier_moe(h, route_f, route_i, counts,
                      moe_w1[li].astype(BF16), moe_w3[li].astype(BF16), moe_w2[li].astype(BF16),
                      ln2_g[li], ln2_b[li], alpha)
    return h.reshape(B, Lp, D)[:, N_META:L]
```

```python
import functools
import math

import numpy as np
import jax
import jax.numpy as jnp
from jax import lax
from jax.experimental import pallas as pl
from jax.experimental.pallas import tpu as pltpu

F32 = jnp.float32
BF16 = jnp.bfloat16

N_META = 16
H_A = 4
QK_NOPE = 128
QK_ROPE = 64
V_DIM = 128
Q_LORA = 256
KV_LORA = 128
ROPE_THETA = 10000.0
H_B = 8
N_B = 64
RW = H_B * N_B
DECAY_LORA = 64
AAA_LORA = 64
GATE_LORA = 128
VRES_LORA = 32
RWKV_GN_EPS = 64e-5
MLA_COLS = Q_LORA + KV_LORA + QK_ROPE
N_GROUPS = 4
EXPERTS_PER_GROUP = 8
N_EXPERTS = N_GROUPS * EXPERTS_PER_GROUP
TOP_K = 2
E_HID = 256
MOE_BLOCK = 256
LN_EPS = 1e-5
RMS_EPS = 1e-6

LANES = 128
SUBLANES = 8
TIME_BLOCK = 128
SCAN_CHUNK = 64
ROW_TILE = 512
VMEM_LIMIT = 56 * 1024 * 1024
HEAD_W = 2 * LANES
NEG = -1e30


def _cparams(sem):
    return pltpu.CompilerParams(dimension_semantics=sem, vmem_limit_bytes=VMEM_LIMIT)


def _row_tile(m):
    return next(t for t in (ROW_TILE, 384, 256, LANES) if m % t == 0)


def _split_bf16(x):
    hi = x.astype(BF16)
    lo = (x - hi.astype(F32)).astype(BF16)
    return hi, lo


def _seg_sum(x, ones):
    hi, lo = _split_bf16(x)
    return jnp.dot(jnp.concatenate([hi, lo], axis=-1), ones, preferred_element_type=F32)


def _seg_ones(width):
    m = np.arange(2 * width)[:, None] % width
    n = np.arange(width)[None, :]
    return jnp.asarray(m // N_B == n // N_B, BF16)


def _ln(x, g, b):
    mu = jnp.mean(x, -1, keepdims=True)
    xc = x - mu
    var = jnp.mean(xc * xc, -1, keepdims=True)
    return xc * lax.rsqrt(var + LN_EPS) * g + b


def _ln_kernel(x_ref, r_ref, g_ref, b_ref, o_ref, *, alpha):
    o_ref[...] = _ln(x_ref[...] * alpha + r_ref[...], g_ref[...], b_ref[...])


def _ln_residual(x, r, g, b, alpha):
    M, D = x.shape
    tm = _row_tile(M)
    row = pl.BlockSpec((tm, D), lambda i: (i, 0))
    vec = pl.BlockSpec((1, D), lambda i: (0, 0))
    return pl.pallas_call(
        functools.partial(_ln_kernel, alpha=alpha),
        out_shape=jax.ShapeDtypeStruct((M, D), F32),
        grid=(M // tm,),
        in_specs=[row, row, vec, vec],
        out_specs=row,
        compiler_params=_cparams(("parallel",)),
        name="layer_norm",
    )(x, r, g.reshape(1, D), b.reshape(1, D))


def _rms(x, g):
    return x * lax.rsqrt(jnp.mean(x * x, -1, keepdims=True) + RMS_EPS) * g


def _proj_kernel(h_ref, win_ref, qn_ref, kvn_ref, wuq_ref, wukv_ref, c_ref, s_ref,
                 q_ref, k_ref, v_ref, rw_ref, *, qscale):
    proj = jnp.dot(h_ref[...].astype(BF16), win_ref[...], preferred_element_type=F32)
    rw_ref[...] = proj[:, MLA_COLS + N_B:]
    q = jnp.dot(_rms(proj[:, :Q_LORA], qn_ref[...]).astype(BF16), wuq_ref[...], preferred_element_type=F32)
    kv = jnp.dot(_rms(proj[:, Q_LORA:Q_LORA + KV_LORA], kvn_ref[...]).astype(BF16), wukv_ref[...],
                 preferred_element_type=F32)
    cos, sin = c_ref[...], s_ref[...]
    lane = lax.broadcasted_iota(jnp.int32, cos.shape, 1)
    half = QK_ROPE // 2

    def rope(x):
        partner = jnp.where(lane < half, pltpu.roll(x, LANES - half, 1), pltpu.roll(x, half, 1))
        return x * cos + partner * sin

    k_pe = rope(proj[:, Q_LORA + KV_LORA:Q_LORA + KV_LORA + LANES]).astype(BF16)
    for h in range(H_A):
        o = h * HEAD_W
        q_ref[:, o:o + LANES] = (q[:, o:o + LANES] * qscale).astype(BF16)
        q_ref[:, o + LANES:o + HEAD_W] = (rope(q[:, o + LANES:o + HEAD_W]) * qscale).astype(BF16)
        k_ref[:, o:o + LANES] = kv[:, o:o + LANES].astype(BF16)
        k_ref[:, o + LANES:o + HEAD_W] = k_pe
        v_ref[:, h * V_DIM:(h + 1) * V_DIM] = kv[:, o + LANES:o + HEAD_W].astype(BF16)


def _project(h, w_in, q_norm, kv_norm, w_uq, w_ukv, rope_cos, rope_sin, qscale):
    T, D = h.shape
    tm = _row_tile(T)
    n_rw = w_in.shape[1] - MLA_COLS
    pad = jnp.zeros((D, N_B), F32)
    win = jnp.concatenate([w_in[:, :MLA_COLS], pad, w_in[:, MLA_COLS:]], axis=1).astype(BF16)
    wq = w_uq.reshape(Q_LORA, H_A, QK_NOPE + QK_ROPE)
    wq = jnp.concatenate([wq, jnp.zeros((Q_LORA, H_A, HEAD_W - QK_NOPE - QK_ROPE), F32)], axis=-1)
    wq = wq.reshape(Q_LORA, H_A * HEAD_W).astype(BF16)
    row = lambda w: pl.BlockSpec((tm, w), lambda i: (i, 0))
    full = lambda a: pl.BlockSpec(a.shape, lambda i: (0, 0))
    args = (h, win, q_norm.reshape(1, -1), kv_norm.reshape(1, -1), wq, w_ukv.astype(BF16), rope_cos, rope_sin)
    return pl.pallas_call(
        functools.partial(_proj_kernel, qscale=qscale),
        out_shape=(jax.ShapeDtypeStruct((T, H_A * HEAD_W), BF16),
                   jax.ShapeDtypeStruct((T, H_A * HEAD_W), BF16),
                   jax.ShapeDtypeStruct((T, H_A * V_DIM), BF16),
                   jax.ShapeDtypeStruct((T, n_rw), F32)),
        grid=(T // tm,),
        in_specs=[row(D)] + [full(a) for a in args[1:6]] + [row(LANES), row(LANES)],
        out_specs=(row(H_A * HEAD_W), row(H_A * HEAD_W), row(H_A * V_DIM), row(n_rw)),
        compiler_params=_cparams(("parallel",)),
        name="in_proj",
    )(*args)


def _attn_kernel(q_ref, k_ref, v_ref, bias_ref, g_ref, o_ref, m_ref, acc_ref, s_ref, p_ref, a_ref, *, tk):
    q = q_ref[0]
    n = k_ref.shape[1] // tk
    one_col = (lax.broadcasted_iota(jnp.int32, (tk, V_DIM), 1) == 0).astype(BF16)

    def scores(c):
        off = pl.multiple_of(c * tk, tk)
        s = lax.dot_general(q, k_ref[0, pl.ds(off, tk), :], (((1,), (1,)), ((), ())), preferred_element_type=F32)
        return s + bias_ref[:, pl.ds(off, tk)]

    def weighted_values(c):
        off = pl.multiple_of(c * tk, tk)
        v1 = jnp.concatenate([v_ref[0, pl.ds(off, tk), :], one_col], axis=1)
        acc_ref[...] = a_ref[...] * acc_ref[...] + jnp.dot(p_ref[...], v1, preferred_element_type=F32)

    m_ref[...] = jnp.full_like(m_ref, NEG)
    acc_ref[...] = jnp.zeros_like(acc_ref)
    p_ref[...] = jnp.zeros_like(p_ref)
    a_ref[...] = jnp.ones_like(a_ref)
    s_ref[...] = scores(0)

    def body(c, carry):
        weighted_values(jnp.maximum(c - 1, 0))
        s = s_ref[...]
        s_ref[...] = scores(jnp.minimum(c + 1, n - 1))
        m_prev = m_ref[...]
        m_new = jnp.maximum(m_prev, jnp.max(s, -1, keepdims=True))
        p_ref[...] = jnp.exp2(s - m_new).astype(BF16)
        a_ref[...] = jnp.exp2(m_prev - m_new)
        m_ref[...] = m_new
        return carry

    lax.fori_loop(0, n, body, 0)
    weighted_values(n - 1)
    acc = acc_ref[...]
    o = acc[:, :V_DIM] / acc[:, V_DIM:V_DIM + 1]
    o = o * lax.rsqrt(jnp.mean(o * o, -1, keepdims=True) + RMS_EPS) * g_ref[...]
    o_ref[0] = o.astype(BF16)


def _attention(q, k, v, bias, out_gain, tq, tk):
    B, Lp, _ = q.shape
    return pl.pallas_call(
        functools.partial(_attn_kernel, tk=tk),
        out_shape=jax.ShapeDtypeStruct((B, Lp, H_A * V_DIM), BF16),
        grid=(B, H_A, Lp // tq),
        in_specs=[pl.BlockSpec((1, tq, HEAD_W), lambda b, h, i: (b, i, h)),
                  pl.BlockSpec((1, Lp, HEAD_W), lambda b, h, i: (b, 0, h)),
                  pl.BlockSpec((1, Lp, V_DIM), lambda b, h, i: (b, 0, h)),
                  pl.BlockSpec((1, Lp), lambda b, h, i: (0, 0)),
                  pl.BlockSpec((1, V_DIM), lambda b, h, i: (0, h))],
        out_specs=pl.BlockSpec((1, tq, V_DIM), lambda b, h, i: (b, i, h)),
        scratch_shapes=[pltpu.VMEM((tq, 1), F32), pltpu.VMEM((tq, 2 * V_DIM), F32),
                        pltpu.VMEM((tq, tk), F32), pltpu.VMEM((tq, tk), BF16), pltpu.VMEM((tq, 1), F32)],
        compiler_params=_cparams(("parallel", "parallel", "parallel")),
        name="mla_attention",
    )(q, k, v, bias, out_gain.reshape(1, H_A * V_DIM))


def _prep_kernel(*refs, first, seq_len, tm):
    if first:
        (rw_ref, pv_ref, nx_ref, mup_ref, mun_ref, w2_ref, w0_ref, a2_ref, a0_ref, g2_ref, kk_ref, ka_ref, rk_ref,
         ones_ref, tri_ref, kk_o, v_o, wf_o, kdf_o, bf_o, rbf_o, ktf_o, btf_o, wb_o, kdb_o, bb_o, rbb_o, ktb_o, btb_o,
         g_o, bvg_o, vt_o, vfirst_o) = refs
    else:
        (rw_ref, pv_ref, nx_ref, mup_ref, mun_ref, w2_ref, w0_ref, a2_ref, a0_ref, g2_ref, kk_ref, ka_ref, rk_ref,
         ones_ref, tri_ref, vf_ref, v0_ref, v1_ref, v2_ref,
         kk_o, v_o, wf_o, kdf_o, bf_o, rbf_o, ktf_o, btf_o, wb_o, kdb_o, bb_o, rbb_o, ktb_o, btb_o, g_o, bvg_o,
         vt_o) = refs
    i = pl.program_id(1)
    n_t = pl.num_programs(1)
    row = lax.broadcasted_iota(jnp.int32, (tm, 1), 0)
    t = i * tm + row
    valid = t < seq_len
    rw = jnp.where(valid, rw_ref[0], 0.0)
    prev_row = jnp.where((i > 0) & (i * tm - 1 < seq_len), pv_ref[0, SUBLANES - 1:SUBLANES, :], 0.0)
    next_row = jnp.where((i < n_t - 1) & ((i + 1) * tm < seq_len), nx_ref[0, 0:1, :], 0.0)
    prev = jnp.where(row == 0, prev_row, pltpu.roll(rw, 1, 0))
    nxt = jnp.where(row == tm - 1, next_row, pltpu.roll(rw, tm - 1, 0))
    u = rw + mup_ref[...] * (prev - rw) + mun_ref[...] * (nxt - rw)
    r, k, v = u[:, :RW], u[:, RW:2 * RW], u[:, 2 * RW:3 * RW]
    wd = u[:, 3 * RW:3 * RW + LANES]
    ad = u[:, 3 * RW + LANES:3 * RW + 2 * LANES]
    gd = u[:, 3 * RW + 2 * LANES:]
    ones = ones_ref[...]
    if first:
        vfirst_o[0] = v
    else:
        low = jnp.dot(v.astype(BF16), v1_ref[...], preferred_element_type=F32)
        mix = jax.nn.sigmoid(v0_ref[...] + jnp.dot(low.astype(BF16), v2_ref[...], preferred_element_type=F32))
        v = v + (vf_ref[0] - v) * mix
    g = jnp.dot(jax.nn.sigmoid(gd).astype(BF16), g2_ref[...], preferred_element_type=F32)
    kk = k * kk_ref[...]
    kk = jnp.where(valid, kk * lax.rsqrt(_seg_sum(kk * kk, ones) + 1e-12), 0.0)
    wl = w0_ref[...] + jnp.dot(jnp.tanh(wd).astype(BF16), w2_ref[...], preferred_element_type=F32)
    logw = -math.exp(-0.5) * jax.nn.sigmoid(wl)
    decay = jnp.exp(logw)
    a = jax.nn.sigmoid(a0_ref[...] + jnp.dot(ad.astype(BF16), a2_ref[...], preferred_element_type=F32))
    ka = ka_ref[...]
    kd_f = jnp.where(valid, k * (1.0 + (a[:, :RW] - 1.0) * ka), 0.0)
    kd_b = jnp.where(valid, k * (1.0 + (a[:, RW:] - 1.0) * ka), 0.0)
    bonus = _seg_sum(r * (kd_f + kd_b) * rk_ref[...], ones)
    b_f, b_b = kk * a[:, :RW], kk * a[:, RW:]
    cum_f = _cumsum3(tri_ref[0], logw[:, :RW])
    cum_b = _cumsum3(tri_ref[1], logw[:, RW:])
    p_f, ip_f = jnp.exp(cum_f), jnp.exp(-cum_f)
    p_b, ip_b = jnp.exp(cum_b), jnp.exp(-cum_b)
    kk_o[0] = kk
    vm = jnp.where(valid, v, 0.0)
    v_o[0] = vm.astype(BF16)
    lo_half = lax.broadcasted_iota(jnp.int32, (N_B, LANES), 1) < N_B
    for tb in range(tm // TIME_BLOCK):
        for hp in range(H_B // 2):
            xt = vm[tb * TIME_BLOCK:(tb + 1) * TIME_BLOCK, hp * LANES:(hp + 1) * LANES].T
            top, bot = xt[:N_B], xt[N_B:]
            vt_o[0, tb, hp, 0] = jnp.where(lo_half, top, pltpu.roll(bot, N_B, 1))
            vt_o[0, tb, hp, 1] = jnp.where(lo_half, pltpu.roll(top, N_B, 1), bot)
    wf_o[0] = decay[:, :RW]
    wb_o[0] = decay[:, RW:]
    kdf_o[0] = kd_f
    kdb_o[0] = kd_b
    bf_o[0] = b_f
    bb_o[0] = b_b
    rbf_o[0] = r * p_f
    rbb_o[0] = r * p_b
    ktf_o[0] = (kd_f * ip_f).astype(BF16)
    ktb_o[0] = (kd_b * ip_b).astype(BF16)
    btf_o[0] = (b_f * ip_f).astype(BF16)
    btb_o[0] = (b_b * ip_b).astype(BF16)
    g_o[0] = g
    bvg_o[0] = bonus * v * g


def _cumsum3(tri, x):
    h1 = x.astype(BF16)
    r1 = x - h1.astype(F32)
    h2 = r1.astype(BF16)
    h3 = (r1 - h2.astype(F32)).astype(BF16)
    return (jnp.dot(tri, h1, preferred_element_type=F32) + jnp.dot(tri, h2, preferred_element_type=F32)
            + jnp.dot(tri, h3, preferred_element_type=F32))


def _chunk_tri(tm):
    t = np.arange(tm)
    same = (t[:, None] // SCAN_CHUNK) == (t[None, :] // SCAN_CHUNK)
    return jnp.asarray(np.stack([same & (t[None, :] <= t[:, None]), same & (t[None, :] >= t[:, None])]), BF16)


def _block_diag2(a, b):
    z = jnp.zeros_like(a)
    return jnp.concatenate([jnp.concatenate([a, z], 1), jnp.concatenate([z, b], 1)], 0)


def _rwkv_prep(rw, seq_len, mu_prev, mu_next, w0, w2, a0, a2, g2, k_k, k_a, r_k, v_first, vres):
    B, Lp, n_rw = rw.shape
    tm = _row_tile(Lp)
    tpb = tm // SUBLANES
    first = vres is None
    vec = lambda a: a.reshape(1, -1)
    consts = [vec(mu_prev), vec(mu_next), _block_diag2(w2[0], w2[1]).astype(BF16), vec(w0),
              _block_diag2(a2[0], a2[1]).astype(BF16), vec(a0), g2.astype(BF16), vec(k_k), vec(k_a), vec(r_k),
              _seg_ones(RW), _chunk_tri(tm)]
    tile = lambda w: pl.BlockSpec((1, tm, w), lambda b, i: (b, i, 0))
    full = lambda a: pl.BlockSpec(a.shape, lambda b, i: (0,) * a.ndim)
    in_specs = [tile(n_rw),
                pl.BlockSpec((1, SUBLANES, n_rw), lambda b, i: (b, jnp.maximum(i * tpb - 1, 0), 0)),
                pl.BlockSpec((1, SUBLANES, n_rw), lambda b, i: (b, jnp.minimum((i + 1) * tpb, Lp // SUBLANES - 1), 0))]
    in_specs += [full(c) for c in consts]
    args = [rw, rw, rw] + consts
    if not first:
        v0, v1, v2 = vres
        v1p = jnp.concatenate([v1, jnp.zeros((RW, LANES - VRES_LORA), F32)], 1).astype(BF16)
        v2p = jnp.concatenate([v2, jnp.zeros((LANES - VRES_LORA, RW), F32)], 0).astype(BF16)
        extra = [vec(v0), v1p, v2p]
        in_specs += [tile(RW)] + [full(c) for c in extra]
        args += [v_first] + extra
    vt_shape = (B, Lp // TIME_BLOCK, H_B // 2, 2, N_B, LANES)
    vt_spec = pl.BlockSpec((1, tm // TIME_BLOCK) + vt_shape[2:], lambda b, i: (b, i, 0, 0, 0, 0))
    shapes = [jax.ShapeDtypeStruct((B, Lp, RW), BF16 if i in (1, 6, 7, 12, 13) else F32) for i in range(16)]
    shapes.append(jax.ShapeDtypeStruct(vt_shape, F32))
    specs = [tile(RW)] * 16 + [vt_spec]
    if first:
        shapes.append(jax.ShapeDtypeStruct((B, Lp, RW), F32))
        specs.append(tile(RW))
    out = pl.pallas_call(
        functools.partial(_prep_kernel, first=first, seq_len=seq_len, tm=tm),
        out_shape=tuple(shapes),
        grid=(B, Lp // tm),
        in_specs=in_specs,
        out_specs=tuple(specs),
        compiler_params=_cparams(("parallel", "parallel")),
        name="rwkv_prep",
    )(*args)
    return out


def _scan_kernel(kkf, wf, kdf, bf, vf, rbf, ktf, btf, vtf, kkb, wb, kdb, bb_, vb_, rbb, ktb, btb, vtb, ones_ref,
                 yf_ref, yb_ref, s_ref, s0_ref, u_ref, *, nb):
    j = pl.program_id(1)
    C = nb * 4
    half_t = TIME_BLOCK // 2
    assert half_t == SCAN_CHUNK

    @pl.when(j == 0)
    def _():
        s_ref[...] = jnp.zeros_like(s_ref)

    lane1 = lax.broadcasted_iota(jnp.int32, (N_B, LANES), 1)
    lo_half = lane1 < N_B

    ones = ones_ref[...]
    lane_id = lax.broadcasted_iota(jnp.int32, (C * N_B, LANES), 1)
    lane = lane_id % N_B
    head_base = (lane_id // N_B) * N_B
    row_refs = ((kkf, wf, kdf, bf), (kkb, wb, kdb, bb_))
    chunk_refs = ((vf, rbf, ktf, btf, yf_ref), (vb_, rbb, ktb, btb, yb_ref))
    vt_refs = (vtf, vtb)
    tr = lax.broadcasted_iota(jnp.int32, (N_B, LANES), 0)
    tc = lane1 % N_B
    earlier2 = (tc <= tr, tc >= tr)
    lanes_dims = ((1,), (1,))

    def step(d, tiles, row, oh, tt):
        def rows(tile):
            return jnp.concatenate(
                [jnp.broadcast_to(tile[n, row:row + 1, hp * LANES:(hp + 1) * LANES], (N_B, LANES))
                 for n in range(nb) for hp in range(4)], axis=0)

        kk, w, kd, b = [rows(x) for x in tiles]
        S = s_ref[d]
        sa = jnp.dot((S * kk).astype(BF16), ones[:LANES], preferred_element_type=F32)
        vt = jnp.concatenate([vt_refs[d][n, 0, hp, oh] for n in range(nb) for hp in range(4)], axis=0)
        vb = jnp.take_along_axis(vt, head_base + tt, axis=1, mode="promise_in_bounds")
        s_ref[d] = S * w - sa * b + vb * kd
        u_ref[d] = jnp.where(lane == tt, sa, u_ref[d])

    def make_body(half):
        def body(g, carry):
            t0f = pl.multiple_of(half * half_t + g * SUBLANES, SUBLANES)
            t0b = pl.multiple_of(TIME_BLOCK - SUBLANES - (half * half_t + g * SUBLANES), SUBLANES)
            tiles_f = [x[:, pl.ds(t0f, SUBLANES), :] for x in row_refs[0]]
            tiles_b = [x[:, pl.ds(t0b, SUBLANES), :] for x in row_refs[1]]
            for i in range(SUBLANES):
                step(0, tiles_f, i, half, g * SUBLANES + i)
                step(1, tiles_b, SUBLANES - 1 - i, 1 - half, half_t - 1 - (g * SUBLANES + i))
            return carry
        return body

    def chunk_outputs(d, oh):
        vref, rref, ktref, btref, yref = chunk_refs[d]
        r0 = oh * half_t
        pairs = [(n, hp) for n in range(nb) for hp in range(4)]

        def blk(ref, n, hp):
            return ref[n, r0:r0 + half_t, hp * LANES:(hp + 1) * LANES]

        def stage1(n, hp):
            c = n * 4 + hp
            rb, kt, bt = blk(rref, n, hp), blk(ktref, n, hp), blk(btref, n, hp)
            s0h = s0_ref[d, c * N_B:(c + 1) * N_B, :].astype(BF16)
            r2 = jnp.concatenate([jnp.where(lo_half, rb, 0.0), jnp.where(lo_half, 0.0, rb)], axis=0)
            r2h, r2l = _split_bf16(r2)
            rhs = jnp.concatenate([kt, bt, s0h, s0h], axis=0)
            return lax.dot_general(jnp.concatenate([r2h, r2l], axis=1), jnp.concatenate([rhs, rhs], axis=1),
                                   (lanes_dims, ((), ())), preferred_element_type=F32)

        def stage2(n, hp, gb):
            c = n * 4 + hp
            vh = blk(vref, n, hp)
            ut = u_ref[d, c * N_B:(c + 1) * N_B, :]
            u2 = jnp.concatenate([ut, ut], axis=0).T
            per_head = []
            for h in range(2):
                gh = gb[h * N_B:(h + 1) * N_B]
                gh_, gl_ = _split_bf16(jnp.where(earlier2[d], gh[:, :LANES], 0.0))
                w = jnp.concatenate([vh, (-u2[h * N_B:(h + 1) * N_B]).astype(BF16)], axis=0)
                corr = jnp.dot(jnp.concatenate([gh_, gl_], axis=1), jnp.concatenate([w, w], axis=0),
                               preferred_element_type=F32)
                per_head.append(gh[:, LANES:] + corr)
            yref[n, r0:r0 + half_t, hp * LANES:(hp + 1) * LANES] = jnp.where(lo_half, per_head[0], per_head[1])

        ahead = 4
        gbs = [stage1(*p) for p in pairs[:ahead]]
        for i, p in enumerate(pairs):
            if i + ahead < len(pairs):
                gbs.append(stage1(*pairs[i + ahead]))
            stage2(*p, gbs[i])

    for half in range(2):
        s0_ref[...] = s_ref[...]
        u_ref[...] = jnp.zeros_like(u_ref)
        lax.fori_loop(0, half_t // SUBLANES, make_body(half), 0)
        chunk_outputs(0, half)
        chunk_outputs(1, 1 - half)


def _wkv_bidir(kk, v, vt, per_dir, nb):
    B, Lp, _ = kk.shape
    nblk = Lp // TIME_BLOCK
    ones = _seg_ones(LANES)
    fwd = lambda bi, j: (bi, j, 0)
    bwd = lambda bi, j: (bi, nblk - 1 - j, 0)
    blk = (nb, TIME_BLOCK, RW)
    vt_blk = (nb, 1) + vt.shape[2:]
    (wf, kdf, bf, rbf, ktf, btf), (wb, kdb, bb_, rbb, ktb, btb) = per_dir
    y_shape = jax.ShapeDtypeStruct((B, Lp, RW), F32)
    rows = nb * 4 * N_B
    return pl.pallas_call(
        functools.partial(_scan_kernel, nb=nb),
        out_shape=(y_shape, y_shape),
        grid=(B // nb, nblk),
        in_specs=[pl.BlockSpec(blk, fwd)] * 8 + [pl.BlockSpec(vt_blk, lambda bi, j: (bi, j, 0, 0, 0, 0))]
                 + [pl.BlockSpec(blk, bwd)] * 8 + [pl.BlockSpec(vt_blk, lambda bi, j: (bi, nblk - 1 - j, 0, 0, 0, 0))]
                 + [pl.BlockSpec(ones.shape, lambda bi, j: (0, 0))],
        out_specs=(pl.BlockSpec(blk, fwd), pl.BlockSpec(blk, bwd)),
        scratch_shapes=[pltpu.VMEM((2, rows, LANES), F32),
                        pltpu.VMEM((2, rows, LANES), F32),
                        pltpu.VMEM((2, rows, LANES), F32)],
        compiler_params=_cparams(("parallel", "arbitrary")),
        name="wkv_scan",
    )(kk, wf, kdf, bf, v, rbf, ktf, btf, vt, kk, wb, kdb, bb_, v, rbb, ktb, btb, vt, ones)


def _post_kernel(yf_ref, yb_ref, g_ref, bvg_ref, att_ref, h_ref, gng_ref, gnb_ref, wo_ref, l1g_ref, l1b_ref,
                 wrh_ref, wrl_ref, rb_ref, tri_ref, ones_ref,
                 h1_ref, rf_ref, ri_ref, cnt_ref, *, alpha, tm):
    ones = ones_ref[...]
    y = yf_ref[...] + yb_ref[...]
    mu = _seg_sum(y, ones) * (1.0 / N_B)
    yc = y - mu
    var = _seg_sum(yc * yc, ones) * (1.0 / N_B)
    yr = (yc * lax.rsqrt(var + RWKV_GN_EPS) * gng_ref[...] + gnb_ref[...]) * g_ref[...] + bvg_ref[...]
    half = H_A * V_DIM
    mixed = (jnp.dot(att_ref[...], wo_ref[:half, :], preferred_element_type=F32)
             + jnp.dot(yr.astype(BF16), wo_ref[half:, :], preferred_element_type=F32))
    h1 = _ln(h_ref[...] * alpha + mixed, l1g_ref[...], l1b_ref[...])
    h1_ref[...] = h1

    xh, xl = _split_bf16(h1)
    x = (jnp.dot(xh, wrh_ref[...], preferred_element_type=F32) + jnp.dot(xl, wrh_ref[...], preferred_element_type=F32)
         + jnp.dot(xh, wrl_ref[...], preferred_element_type=F32)) + rb_ref[...]
    lane = lax.broadcasted_iota(jnp.int32, x.shape, 1)
    lanef = lane.astype(F32)
    big = float(LANES)
    gmask = lane < N_GROUPS
    gx = jnp.where(gmask, x, NEG)
    gmax = jnp.max(gx, -1, keepdims=True)
    gidx = jnp.min(jnp.where(gx == gmax, lanef, big), -1, keepdims=True)
    gsum = jnp.sum(jnp.where(gmask, jnp.exp(gx - gmax), 0.0), -1, keepdims=True)
    lo = N_GROUPS + EXPERTS_PER_GROUP * gidx
    emask = (lanef >= lo) & (lanef < lo + EXPERTS_PER_GROUP)
    ex = jnp.where(emask, x, NEG)
    m1 = jnp.max(ex, -1, keepdims=True)
    i1 = jnp.min(jnp.where(emask & (ex == m1), lanef, big), -1, keepdims=True)
    ex2 = jnp.where(lanef == i1, NEG, ex)
    m2 = jnp.max(ex2, -1, keepdims=True)
    i2 = jnp.min(jnp.where(emask & (lanef != i1) & (ex2 == m2), lanef, big), -1, keepdims=True)
    esum = jnp.sum(jnp.where(emask, jnp.exp(ex - m1), 0.0), -1, keepdims=True)
    gp = 1.0 / gsum
    gate0 = gp * (1.0 / esum)
    gate1 = gp * (jnp.exp(m2 - m1) / esum)
    e0 = i1 - N_GROUPS
    e1 = i2 - N_GROUPS

    @pl.when(pl.program_id(0) == 0)
    def _():
        cnt_ref[...] = jnp.zeros_like(cnt_ref)

    onehot = jnp.where((lanef == e0) | (lanef == e1), 1.0, 0.0)
    prefix = jnp.dot(tri_ref[...], onehot.astype(BF16), preferred_element_type=F32) + cnt_ref[...]
    r0 = jnp.sum(jnp.where(lanef == e0, prefix, 0.0), -1, keepdims=True)
    r1 = jnp.sum(jnp.where(lanef == e1, prefix, 0.0), -1, keepdims=True)
    cnt_ref[...] += jnp.sum(onehot, 0, keepdims=True)
    rf_ref[...] = jnp.where(lane == 0, gate0, jnp.where(lane == 1, gate1, 0.0))
    ri = jnp.where(lane == 0, e0, jnp.where(lane == 1, e1, jnp.where(lane == 2, r0, jnp.where(lane == 3, r1, 0.0))))
    ri_ref[...] = ri.astype(jnp.int32)


def _post_mixer(yf, yb, g, bvg, att, h, gn_g, gn_b, w_out, ln_g, ln_b, w_group, b_group, w_expert, b_expert, alpha):
    T, D = h.shape
    tm = _row_tile(T)
    vec = lambda a: a.reshape(1, -1)
    zpad = LANES - N_GROUPS - N_EXPERTS
    wr = jnp.concatenate([w_group, w_expert, jnp.zeros((D, zpad), F32)], axis=1)
    wrh = wr.astype(BF16)
    wrl = (wr - wrh.astype(F32)).astype(BF16)
    rb = jnp.concatenate([b_group, b_expert, jnp.zeros((zpad,), F32)]).reshape(1, LANES)
    tri = jnp.asarray(np.arange(tm)[:, None] > np.arange(tm)[None, :], BF16)
    consts = [vec(gn_g), vec(gn_b), w_out.astype(BF16), vec(ln_g), vec(ln_b), wrh, wrl, rb, tri, _seg_ones(RW)]
    row = lambda w: pl.BlockSpec((tm, w), lambda i: (i, 0))
    full = lambda a: pl.BlockSpec(a.shape, lambda i: (0, 0))
    return pl.pallas_call(
        functools.partial(_post_kernel, alpha=alpha, tm=tm),
        out_shape=(jax.ShapeDtypeStruct((T, D), F32),
                   jax.ShapeDtypeStruct((T, LANES), F32),
                   jax.ShapeDtypeStruct((T, LANES), jnp.int32),
                   jax.ShapeDtypeStruct((1, LANES), F32)),
        grid=(T // tm,),
        in_specs=[row(RW)] * 4 + [row(H_A * V_DIM), row(D)] + [full(c) for c in consts],
        out_specs=(row(D), row(LANES), row(LANES), pl.BlockSpec((1, LANES), lambda i: (0, 0))),
        compiler_params=_cparams(("arbitrary",)),
        name="post_mixer",
    )(yf, yb, g, bvg, att, h, *consts)


def _row_copies(dest_smem, t, make):
    return [make(s, dest_smem[TOP_K * t + s]) for s in range(TOP_K)]


def _load_dest(dest_hbm, dest_smem, isem, tm):
    load = pltpu.make_async_copy(dest_hbm.at[pl.ds(pl.program_id(0) * tm * TOP_K, tm * TOP_K)], dest_smem, isem)
    load.start()
    load.wait()


def _dispatch_kernel(dest_hbm, h_ref, xz_hbm, xb_hbm, dest_smem, isem, sem, *, tm):
    del xz_hbm
    _load_dest(dest_hbm, dest_smem, isem, tm)

    def copies(t):
        src = h_ref.at[pl.ds(t, 1)]
        return _row_copies(dest_smem, t, lambda s, dst: pltpu.make_async_copy(src, xb_hbm.at[pl.ds(dst, 1)], sem))

    def start(t, c):
        for cp in copies(t):
            cp.start()
        return c

    def wait(t, c):
        for cp in copies(t):
            cp.wait()
        return c

    lax.fori_loop(0, tm, start, 0, unroll=8)
    lax.fori_loop(0, tm, wait, 0, unroll=8)


def _dispatch(dest, h, n_rows):
    T, D = h.shape
    tm = _row_tile(T)
    any_spec = pl.BlockSpec(memory_space=pl.ANY)
    return pl.pallas_call(
        functools.partial(_dispatch_kernel, tm=tm),
        out_shape=jax.ShapeDtypeStruct((n_rows, D), F32),
        grid=(T // tm,),
        in_specs=[any_spec, pl.BlockSpec((tm, D), lambda i: (i, 0)), any_spec],
        out_specs=any_spec,
        scratch_shapes=[pltpu.SMEM((tm * TOP_K,), jnp.int32), pltpu.SemaphoreType.DMA, pltpu.SemaphoreType.DMA],
        input_output_aliases={2: 0},
        compiler_params=_cparams(("arbitrary",)),
        name="moe_dispatch",
    )(dest, h, jnp.zeros((n_rows, D), F32))


def _expert_kernel(be_ref, nu_ref, x_ref, w1_ref, w3_ref, w2_ref, o_ref):
    @pl.when(pl.program_id(0) < nu_ref[0])
    def _():
        x = x_ref[...].astype(BF16)
        a = jnp.dot(x, w1_ref[0], preferred_element_type=F32)
        b = jnp.dot(x, w3_ref[0], preferred_element_type=F32)
        hid = (a * jax.nn.sigmoid(a)) * b
        o_ref[...] = jnp.dot(hid.astype(BF16), w2_ref[0], preferred_element_type=F32)


def _expert_ffn(xb, block_e, n_used, w1, w3, w2):
    n_rows, D = xb.shape
    n_blocks = n_rows // MOE_BLOCK
    blk = lambda i, be, nu: jnp.minimum(i, nu[0] - 1)
    wmap = lambda i, be, nu: (be[blk(i, be, nu)], 0, 0)
    grid_spec = pltpu.PrefetchScalarGridSpec(
        num_scalar_prefetch=2,
        grid=(n_blocks,),
        in_specs=[pl.BlockSpec((MOE_BLOCK, D), lambda i, be, nu: (blk(i, be, nu), 0)),
                  pl.BlockSpec((1, D, E_HID), wmap),
                  pl.BlockSpec((1, D, E_HID), wmap),
                  pl.BlockSpec((1, E_HID, D), wmap)],
        out_specs=pl.BlockSpec((MOE_BLOCK, D), lambda i, be, nu: (blk(i, be, nu), 0)),
    )
    return pl.pallas_call(
        _expert_kernel,
        out_shape=jax.ShapeDtypeStruct((n_rows, D), F32),
        grid_spec=grid_spec,
        compiler_params=_cparams(("arbitrary",)),
        name="moe_experts",
    )(block_e, n_used, xb, w1, w3, w2)


def _combine_kernel(dest_hbm, yb_hbm, gate_ref, h_ref, g_ref, b_ref, o_ref, dest_smem, buf, isem, sem, *, tm, alpha):
    _load_dest(dest_hbm, dest_smem, isem, tm)

    def copies(t):
        return _row_copies(dest_smem, t, lambda s, src: pltpu.make_async_copy(
            yb_hbm.at[pl.ds(src, 1)], buf.at[s, pl.ds(t, 1)], sem))

    def start(t, c):
        for cp in copies(t):
            cp.start()
        return c

    def wait(t, c):
        for cp in copies(t):
            cp.wait()
        return c

    lax.fori_loop(0, tm, start, 0, unroll=8)
    lax.fori_loop(0, tm, wait, 0, unroll=8)
    gate = gate_ref[...]
    ff = sum(buf[s] * gate[:, s:s + 1] for s in range(TOP_K))
    o_ref[...] = _ln(h_ref[...] * alpha + ff, g_ref[...], b_ref[...])


def _combine(dest, yb, route_f, h, ln_g, ln_b, alpha):
    T, D = h.shape
    tm = _row_tile(T)
    any_spec = pl.BlockSpec(memory_space=pl.ANY)
    row = lambda w: pl.BlockSpec((tm, w), lambda i: (i, 0))
    vec = pl.BlockSpec((1, D), lambda i: (0, 0))
    return pl.pallas_call(
        functools.partial(_combine_kernel, tm=tm, alpha=alpha),
        out_shape=jax.ShapeDtypeStruct((T, D), F32),
        grid=(T // tm,),
        in_specs=[any_spec, any_spec, row(LANES), row(D), vec, vec],
        out_specs=row(D),
        scratch_shapes=[pltpu.SMEM((tm * TOP_K,), jnp.int32), pltpu.VMEM((TOP_K, tm, D), F32),
                        pltpu.SemaphoreType.DMA, pltpu.SemaphoreType.DMA],
        compiler_params=_cparams(("arbitrary",)),
        name="moe_combine",
    )(dest, yb, route_f, h, ln_g.reshape(1, D), ln_b.reshape(1, D))


def _hier_moe(h, route_f, route_i, counts, w1, w3, w2, ln_g, ln_b, alpha):
    T, D = h.shape
    counts = counts[0, :N_EXPERTS].astype(jnp.int32)
    padded = (counts + MOE_BLOCK - 1) // MOE_BLOCK * MOE_BLOCK
    pends = jnp.cumsum(padded)
    pstart = pends - padded
    n_rows = (T * TOP_K + N_EXPERTS * (MOE_BLOCK - 1) + MOE_BLOCK - 1) // MOE_BLOCK * MOE_BLOCK
    n_blocks = n_rows // MOE_BLOCK
    block_start = jnp.arange(n_blocks, dtype=jnp.int32) * MOE_BLOCK
    block_e = jnp.minimum(jnp.sum(pends[None, :] <= block_start[:, None], axis=1), N_EXPERTS - 1).astype(jnp.int32)
    n_used = (pends[-1:] // MOE_BLOCK).astype(jnp.int32)
    dest = (jnp.take(pstart, route_i[:, :TOP_K]) + route_i[:, TOP_K:2 * TOP_K]).reshape(-1)
    xb = _dispatch(dest, h, n_rows)
    yb = _expert_ffn(xb, block_e, n_used, w1, w3, w2)
    return _combine(dest, yb, route_f, h, ln_g, ln_b, alpha)


def kernel(x, positions, meta_tokens, emb_ln_g, emb_ln_b, w_in, mla_q_norm, mla_kv_norm, mla_w_uq, mla_w_ukv, mla_out_norm, rwkv_mu_prev, rwkv_mu_next, rwkv_w0, rwkv_w2, rwkv_a0, rwkv_a2, rwkv_g2, rwkv_k_k, rwkv_k_a, rwkv_r_k, rwkv_gn_g, rwkv_gn_b, rwkv_v0, rwkv_v1, rwkv_v2, w_out, ln1_g, ln1_b, moe_w_group, moe_b_group, moe_w_expert, moe_b_expert, moe_w1, moe_w3, moe_w2, ln2_g, ln2_b):
    B, seq, D = x.shape
    depth = w_in.shape[0]
    alpha = (2 * depth) ** 0.25
    L = seq + N_META
    Lp = -(-L // LANES) * LANES
    T = B * Lp
    nb = next(n for n in (4, 2, 1) if B % n == 0)
    tk = 384 if Lp % 384 == 0 else LANES
    tq = next(t for t in (1408, 1152, 768, 384, LANES) if Lp % t == 0)

    meta = jnp.broadcast_to(meta_tokens.astype(x.dtype)[None], (B, N_META, D))
    h = jnp.concatenate([meta, x, jnp.zeros((B, Lp - L, D), x.dtype)], axis=1).reshape(T, D)
    h = _ln_residual(h, jnp.zeros_like(h), emb_ln_g, emb_ln_b, 1.0)

    key_bias = jnp.where(jnp.arange(Lp) < L, 0.0, NEG).astype(F32)[None, :]
    pos = jnp.concatenate([jnp.broadcast_to(jnp.arange(N_META, dtype=jnp.int32), (B, N_META)),
                           positions + N_META, jnp.zeros((B, Lp - L), jnp.int32)], axis=1)
    inv_freq = ROPE_THETA ** (-jnp.arange(0, QK_ROPE, 2, dtype=F32) / QK_ROPE)
    ang = pos.astype(F32)[..., None] * inv_freq
    cos, sin = jnp.cos(ang).reshape(T, -1), jnp.sin(ang).reshape(T, -1)
    zeros = jnp.zeros((T, LANES - QK_ROPE), F32)
    rope_cos = jnp.concatenate([cos, cos, zeros], axis=1)
    rope_sin = jnp.concatenate([-sin, sin, zeros], axis=1)
    qscale = (QK_NOPE + QK_ROPE) ** -0.5 * math.log2(math.e)

    v_first = None
    for li in range(depth):
        q, k, v, rw = _project(h, w_in[li], mla_q_norm[li], mla_kv_norm[li], mla_w_uq[li], mla_w_ukv[li],
                               rope_cos, rope_sin, qscale)
        y_att = _attention(q.reshape(B, Lp, -1), k.reshape(B, Lp, -1), v.reshape(B, Lp, -1),
                           key_bias, mla_out_norm[li], tq, tk)
        vres = None if li == 0 else (rwkv_v0[li - 1], rwkv_v1[li - 1], rwkv_v2[li - 1])
        outs = _rwkv_prep(rw.reshape(B, Lp, -1), L, rwkv_mu_prev[li], rwkv_mu_next[li], rwkv_w0[li], rwkv_w2[li],
                          rwkv_a0[li], rwkv_a2[li], rwkv_g2[li], rwkv_k_k[li], rwkv_k_a[li], rwkv_r_k[li],
                          v_first, vres)
        kk, vm, g, bvg, vt = outs[0], outs[1], outs[14], outs[15], outs[16]
        if li == 0:
            v_first = outs[17]
        y_f, y_b = _wkv_bidir(kk, vm, vt, (outs[2:8], outs[8:14]), nb)
        flat = lambda a: a.reshape(T, -1)
        h, route_f, route_i, counts = _post_mixer(
            flat(y_f), flat(y_b), flat(g), flat(bvg), flat(y_att), h, rwkv_gn_g[li], rwkv_gn_b[li], w_out[li],
            ln1_g[li], ln1_b[li], moe_w_group[li], moe_b_group[li], moe_w_expert[li], moe_b_expert[li], alpha)
        h = _hier_moe(h, route_f, route_i, counts,
                      moe_w1[li].astype(BF16), moe_w3[li].astype(BF16), moe_w2[li].astype(BF16),
                      ln2_g[li], ln2_b[li], alpha)
    return h.reshape(B, Lp, D)[:, N_META:L]
```

```python
import functools
import math

import numpy as np
import jax
import jax.numpy as jnp
from jax import lax
from jax.experimental import pallas as pl
from jax.experimental.pallas import tpu as pltpu

F32 = jnp.float32
BF16 = jnp.bfloat16

N_META = 16
H_A = 4
QK_NOPE = 128
QK_ROPE = 64
V_DIM = 128
Q_LORA = 256
KV_LORA = 128
ROPE_THETA = 10000.0
H_B = 8
N_B = 64
RW = H_B * N_B
DECAY_LORA = 64
AAA_LORA = 64
GATE_LORA = 128
VRES_LORA = 32
RWKV_GN_EPS = 64e-5
MLA_COLS = Q_LORA + KV_LORA + QK_ROPE
N_GROUPS = 4
EXPERTS_PER_GROUP = 8
N_EXPERTS = N_GROUPS * EXPERTS_PER_GROUP
TOP_K = 2
E_HID = 256
MOE_BLOCK = 256
LN_EPS = 1e-5
RMS_EPS = 1e-6

LANES = 128
SUBLANES = 8
TIME_BLOCK = 128
SCAN_CHUNK = 64
GROUPS_PER_ITER = 2
ROW_TILE = 512
VMEM_LIMIT = 56 * 1024 * 1024
HEAD_W = 2 * LANES
NEG = -1e30


def _cparams(sem):
    return pltpu.CompilerParams(dimension_semantics=sem, vmem_limit_bytes=VMEM_LIMIT)


def _row_tile(m):
    return next(t for t in (ROW_TILE, 384, 256, LANES) if m % t == 0)


def _split_bf16(x):
    hi = x.astype(BF16)
    lo = (x - hi.astype(F32)).astype(BF16)
    return hi, lo


def _seg_sum(x, ones):
    hi, lo = _split_bf16(x)
    return jnp.dot(jnp.concatenate([hi, lo], axis=-1), ones, preferred_element_type=F32)


def _seg_ones(width):
    m = np.arange(2 * width)[:, None] % width
    n = np.arange(width)[None, :]
    return jnp.asarray(m // N_B == n // N_B, BF16)


def _ln(x, g, b):
    mu = jnp.mean(x, -1, keepdims=True)
    xc = x - mu
    var = jnp.mean(xc * xc, -1, keepdims=True)
    return xc * lax.rsqrt(var + LN_EPS) * g + b


def _ln_kernel(x_ref, r_ref, g_ref, b_ref, o_ref, *, alpha):
    o_ref[...] = _ln(x_ref[...] * alpha + r_ref[...], g_ref[...], b_ref[...])


def _ln_residual(x, r, g, b, alpha):
    M, D = x.shape
    tm = _row_tile(M)
    row = pl.BlockSpec((tm, D), lambda i: (i, 0))
    vec = pl.BlockSpec((1, D), lambda i: (0, 0))
    return pl.pallas_call(
        functools.partial(_ln_kernel, alpha=alpha),
        out_shape=jax.ShapeDtypeStruct((M, D), F32),
        grid=(M // tm,),
        in_specs=[row, row, vec, vec],
        out_specs=row,
        compiler_params=_cparams(("parallel",)),
        name="layer_norm",
    )(x, r, g.reshape(1, D), b.reshape(1, D))


def _rms(x, g):
    return x * lax.rsqrt(jnp.mean(x * x, -1, keepdims=True) + RMS_EPS) * g


def _proj_kernel(h_ref, win_ref, qn_ref, kvn_ref, wuq_ref, wukv_ref, c_ref, s_ref,
                 q_ref, k_ref, v_ref, rw_ref, *, qscale):
    proj = jnp.dot(h_ref[...].astype(BF16), win_ref[...], preferred_element_type=F32)
    rw_ref[...] = proj[:, MLA_COLS + N_B:]
    q = jnp.dot(_rms(proj[:, :Q_LORA], qn_ref[...]).astype(BF16), wuq_ref[...], preferred_element_type=F32)
    kv = jnp.dot(_rms(proj[:, Q_LORA:Q_LORA + KV_LORA], kvn_ref[...]).astype(BF16), wukv_ref[...],
                 preferred_element_type=F32)
    cos, sin = c_ref[...], s_ref[...]
    lane = lax.broadcasted_iota(jnp.int32, cos.shape, 1)
    half = QK_ROPE // 2

    def rope(x):
        partner = jnp.where(lane < half, pltpu.roll(x, LANES - half, 1), pltpu.roll(x, half, 1))
        return x * cos + partner * sin

    k_pe = rope(proj[:, Q_LORA + KV_LORA:Q_LORA + KV_LORA + LANES]).astype(BF16)
    for h in range(H_A):
        o = h * HEAD_W
        q_ref[:, o:o + LANES] = (q[:, o:o + LANES] * qscale).astype(BF16)
        q_ref[:, o + LANES:o + HEAD_W] = (rope(q[:, o + LANES:o + HEAD_W]) * qscale).astype(BF16)
        k_ref[:, o:o + LANES] = kv[:, o:o + LANES].astype(BF16)
        k_ref[:, o + LANES:o + HEAD_W] = k_pe
        v_ref[:, h * V_DIM:(h + 1) * V_DIM] = kv[:, o + LANES:o + HEAD_W].astype(BF16)


def _project(h, w_in, q_norm, kv_norm, w_uq, w_ukv, rope_cos, rope_sin, qscale):
    T, D = h.shape
    tm = _row_tile(T)
    n_rw = w_in.shape[1] - MLA_COLS
    pad = jnp.zeros((D, N_B), F32)
    win = jnp.concatenate([w_in[:, :MLA_COLS], pad, w_in[:, MLA_COLS:]], axis=1).astype(BF16)
    wq = w_uq.reshape(Q_LORA, H_A, QK_NOPE + QK_ROPE)
    wq = jnp.concatenate([wq, jnp.zeros((Q_LORA, H_A, HEAD_W - QK_NOPE - QK_ROPE), F32)], axis=-1)
    wq = wq.reshape(Q_LORA, H_A * HEAD_W).astype(BF16)
    row = lambda w: pl.BlockSpec((tm, w), lambda i: (i, 0))
    full = lambda a: pl.BlockSpec(a.shape, lambda i: (0, 0))
    args = (h, win, q_norm.reshape(1, -1), kv_norm.reshape(1, -1), wq, w_ukv.astype(BF16), rope_cos, rope_sin)
    return pl.pallas_call(
        functools.partial(_proj_kernel, qscale=qscale),
        out_shape=(jax.ShapeDtypeStruct((T, H_A * HEAD_W), BF16),
                   jax.ShapeDtypeStruct((T, H_A * HEAD_W), BF16),
                   jax.ShapeDtypeStruct((T, H_A * V_DIM), BF16),
                   jax.ShapeDtypeStruct((T, n_rw), F32)),
        grid=(T // tm,),
        in_specs=[row(D)] + [full(a) for a in args[1:6]] + [row(LANES), row(LANES)],
        out_specs=(row(H_A * HEAD_W), row(H_A * HEAD_W), row(H_A * V_DIM), row(n_rw)),
        compiler_params=_cparams(("parallel",)),
        name="in_proj",
    )(*args)


def _attn_kernel(q_ref, k_ref, v_ref, bias_ref, g_ref, o_ref, m_ref, acc_ref, s_ref, p_ref, a_ref, *, tk):
    q = q_ref[0]
    n = k_ref.shape[1] // tk
    one_col = (lax.broadcasted_iota(jnp.int32, (tk, V_DIM), 1) == 0).astype(BF16)

    def scores(c):
        off = pl.multiple_of(c * tk, tk)
        s = lax.dot_general(q, k_ref[0, pl.ds(off, tk), :], (((1,), (1,)), ((), ())), preferred_element_type=F32)
        return s + bias_ref[:, pl.ds(off, tk)]

    def weighted_values(c):
        off = pl.multiple_of(c * tk, tk)
        v1 = jnp.concatenate([v_ref[0, pl.ds(off, tk), :], one_col], axis=1)
        acc_ref[...] = a_ref[...] * acc_ref[...] + jnp.dot(p_ref[...], v1, preferred_element_type=F32)

    m_ref[...] = jnp.full_like(m_ref, NEG)
    acc_ref[...] = jnp.zeros_like(acc_ref)
    p_ref[...] = jnp.zeros_like(p_ref)
    a_ref[...] = jnp.ones_like(a_ref)
    s_ref[...] = scores(0)

    def body(c, carry):
        weighted_values(jnp.maximum(c - 1, 0))
        s = s_ref[...]
        s_ref[...] = scores(jnp.minimum(c + 1, n - 1))
        m_prev = m_ref[...]
        m_new = jnp.maximum(m_prev, jnp.max(s, -1, keepdims=True))
        p_ref[...] = jnp.exp2(s - m_new).astype(BF16)
        a_ref[...] = jnp.exp2(m_prev - m_new)
        m_ref[...] = m_new
        return carry

    lax.fori_loop(0, n, body, 0)
    weighted_values(n - 1)
    acc = acc_ref[...]
    o = acc[:, :V_DIM] / acc[:, V_DIM:V_DIM + 1]
    o = o * lax.rsqrt(jnp.mean(o * o, -1, keepdims=True) + RMS_EPS) * g_ref[...]
    o_ref[0] = o.astype(BF16)


def _attention(q, k, v, bias, out_gain, tq, tk):
    B, Lp, _ = q.shape
    return pl.pallas_call(
        functools.partial(_attn_kernel, tk=tk),
        out_shape=jax.ShapeDtypeStruct((B, Lp, H_A * V_DIM), BF16),
        grid=(B, H_A, Lp // tq),
        in_specs=[pl.BlockSpec((1, tq, HEAD_W), lambda b, h, i: (b, i, h)),
                  pl.BlockSpec((1, Lp, HEAD_W), lambda b, h, i: (b, 0, h)),
                  pl.BlockSpec((1, Lp, V_DIM), lambda b, h, i: (b, 0, h)),
                  pl.BlockSpec((1, Lp), lambda b, h, i: (0, 0)),
                  pl.BlockSpec((1, V_DIM), lambda b, h, i: (0, h))],
        out_specs=pl.BlockSpec((1, tq, V_DIM), lambda b, h, i: (b, i, h)),
        scratch_shapes=[pltpu.VMEM((tq, 1), F32), pltpu.VMEM((tq, 2 * V_DIM), F32),
                        pltpu.VMEM((tq, tk), F32), pltpu.VMEM((tq, tk), BF16), pltpu.VMEM((tq, 1), F32)],
        compiler_params=_cparams(("parallel", "parallel", "parallel")),
        name="mla_attention",
    )(q, k, v, bias, out_gain.reshape(1, H_A * V_DIM))


def _prep_kernel(*refs, first, seq_len, tm):
    if first:
        (rw_ref, pv_ref, nx_ref, mup_ref, mun_ref, w2_ref, w0_ref, a2_ref, a0_ref, g2_ref, kk_ref, ka_ref, rk_ref,
         ones_ref, tri_ref, kk_o, v_o, wf_o, kdf_o, bf_o, rbf_o, ktf_o, btf_o, wb_o, kdb_o, bb_o, rbb_o, ktb_o, btb_o,
         g_o, bvg_o, vt_o, vfirst_o) = refs
    else:
        (rw_ref, pv_ref, nx_ref, mup_ref, mun_ref, w2_ref, w0_ref, a2_ref, a0_ref, g2_ref, kk_ref, ka_ref, rk_ref,
         ones_ref, tri_ref, vf_ref, v0_ref, v1_ref, v2_ref,
         kk_o, v_o, wf_o, kdf_o, bf_o, rbf_o, ktf_o, btf_o, wb_o, kdb_o, bb_o, rbb_o, ktb_o, btb_o, g_o, bvg_o,
         vt_o) = refs
    i = pl.program_id(1)
    n_t = pl.num_programs(1)
    row = lax.broadcasted_iota(jnp.int32, (tm, 1), 0)
    t = i * tm + row
    valid = t < seq_len
    rw = jnp.where(valid, rw_ref[0], 0.0)
    prev_row = jnp.where((i > 0) & (i * tm - 1 < seq_len), pv_ref[0, SUBLANES - 1:SUBLANES, :], 0.0)
    next_row = jnp.where((i < n_t - 1) & ((i + 1) * tm < seq_len), nx_ref[0, 0:1, :], 0.0)
    prev = jnp.where(row == 0, prev_row, pltpu.roll(rw, 1, 0))
    nxt = jnp.where(row == tm - 1, next_row, pltpu.roll(rw, tm - 1, 0))
    u = rw + mup_ref[...] * (prev - rw) + mun_ref[...] * (nxt - rw)
    r, k, v = u[:, :RW], u[:, RW:2 * RW], u[:, 2 * RW:3 * RW]
    wd = u[:, 3 * RW:3 * RW + LANES]
    ad = u[:, 3 * RW + LANES:3 * RW + 2 * LANES]
    gd = u[:, 3 * RW + 2 * LANES:]
    ones = ones_ref[...]
    if first:
        vfirst_o[0] = v
    else:
        low = jnp.dot(v.astype(BF16), v1_ref[...], preferred_element_type=F32)
        mix = jax.nn.sigmoid(v0_ref[...] + jnp.dot(low.astype(BF16), v2_ref[...], preferred_element_type=F32))
        v = v + (vf_ref[0] - v) * mix
    g = jnp.dot(jax.nn.sigmoid(gd).astype(BF16), g2_ref[...], preferred_element_type=F32)
    kk = k * kk_ref[...]
    kk = jnp.where(valid, kk * lax.rsqrt(_seg_sum(kk * kk, ones) + 1e-12), 0.0)
    wl = w0_ref[...] + jnp.dot(jnp.tanh(wd).astype(BF16), w2_ref[...], preferred_element_type=F32)
    logw = -math.exp(-0.5) * jax.nn.sigmoid(wl)
    decay = jnp.exp(logw)
    a = jax.nn.sigmoid(a0_ref[...] + jnp.dot(ad.astype(BF16), a2_ref[...], preferred_element_type=F32))
    ka = ka_ref[...]
    kd_f = jnp.where(valid, k * (1.0 + (a[:, :RW] - 1.0) * ka), 0.0)
    kd_b = jnp.where(valid, k * (1.0 + (a[:, RW:] - 1.0) * ka), 0.0)
    bonus = _seg_sum(r * (kd_f + kd_b) * rk_ref[...], ones)
    b_f, b_b = kk * a[:, :RW], kk * a[:, RW:]
    cum_f = _cumsum3(tri_ref[0], logw[:, :RW])
    cum_b = _cumsum3(tri_ref[1], logw[:, RW:])
    p_f, ip_f = jnp.exp(cum_f), jnp.exp(-cum_f)
    p_b, ip_b = jnp.exp(cum_b), jnp.exp(-cum_b)
    kk_o[0] = kk
    vm = jnp.where(valid, v, 0.0)
    v_o[0] = vm.astype(BF16)
    lo_half = lax.broadcasted_iota(jnp.int32, (N_B, LANES), 1) < N_B
    for tb in range(tm // TIME_BLOCK):
        for hp in range(H_B // 2):
            xt = vm[tb * TIME_BLOCK:(tb + 1) * TIME_BLOCK, hp * LANES:(hp + 1) * LANES].T
            top, bot = xt[:N_B], xt[N_B:]
            vt_o[0, tb, hp, 0] = jnp.where(lo_half, top, pltpu.roll(bot, N_B, 1))
            vt_o[0, tb, hp, 1] = jnp.where(lo_half, pltpu.roll(top, N_B, 1), bot)
    wf_o[0] = decay[:, :RW]
    wb_o[0] = decay[:, RW:]
    kdf_o[0] = kd_f
    kdb_o[0] = kd_b
    bf_o[0] = b_f
    bb_o[0] = b_b
    rbf_o[0] = r * p_f
    rbb_o[0] = r * p_b
    ktf_o[0] = (kd_f * ip_f).astype(BF16)
    ktb_o[0] = (kd_b * ip_b).astype(BF16)
    btf_o[0] = (b_f * ip_f).astype(BF16)
    btb_o[0] = (b_b * ip_b).astype(BF16)
    g_o[0] = g
    bvg_o[0] = bonus * v * g


def _cumsum3(tri, x):
    h1 = x.astype(BF16)
    r1 = x - h1.astype(F32)
    h2 = r1.astype(BF16)
    h3 = (r1 - h2.astype(F32)).astype(BF16)
    return (jnp.dot(tri, h1, preferred_element_type=F32) + jnp.dot(tri, h2, preferred_element_type=F32)
            + jnp.dot(tri, h3, preferred_element_type=F32))


def _chunk_tri(tm):
    t = np.arange(tm)
    same = (t[:, None] // SCAN_CHUNK) == (t[None, :] // SCAN_CHUNK)
    return jnp.asarray(np.stack([same & (t[None, :] <= t[:, None]), same & (t[None, :] >= t[:, None])]), BF16)


def _block_diag2(a, b):
    z = jnp.zeros_like(a)
    return jnp.concatenate([jnp.concatenate([a, z], 1), jnp.concatenate([z, b], 1)], 0)


def _rwkv_prep(rw, seq_len, mu_prev, mu_next, w0, w2, a0, a2, g2, k_k, k_a, r_k, v_first, vres):
    B, Lp, n_rw = rw.shape
    tm = _row_tile(Lp)
    tpb = tm // SUBLANES
    first = vres is None
    vec = lambda a: a.reshape(1, -1)
    consts = [vec(mu_prev), vec(mu_next), _block_diag2(w2[0], w2[1]).astype(BF16), vec(w0),
              _block_diag2(a2[0], a2[1]).astype(BF16), vec(a0), g2.astype(BF16), vec(k_k), vec(k_a), vec(r_k),
              _seg_ones(RW), _chunk_tri(tm)]
    tile = lambda w: pl.BlockSpec((1, tm, w), lambda b, i: (b, i, 0))
    full = lambda a: pl.BlockSpec(a.shape, lambda b, i: (0,) * a.ndim)
    in_specs = [tile(n_rw),
                pl.BlockSpec((1, SUBLANES, n_rw), lambda b, i: (b, jnp.maximum(i * tpb - 1, 0), 0)),
                pl.BlockSpec((1, SUBLANES, n_rw), lambda b, i: (b, jnp.minimum((i + 1) * tpb, Lp // SUBLANES - 1), 0))]
    in_specs += [full(c) for c in consts]
    args = [rw, rw, rw] + consts
    if not first:
        v0, v1, v2 = vres
        v1p = jnp.concatenate([v1, jnp.zeros((RW, LANES - VRES_LORA), F32)], 1).astype(BF16)
        v2p = jnp.concatenate([v2, jnp.zeros((LANES - VRES_LORA, RW), F32)], 0).astype(BF16)
        extra = [vec(v0), v1p, v2p]
        in_specs += [tile(RW)] + [full(c) for c in extra]
        args += [v_first] + extra
    vt_shape = (B, Lp // TIME_BLOCK, H_B // 2, 2, N_B, LANES)
    vt_spec = pl.BlockSpec((1, tm // TIME_BLOCK) + vt_shape[2:], lambda b, i: (b, i, 0, 0, 0, 0))
    shapes = [jax.ShapeDtypeStruct((B, Lp, RW), BF16 if i in (1, 6, 7, 12, 13) else F32) for i in range(16)]
    shapes.append(jax.ShapeDtypeStruct(vt_shape, F32))
    specs = [tile(RW)] * 16 + [vt_spec]
    if first:
        shapes.append(jax.ShapeDtypeStruct((B, Lp, RW), F32))
        specs.append(tile(RW))
    out = pl.pallas_call(
        functools.partial(_prep_kernel, first=first, seq_len=seq_len, tm=tm),
        out_shape=tuple(shapes),
        grid=(B, Lp // tm),
        in_specs=in_specs,
        out_specs=tuple(specs),
        compiler_params=_cparams(("parallel", "parallel")),
        name="rwkv_prep",
    )(*args)
    return out


def _scan_kernel(kkf, wf, kdf, bf, vf, rbf, ktf, btf, vtf, kkb, wb, kdb, bb_, vb_, rbb, ktb, btb, vtb, ones_ref,
                 yf_ref, yb_ref, s_ref, s0_ref, u_ref, *, nb):
    j = pl.program_id(1)
    C = nb * 4
    half_t = TIME_BLOCK // 2
    assert half_t == SCAN_CHUNK

    @pl.when(j == 0)
    def _():
        s_ref[...] = jnp.zeros_like(s_ref)

    lane1 = lax.broadcasted_iota(jnp.int32, (N_B, LANES), 1)
    lo_half = lane1 < N_B

    ones = ones_ref[...]
    lane_id = lax.broadcasted_iota(jnp.int32, (C * N_B, LANES), 1)
    lane = lane_id % N_B
    head_base = (lane_id // N_B) * N_B
    row_refs = ((kkf, wf, kdf, bf), (kkb, wb, kdb, bb_))
    chunk_refs = ((vf, rbf, ktf, btf, yf_ref), (vb_, rbb, ktb, btb, yb_ref))
    vt_refs = (vtf, vtb)
    tr = lax.broadcasted_iota(jnp.int32, (N_B, LANES), 0)
    tc = lane1 % N_B
    earlier2 = (tc <= tr, tc >= tr)
    lanes_dims = ((1,), (1,))

    def step(d, tiles, row, oh, tt):
        def rows(tile):
            return jnp.concatenate(
                [jnp.broadcast_to(tile[n, row:row + 1, hp * LANES:(hp + 1) * LANES], (N_B, LANES))
                 for n in range(nb) for hp in range(4)], axis=0)

        kk, w, kd, b = [rows(x) for x in tiles]
        S = s_ref[d]
        sa = jnp.dot((S * kk).astype(BF16), ones[:LANES], preferred_element_type=F32)
        vt = jnp.concatenate([vt_refs[d][n, 0, hp, oh] for n in range(nb) for hp in range(4)], axis=0)
        vb = jnp.take_along_axis(vt, head_base + tt, axis=1, mode="promise_in_bounds")
        s_ref[d] = S * w - sa * b + vb * kd
        u_ref[d] = jnp.where(lane == tt, sa, u_ref[d])

    def make_body(half):
        def body(g2, carry):
            for sub in range(GROUPS_PER_ITER):
                g = g2 * GROUPS_PER_ITER + sub
                t0f = pl.multiple_of(half * half_t + g * SUBLANES, SUBLANES)
                t0b = pl.multiple_of(TIME_BLOCK - SUBLANES - (half * half_t + g * SUBLANES), SUBLANES)
                tiles_f = [x[:, pl.ds(t0f, SUBLANES), :] for x in row_refs[0]]
                tiles_b = [x[:, pl.ds(t0b, SUBLANES), :] for x in row_refs[1]]
                for i in range(SUBLANES):
                    step(0, tiles_f, i, half, g * SUBLANES + i)
                    step(1, tiles_b, SUBLANES - 1 - i, 1 - half, half_t - 1 - (g * SUBLANES + i))
            return carry
        return body

    def chunk_outputs(d, oh):
        vref, rref, ktref, btref, yref = chunk_refs[d]
        r0 = oh * half_t
        pairs = [(n, hp) for n in range(nb) for hp in range(4)]

        def blk(ref, n, hp):
            return ref[n, r0:r0 + half_t, hp * LANES:(hp + 1) * LANES]

        def stage1(n, hp):
            c = n * 4 + hp
            rb, kt, bt = blk(rref, n, hp), blk(ktref, n, hp), blk(btref, n, hp)
            s0h = s0_ref[d, c * N_B:(c + 1) * N_B, :].astype(BF16)
            r2 = jnp.concatenate([jnp.where(lo_half, rb, 0.0), jnp.where(lo_half, 0.0, rb)], axis=0)
            r2h, r2l = _split_bf16(r2)
            rhs = jnp.concatenate([kt, bt, s0h, s0h], axis=0)
            return lax.dot_general(jnp.concatenate([r2h, r2l], axis=1), jnp.concatenate([rhs, rhs], axis=1),
                                   (lanes_dims, ((), ())), preferred_element_type=F32)

        def stage2(n, hp, gb):
            c = n * 4 + hp
            vh = blk(vref, n, hp)
            ut = u_ref[d, c * N_B:(c + 1) * N_B, :]
            u2 = jnp.concatenate([ut, ut], axis=0).T
            per_head = []
            for h in range(2):
                gh = gb[h * N_B:(h + 1) * N_B]
                gh_, gl_ = _split_bf16(jnp.where(earlier2[d], gh[:, :LANES], 0.0))
                w = jnp.concatenate([vh, (-u2[h * N_B:(h + 1) * N_B]).astype(BF16)], axis=0)
                corr = jnp.dot(jnp.concatenate([gh_, gl_], axis=1), jnp.concatenate([w, w], axis=0),
                               preferred_element_type=F32)
                per_head.append(gh[:, LANES:] + corr)
            yref[n, r0:r0 + half_t, hp * LANES:(hp + 1) * LANES] = jnp.where(lo_half, per_head[0], per_head[1])

        ahead = 6
        gbs = [stage1(*p) for p in pairs[:ahead]]
        for i, p in enumerate(pairs):
            if i + ahead < len(pairs):
                gbs.append(stage1(*pairs[i + ahead]))
            stage2(*p, gbs[i])

    for half in range(2):
        s0_ref[...] = s_ref[...]
        u_ref[...] = jnp.zeros_like(u_ref)
        lax.fori_loop(0, half_t // (SUBLANES * GROUPS_PER_ITER), make_body(half), 0)
        chunk_outputs(0, half)
        chunk_outputs(1, 1 - half)


def _wkv_bidir(kk, v, vt, per_dir, nb):
    B, Lp, _ = kk.shape
    nblk = Lp // TIME_BLOCK
    ones = _seg_ones(LANES)
    fwd = lambda bi, j: (bi, j, 0)
    bwd = lambda bi, j: (bi, nblk - 1 - j, 0)
    blk = (nb, TIME_BLOCK, RW)
    vt_blk = (nb, 1) + vt.shape[2:]
    (wf, kdf, bf, rbf, ktf, btf), (wb, kdb, bb_, rbb, ktb, btb) = per_dir
    y_shape = jax.ShapeDtypeStruct((B, Lp, RW), F32)
    rows = nb * 4 * N_B
    return pl.pallas_call(
        functools.partial(_scan_kernel, nb=nb),
        out_shape=(y_shape, y_shape),
        grid=(B // nb, nblk),
        in_specs=[pl.BlockSpec(blk, fwd)] * 8 + [pl.BlockSpec(vt_blk, lambda bi, j: (bi, j, 0, 0, 0, 0))]
                 + [pl.BlockSpec(blk, bwd)] * 8 + [pl.BlockSpec(vt_blk, lambda bi, j: (bi, nblk - 1 - j, 0, 0, 0, 0))]
                 + [pl.BlockSpec(ones.shape, lambda bi, j: (0, 0))],
        out_specs=(pl.BlockSpec(blk, fwd), pl.BlockSpec(blk, bwd)),
        scratch_shapes=[pltpu.VMEM((2, rows, LANES), F32),
                        pltpu.VMEM((2, rows, LANES), F32),
                        pltpu.VMEM((2, rows, LANES), F32)],
        compiler_params=_cparams(("parallel", "arbitrary")),
        name="wkv_scan",
    )(kk, wf, kdf, bf, v, rbf, ktf, btf, vt, kk, wb, kdb, bb_, v, rbb, ktb, btb, vt, ones)


def _post_kernel(yf_ref, yb_ref, g_ref, bvg_ref, att_ref, h_ref, gng_ref, gnb_ref, wo_ref, l1g_ref, l1b_ref,
                 wrh_ref, wrl_ref, rb_ref, tri_ref, ones_ref,
                 h1_ref, rf_ref, ri_ref, cnt_ref, *, alpha, tm):
    ones = ones_ref[...]
    y = yf_ref[...] + yb_ref[...]
    mu = _seg_sum(y, ones) * (1.0 / N_B)
    yc = y - mu
    var = _seg_sum(yc * yc, ones) * (1.0 / N_B)
    yr = (yc * lax.rsqrt(var + RWKV_GN_EPS) * gng_ref[...] + gnb_ref[...]) * g_ref[...] + bvg_ref[...]
    half = H_A * V_DIM
    mixed = (jnp.dot(att_ref[...], wo_ref[:half, :], preferred_element_type=F32)
             + jnp.dot(yr.astype(BF16), wo_ref[half:, :], preferred_element_type=F32))
    h1 = _ln(h_ref[...] * alpha + mixed, l1g_ref[...], l1b_ref[...])
    h1_ref[...] = h1

    xh, xl = _split_bf16(h1)
    x = (jnp.dot(xh, wrh_ref[...], preferred_element_type=F32) + jnp.dot(xl, wrh_ref[...], preferred_element_type=F32)
         + jnp.dot(xh, wrl_ref[...], preferred_element_type=F32)) + rb_ref[...]
    lane = lax.broadcasted_iota(jnp.int32, x.shape, 1)
    lanef = lane.astype(F32)
    big = float(LANES)
    gmask = lane < N_GROUPS
    gx = jnp.where(gmask, x, NEG)
    gmax = jnp.max(gx, -1, keepdims=True)
    gidx = jnp.min(jnp.where(gx == gmax, lanef, big), -1, keepdims=True)
    gsum = jnp.sum(jnp.where(gmask, jnp.exp(gx - gmax), 0.0), -1, keepdims=True)
    lo = N_GROUPS + EXPERTS_PER_GROUP * gidx
    emask = (lanef >= lo) & (lanef < lo + EXPERTS_PER_GROUP)
    ex = jnp.where(emask, x, NEG)
    m1 = jnp.max(ex, -1, keepdims=True)
    i1 = jnp.min(jnp.where(emask & (ex == m1), lanef, big), -1, keepdims=True)
    ex2 = jnp.where(lanef == i1, NEG, ex)
    m2 = jnp.max(ex2, -1, keepdims=True)
    i2 = jnp.min(jnp.where(emask & (lanef != i1) & (ex2 == m2), lanef, big), -1, keepdims=True)
    esum = jnp.sum(jnp.where(emask, jnp.exp(ex - m1), 0.0), -1, keepdims=True)
    gp = 1.0 / gsum
    gate0 = gp * (1.0 / esum)
    gate1 = gp * (jnp.exp(m2 - m1) / esum)
    e0 = i1 - N_GROUPS
    e1 = i2 - N_GROUPS

    @pl.when(pl.program_id(0) == 0)
    def _():
        cnt_ref[...] = jnp.zeros_like(cnt_ref)

    onehot = jnp.where((lanef == e0) | (lanef == e1), 1.0, 0.0)
    prefix = jnp.dot(tri_ref[...], onehot.astype(BF16), preferred_element_type=F32) + cnt_ref[...]
    r0 = jnp.sum(jnp.where(lanef == e0, prefix, 0.0), -1, keepdims=True)
    r1 = jnp.sum(jnp.where(lanef == e1, prefix, 0.0), -1, keepdims=True)
    cnt_ref[...] += jnp.sum(onehot, 0, keepdims=True)
    rf_ref[...] = jnp.where(lane == 0, gate0, jnp.where(lane == 1, gate1, 0.0))
    ri = jnp.where(lane == 0, e0, jnp.where(lane == 1, e1, jnp.where(lane == 2, r0, jnp.where(lane == 3, r1, 0.0))))
    ri_ref[...] = ri.astype(jnp.int32)


def _post_mixer(yf, yb, g, bvg, att, h, gn_g, gn_b, w_out, ln_g, ln_b, w_group, b_group, w_expert, b_expert, alpha):
    T, D = h.shape
    tm = _row_tile(T)
    vec = lambda a: a.reshape(1, -1)
    zpad = LANES - N_GROUPS - N_EXPERTS
    wr = jnp.concatenate([w_group, w_expert, jnp.zeros((D, zpad), F32)], axis=1)
    wrh = wr.astype(BF16)
    wrl = (wr - wrh.astype(F32)).astype(BF16)
    rb = jnp.concatenate([b_group, b_expert, jnp.zeros((zpad,), F32)]).reshape(1, LANES)
    tri = jnp.asarray(np.arange(tm)[:, None] > np.arange(tm)[None, :], BF16)
    consts = [vec(gn_g), vec(gn_b), w_out.astype(BF16), vec(ln_g), vec(ln_b), wrh, wrl, rb, tri, _seg_ones(RW)]
    row = lambda w: pl.BlockSpec((tm, w), lambda i: (i, 0))
    full = lambda a: pl.BlockSpec(a.shape, lambda i: (0, 0))
    return pl.pallas_call(
        functools.partial(_post_kernel, alpha=alpha, tm=tm),
        out_shape=(jax.ShapeDtypeStruct((T, D), F32),
                   jax.ShapeDtypeStruct((T, LANES), F32),
                   jax.ShapeDtypeStruct((T, LANES), jnp.int32),
                   jax.ShapeDtypeStruct((1, LANES), F32)),
        grid=(T // tm,),
        in_specs=[row(RW)] * 4 + [row(H_A * V_DIM), row(D)] + [full(c) for c in consts],
        out_specs=(row(D), row(LANES), row(LANES), pl.BlockSpec((1, LANES), lambda i: (0, 0))),
        compiler_params=_cparams(("arbitrary",)),
        name="post_mixer",
    )(yf, yb, g, bvg, att, h, *consts)


def _row_copies(dest_smem, t, make):
    return [make(s, dest_smem[TOP_K * t + s]) for s in range(TOP_K)]


def _load_dest(dest_hbm, dest_smem, isem, tm):
    load = pltpu.make_async_copy(dest_hbm.at[pl.ds(pl.program_id(0) * tm * TOP_K, tm * TOP_K)], dest_smem, isem)
    load.start()
    load.wait()


def _dispatch_kernel(dest_hbm, h_ref, xz_hbm, xb_hbm, dest_smem, isem, sem, *, tm):
    del xz_hbm
    _load_dest(dest_hbm, dest_smem, isem, tm)

    def copies(t):
        src = h_ref.at[pl.ds(t, 1)]
        return _row_copies(dest_smem, t, lambda s, dst: pltpu.make_async_copy(src, xb_hbm.at[pl.ds(dst, 1)], sem))

    def start(t, c):
        for cp in copies(t):
            cp.start()
        return c

    def wait(t, c):
        for cp in copies(t):
            cp.wait()
        return c

    lax.fori_loop(0, tm, start, 0, unroll=8)
    lax.fori_loop(0, tm, wait, 0, unroll=8)


def _dispatch(dest, h, n_rows):
    T, D = h.shape
    tm = _row_tile(T)
    any_spec = pl.BlockSpec(memory_space=pl.ANY)
    return pl.pallas_call(
        functools.partial(_dispatch_kernel, tm=tm),
        out_shape=jax.ShapeDtypeStruct((n_rows, D), F32),
        grid=(T // tm,),
        in_specs=[any_spec, pl.BlockSpec((tm, D), lambda i: (i, 0)), any_spec],
        out_specs=any_spec,
        scratch_shapes=[pltpu.SMEM((tm * TOP_K,), jnp.int32), pltpu.SemaphoreType.DMA, pltpu.SemaphoreType.DMA],
        input_output_aliases={2: 0},
        compiler_params=_cparams(("arbitrary",)),
        name="moe_dispatch",
    )(dest, h, jnp.zeros((n_rows, D), F32))


def _expert_kernel(be_ref, nu_ref, x_ref, w1_ref, w3_ref, w2_ref, o_ref):
    @pl.when(pl.program_id(0) < nu_ref[0])
    def _():
        x = x_ref[...].astype(BF16)
        a = jnp.dot(x, w1_ref[0], preferred_element_type=F32)
        b = jnp.dot(x, w3_ref[0], preferred_element_type=F32)
        hid = (a * jax.nn.sigmoid(a)) * b
        o_ref[...] = jnp.dot(hid.astype(BF16), w2_ref[0], preferred_element_type=F32)


def _expert_ffn(xb, block_e, n_used, w1, w3, w2):
    n_rows, D = xb.shape
    n_blocks = n_rows // MOE_BLOCK
    blk = lambda i, be, nu: jnp.minimum(i, nu[0] - 1)
    wmap = lambda i, be, nu: (be[blk(i, be, nu)], 0, 0)
    grid_spec = pltpu.PrefetchScalarGridSpec(
        num_scalar_prefetch=2,
        grid=(n_blocks,),
        in_specs=[pl.BlockSpec((MOE_BLOCK, D), lambda i, be, nu: (blk(i, be, nu), 0)),
                  pl.BlockSpec((1, D, E_HID), wmap),
                  pl.BlockSpec((1, D, E_HID), wmap),
                  pl.BlockSpec((1, E_HID, D), wmap)],
        out_specs=pl.BlockSpec((MOE_BLOCK, D), lambda i, be, nu: (blk(i, be, nu), 0)),
    )
    return pl.pallas_call(
        _expert_kernel,
        out_shape=jax.ShapeDtypeStruct((n_rows, D), F32),
        grid_spec=grid_spec,
        compiler_params=_cparams(("arbitrary",)),
        name="moe_experts",
    )(block_e, n_used, xb, w1, w3, w2)


def _combine_kernel(dest_hbm, yb_hbm, gate_ref, h_ref, g_ref, b_ref, o_ref, dest_smem, buf, isem, sem, *, tm, alpha):
    _load_dest(dest_hbm, dest_smem, isem, tm)

    def copies(t):
        return _row_copies(dest_smem, t, lambda s, src: pltpu.make_async_copy(
            yb_hbm.at[pl.ds(src, 1)], buf.at[s, pl.ds(t, 1)], sem))

    def start(t, c):
        for cp in copies(t):
            cp.start()
        return c

    def wait(t, c):
        for cp in copies(t):
            cp.wait()
        return c

    lax.fori_loop(0, tm, start, 0, unroll=8)
    lax.fori_loop(0, tm, wait, 0, unroll=8)
    gate = gate_ref[...]
    ff = sum(buf[s] * gate[:, s:s + 1] for s in range(TOP_K))
    o_ref[...] = _ln(h_ref[...] * alpha + ff, g_ref[...], b_ref[...])


def _combine(dest, yb, route_f, h, ln_g, ln_b, alpha):
    T, D = h.shape
    tm = _row_tile(T)
    any_spec = pl.BlockSpec(memory_space=pl.ANY)
    row = lambda w: pl.BlockSpec((tm, w), lambda i: (i, 0))
    vec = pl.BlockSpec((1, D), lambda i: (0, 0))
    return pl.pallas_call(
        functools.partial(_combine_kernel, tm=tm, alpha=alpha),
        out_shape=jax.ShapeDtypeStruct((T, D), F32),
        grid=(T // tm,),
        in_specs=[any_spec, any_spec, row(LANES), row(D), vec, vec],
        out_specs=row(D),
        scratch_shapes=[pltpu.SMEM((tm * TOP_K,), jnp.int32), pltpu.VMEM((TOP_K, tm, D), F32),
                        pltpu.SemaphoreType.DMA, pltpu.SemaphoreType.DMA],
        compiler_params=_cparams(("arbitrary",)),
        name="moe_combine",
    )(dest, yb, route_f, h, ln_g.reshape(1, D), ln_b.reshape(1, D))


def _hier_moe(h, route_f, route_i, counts, w1, w3, w2, ln_g, ln_b, alpha):
    T, D = h.shape
    counts = counts[0, :N_EXPERTS].astype(jnp.int32)
    padded = (counts + MOE_BLOCK - 1) // MOE_BLOCK * MOE_BLOCK
    pends = jnp.cumsum(padded)
    pstart = pends - padded
    n_rows = (T * TOP_K + N_EXPERTS * (MOE_BLOCK - 1) + MOE_BLOCK - 1) // MOE_BLOCK * MOE_BLOCK
    n_blocks = n_rows // MOE_BLOCK
    block_start = jnp.arange(n_blocks, dtype=jnp.int32) * MOE_BLOCK
    block_e = jnp.minimum(jnp.sum(pends[None, :] <= block_start[:, None], axis=1), N_EXPERTS - 1).astype(jnp.int32)
    n_used = (pends[-1:] // MOE_BLOCK).astype(jnp.int32)
    dest = (jnp.take(pstart, route_i[:, :TOP_K]) + route_i[:, TOP_K:2 * TOP_K]).reshape(-1)
    xb = _dispatch(dest, h, n_rows)
    yb = _expert_ffn(xb, block_e, n_used, w1, w3, w2)
    return _combine(dest, yb, route_f, h, ln_g, ln_b, alpha)


def kernel(x, positions, meta_tokens, emb_ln_g, emb_ln_b, w_in, mla_q_norm, mla_kv_norm, mla_w_uq, mla_w_ukv, mla_out_norm, rwkv_mu_prev, rwkv_mu_next, rwkv_w0, rwkv_w2, rwkv_a0, rwkv_a2, rwkv_g2, rwkv_k_k, rwkv_k_a, rwkv_r_k, rwkv_gn_g, rwkv_gn_b, rwkv_v0, rwkv_v1, rwkv_v2, w_out, ln1_g, ln1_b, moe_w_group, moe_b_group, moe_w_expert, moe_b_expert, moe_w1, moe_w3, moe_w2, ln2_g, ln2_b):
    B, seq, D = x.shape
    depth = w_in.shape[0]
    alpha = (2 * depth) ** 0.25
    L = seq + N_META
    Lp = -(-L // LANES) * LANES
    T = B * Lp
    nb = next(n for n in (4, 2, 1) if B % n == 0)
    tk = 384 if Lp % 384 == 0 else LANES
    tq = next(t for t in (1408, 1152, 768, 384, LANES) if Lp % t == 0)

    meta = jnp.broadcast_to(meta_tokens.astype(x.dtype)[None], (B, N_META, D))
    h = jnp.concatenate([meta, x, jnp.zeros((B, Lp - L, D), x.dtype)], axis=1).reshape(T, D)
    h = _ln_residual(h, jnp.zeros_like(h), emb_ln_g, emb_ln_b, 1.0)

    key_bias = jnp.where(jnp.arange(Lp) < L, 0.0, NEG).astype(F32)[None, :]
    pos = jnp.concatenate([jnp.broadcast_to(jnp.arange(N_META, dtype=jnp.int32), (B, N_META)),
                           positions + N_META, jnp.zeros((B, Lp - L), jnp.int32)], axis=1)
    inv_freq = ROPE_THETA ** (-jnp.arange(0, QK_ROPE, 2, dtype=F32) / QK_ROPE)
    ang = pos.astype(F32)[..., None] * inv_freq
    cos, sin = jnp.cos(ang).reshape(T, -1), jnp.sin(ang).reshape(T, -1)
    zeros = jnp.zeros((T, LANES - QK_ROPE), F32)
    rope_cos = jnp.concatenate([cos, cos, zeros], axis=1)
    rope_sin = jnp.concatenate([-sin, sin, zeros], axis=1)
    qscale = (QK_NOPE + QK_ROPE) ** -0.5 * math.log2(math.e)

    v_first = None
    for li in range(depth):
        q, k, v, rw = _project(h, w_in[li], mla_q_norm[li], mla_kv_norm[li], mla_w_uq[li], mla_w_ukv[li],
                               rope_cos, rope_sin, qscale)
        y_att = _attention(q.reshape(B, Lp, -1), k.reshape(B, Lp, -1), v.reshape(B, Lp, -1),
                           key_bias, mla_out_norm[li], tq, tk)
        vres = None if li == 0 else (rwkv_v0[li - 1], rwkv_v1[li - 1], rwkv_v2[li - 1])
        outs = _rwkv_prep(rw.reshape(B, Lp, -1), L, rwkv_mu_prev[li], rwkv_mu_next[li], rwkv_w0[li], rwkv_w2[li],
                          rwkv_a0[li], rwkv_a2[li], rwkv_g2[li], rwkv_k_k[li], rwkv_k_a[li], rwkv_r_k[li],
                          v_first, vres)
        kk, vm, g, bvg, vt = outs[0], outs[1], outs[14], outs[15], outs[16]
        if li == 0:
            v_first = outs[17]
        y_f, y_b = _wkv_bidir(kk, vm, vt, (outs[2:8], outs[8:14]), nb)
        flat = lambda a: a.reshape(T, -1)
        h, route_f, route_i, counts = _post_mixer(
            flat(y_f), flat(y_b), flat(g), flat(bvg), flat(y_att), h, rwkv_gn_g[li], rwkv_gn_b[li], w_out[li],
            ln1_g[li], ln1_b[li], moe_w_group[li], moe_b_group[li], moe_w_expert[li], moe_b_expert[li], alpha)
        h = _hier_moe(h, route_f, route_i, counts,
                      moe_w1[li].astype(BF16), moe_w3[li].astype(BF16), moe_w2[li].astype(BF16),
                      ln2_g[li], ln2_b[li], alpha)
    return h.reshape(B, Lp, D)[:, N_META:L]
```

```python
import functools
import math

import numpy as np
import jax
import jax.numpy as jnp
from jax import lax
from jax.experimental import pallas as pl
from jax.experimental.pallas import tpu as pltpu

F32 = jnp.float32
BF16 = jnp.bfloat16

N_META = 16
H_A = 4
QK_NOPE = 128
QK_ROPE = 64
V_DIM = 128
Q_LORA = 256
KV_LORA = 128
ROPE_THETA = 10000.0
H_B = 8
N_B = 64
RW = H_B * N_B
DECAY_LORA = 64
AAA_LORA = 64
GATE_LORA = 128
VRES_LORA = 32
RWKV_GN_EPS = 64e-5
MLA_COLS = Q_LORA + KV_LORA + QK_ROPE
N_GROUPS = 4
EXPERTS_PER_GROUP = 8
N_EXPERTS = N_GROUPS * EXPERTS_PER_GROUP
TOP_K = 2
E_HID = 256
MOE_BLOCK = 256
LN_EPS = 1e-5
RMS_EPS = 1e-6

LANES = 128
SUBLANES = 8
TIME_BLOCK = 128
SCAN_CHUNK = 64
GROUPS_PER_ITER = 4
ROW_TILE = 512
VMEM_LIMIT = 56 * 1024 * 1024
HEAD_W = 2 * LANES
NEG = -1e30


def _cparams(sem):
    return pltpu.CompilerParams(dimension_semantics=sem, vmem_limit_bytes=VMEM_LIMIT)


def _row_tile(m):
    return next(t for t in (ROW_TILE, 384, 256, LANES) if m % t == 0)


def _split_bf16(x):
    hi = x.astype(BF16)
    lo = (x - hi.astype(F32)).astype(BF16)
    return hi, lo


def _seg_sum(x, ones):
    hi, lo = _split_bf16(x)
    return jnp.dot(jnp.concatenate([hi, lo], axis=-1), ones, preferred_element_type=F32)


def _seg_ones(width):
    m = np.arange(2 * width)[:, None] % width
    n = np.arange(width)[None, :]
    return jnp.asarray(m // N_B == n // N_B, BF16)


def _ln(x, g, b):
    mu = jnp.mean(x, -1, keepdims=True)
    xc = x - mu
    var = jnp.mean(xc * xc, -1, keepdims=True)
    return xc * lax.rsqrt(var + LN_EPS) * g + b


def _ln_kernel(x_ref, r_ref, g_ref, b_ref, o_ref, *, alpha):
    o_ref[...] = _ln(x_ref[...] * alpha + r_ref[...], g_ref[...], b_ref[...])


def _ln_residual(x, r, g, b, alpha):
    M, D = x.shape
    tm = _row_tile(M)
    row = pl.BlockSpec((tm, D), lambda i: (i, 0))
    vec = pl.BlockSpec((1, D), lambda i: (0, 0))
    return pl.pallas_call(
        functools.partial(_ln_kernel, alpha=alpha),
        out_shape=jax.ShapeDtypeStruct((M, D), F32),
        grid=(M // tm,),
        in_specs=[row, row, vec, vec],
        out_specs=row,
        compiler_params=_cparams(("parallel",)),
        name="layer_norm",
    )(x, r, g.reshape(1, D), b.reshape(1, D))


def _rms(x, g):
    return x * lax.rsqrt(jnp.mean(x * x, -1, keepdims=True) + RMS_EPS) * g


def _proj_kernel(h_ref, win_ref, qn_ref, kvn_ref, wuq_ref, wukv_ref, c_ref, s_ref,
                 q_ref, k_ref, v_ref, rw_ref, *, qscale):
    proj = jnp.dot(h_ref[...].astype(BF16), win_ref[...], preferred_element_type=F32)
    rw_ref[...] = proj[:, MLA_COLS + N_B:]
    q = jnp.dot(_rms(proj[:, :Q_LORA], qn_ref[...]).astype(BF16), wuq_ref[...], preferred_element_type=F32)
    kv = jnp.dot(_rms(proj[:, Q_LORA:Q_LORA + KV_LORA], kvn_ref[...]).astype(BF16), wukv_ref[...],
                 preferred_element_type=F32)
    cos, sin = c_ref[...], s_ref[...]
    lane = lax.broadcasted_iota(jnp.int32, cos.shape, 1)
    half = QK_ROPE // 2

    def rope(x):
        partner = jnp.where(lane < half, pltpu.roll(x, LANES - half, 1), pltpu.roll(x, half, 1))
        return x * cos + partner * sin

    k_pe = rope(proj[:, Q_LORA + KV_LORA:Q_LORA + KV_LORA + LANES]).astype(BF16)
    for h in range(H_A):
        o = h * HEAD_W
        q_ref[:, o:o + LANES] = (q[:, o:o + LANES] * qscale).astype(BF16)
        q_ref[:, o + LANES:o + HEAD_W] = (rope(q[:, o + LANES:o + HEAD_W]) * qscale).astype(BF16)
        k_ref[:, o:o + LANES] = kv[:, o:o + LANES].astype(BF16)
        k_ref[:, o + LANES:o + HEAD_W] = k_pe
        v_ref[:, h * V_DIM:(h + 1) * V_DIM] = kv[:, o + LANES:o + HEAD_W].astype(BF16)


def _project(h, w_in, q_norm, kv_norm, w_uq, w_ukv, rope_cos, rope_sin, qscale):
    T, D = h.shape
    tm = _row_tile(T)
    n_rw = w_in.shape[1] - MLA_COLS
    pad = jnp.zeros((D, N_B), F32)
    win = jnp.concatenate([w_in[:, :MLA_COLS], pad, w_in[:, MLA_COLS:]], axis=1).astype(BF16)
    wq = w_uq.reshape(Q_LORA, H_A, QK_NOPE + QK_ROPE)
    wq = jnp.concatenate([wq, jnp.zeros((Q_LORA, H_A, HEAD_W - QK_NOPE - QK_ROPE), F32)], axis=-1)
    wq = wq.reshape(Q_LORA, H_A * HEAD_W).astype(BF16)
    row = lambda w: pl.BlockSpec((tm, w), lambda i: (i, 0))
    full = lambda a: pl.BlockSpec(a.shape, lambda i: (0, 0))
    args = (h, win, q_norm.reshape(1, -1), kv_norm.reshape(1, -1), wq, w_ukv.astype(BF16), rope_cos, rope_sin)
    return pl.pallas_call(
        functools.partial(_proj_kernel, qscale=qscale),
        out_shape=(jax.ShapeDtypeStruct((T, H_A * HEAD_W), BF16),
                   jax.ShapeDtypeStruct((T, H_A * HEAD_W), BF16),
                   jax.ShapeDtypeStruct((T, H_A * V_DIM), BF16),
                   jax.ShapeDtypeStruct((T, n_rw), F32)),
        grid=(T // tm,),
        in_specs=[row(D)] + [full(a) for a in args[1:6]] + [row(LANES), row(LANES)],
        out_specs=(row(H_A * HEAD_W), row(H_A * HEAD_W), row(H_A * V_DIM), row(n_rw)),
        compiler_params=_cparams(("parallel",)),
        name="in_proj",
    )(*args)


def _attn_kernel(q_ref, k_ref, v_ref, bias_ref, g_ref, o_ref, m_ref, acc_ref, s_ref, p_ref, a_ref, *, tk):
    q = q_ref[0]
    n = k_ref.shape[1] // tk
    one_col = (lax.broadcasted_iota(jnp.int32, (tk, V_DIM), 1) == 0).astype(BF16)

    def scores(c):
        off = pl.multiple_of(c * tk, tk)
        s = lax.dot_general(q, k_ref[0, pl.ds(off, tk), :], (((1,), (1,)), ((), ())), preferred_element_type=F32)
        return s + bias_ref[:, pl.ds(off, tk)]

    def weighted_values(c):
        off = pl.multiple_of(c * tk, tk)
        v1 = jnp.concatenate([v_ref[0, pl.ds(off, tk), :], one_col], axis=1)
        acc_ref[...] = a_ref[...] * acc_ref[...] + jnp.dot(p_ref[...], v1, preferred_element_type=F32)

    m_ref[...] = jnp.full_like(m_ref, NEG)
    acc_ref[...] = jnp.zeros_like(acc_ref)
    p_ref[...] = jnp.zeros_like(p_ref)
    a_ref[...] = jnp.ones_like(a_ref)
    s_ref[...] = scores(0)

    def body(c, carry):
        weighted_values(jnp.maximum(c - 1, 0))
        s = s_ref[...]
        s_ref[...] = scores(jnp.minimum(c + 1, n - 1))
        m_prev = m_ref[...]
        m_new = jnp.maximum(m_prev, jnp.max(s, -1, keepdims=True))
        p_ref[...] = jnp.exp2(s - m_new).astype(BF16)
        a_ref[...] = jnp.exp2(m_prev - m_new)
        m_ref[...] = m_new
        return carry

    lax.fori_loop(0, n, body, 0)
    weighted_values(n - 1)
    acc = acc_ref[...]
    o = acc[:, :V_DIM] / acc[:, V_DIM:V_DIM + 1]
    o = o * lax.rsqrt(jnp.mean(o * o, -1, keepdims=True) + RMS_EPS) * g_ref[...]
    o_ref[0] = o.astype(BF16)


def _attention(q, k, v, bias, out_gain, tq, tk):
    B, Lp, _ = q.shape
    return pl.pallas_call(
        functools.partial(_attn_kernel, tk=tk),
        out_shape=jax.ShapeDtypeStruct((B, Lp, H_A * V_DIM), BF16),
        grid=(B, H_A, Lp // tq),
        in_specs=[pl.BlockSpec((1, tq, HEAD_W), lambda b, h, i: (b, i, h)),
                  pl.BlockSpec((1, Lp, HEAD_W), lambda b, h, i: (b, 0, h)),
                  pl.BlockSpec((1, Lp, V_DIM), lambda b, h, i: (b, 0, h)),
                  pl.BlockSpec((1, Lp), lambda b, h, i: (0, 0)),
                  pl.BlockSpec((1, V_DIM), lambda b, h, i: (0, h))],
        out_specs=pl.BlockSpec((1, tq, V_DIM), lambda b, h, i: (b, i, h)),
        scratch_shapes=[pltpu.VMEM((tq, 1), F32), pltpu.VMEM((tq, 2 * V_DIM), F32),
                        pltpu.VMEM((tq, tk), F32), pltpu.VMEM((tq, tk), BF16), pltpu.VMEM((tq, 1), F32)],
        compiler_params=_cparams(("parallel", "parallel", "parallel")),
        name="mla_attention",
    )(q, k, v, bias, out_gain.reshape(1, H_A * V_DIM))


def _prep_kernel(*refs, first, seq_len, tm):
    if first:
        (rw_ref, pv_ref, nx_ref, mup_ref, mun_ref, w2_ref, w0_ref, a2_ref, a0_ref, g2_ref, kk_ref, ka_ref, rk_ref,
         ones_ref, tri_ref, kk_o, v_o, wf_o, kdf_o, bf_o, rbf_o, ktf_o, btf_o, wb_o, kdb_o, bb_o, rbb_o, ktb_o, btb_o,
         g_o, bvg_o, vt_o, vfirst_o) = refs
    else:
        (rw_ref, pv_ref, nx_ref, mup_ref, mun_ref, w2_ref, w0_ref, a2_ref, a0_ref, g2_ref, kk_ref, ka_ref, rk_ref,
         ones_ref, tri_ref, vf_ref, v0_ref, v1_ref, v2_ref,
         kk_o, v_o, wf_o, kdf_o, bf_o, rbf_o, ktf_o, btf_o, wb_o, kdb_o, bb_o, rbb_o, ktb_o, btb_o, g_o, bvg_o,
         vt_o) = refs
    i = pl.program_id(1)
    n_t = pl.num_programs(1)
    row = lax.broadcasted_iota(jnp.int32, (tm, 1), 0)
    t = i * tm + row
    valid = t < seq_len
    rw = jnp.where(valid, rw_ref[0], 0.0)
    prev_row = jnp.where((i > 0) & (i * tm - 1 < seq_len), pv_ref[0, SUBLANES - 1:SUBLANES, :], 0.0)
    next_row = jnp.where((i < n_t - 1) & ((i + 1) * tm < seq_len), nx_ref[0, 0:1, :], 0.0)
    prev = jnp.where(row == 0, prev_row, pltpu.roll(rw, 1, 0))
    nxt = jnp.where(row == tm - 1, next_row, pltpu.roll(rw, tm - 1, 0))
    u = rw + mup_ref[...] * (prev - rw) + mun_ref[...] * (nxt - rw)
    r, k, v = u[:, :RW], u[:, RW:2 * RW], u[:, 2 * RW:3 * RW]
    wd = u[:, 3 * RW:3 * RW + LANES]
    ad = u[:, 3 * RW + LANES:3 * RW + 2 * LANES]
    gd = u[:, 3 * RW + 2 * LANES:]
    ones = ones_ref[...]
    if first:
        vfirst_o[0] = v
    else:
        low = jnp.dot(v.astype(BF16), v1_ref[...], preferred_element_type=F32)
        mix = jax.nn.sigmoid(v0_ref[...] + jnp.dot(low.astype(BF16), v2_ref[...], preferred_element_type=F32))
        v = v + (vf_ref[0] - v) * mix
    g = jnp.dot(jax.nn.sigmoid(gd).astype(BF16), g2_ref[...], preferred_element_type=F32)
    kk = k * kk_ref[...]
    kk = jnp.where(valid, kk * lax.rsqrt(_seg_sum(kk * kk, ones) + 1e-12), 0.0)
    wl = w0_ref[...] + jnp.dot(jnp.tanh(wd).astype(BF16), w2_ref[...], preferred_element_type=F32)
    logw = -math.exp(-0.5) * jax.nn.sigmoid(wl)
    decay = jnp.exp(logw)
    a = jax.nn.sigmoid(a0_ref[...] + jnp.dot(ad.astype(BF16), a2_ref[...], preferred_element_type=F32))
    ka = ka_ref[...]
    kd_f = jnp.where(valid, k * (1.0 + (a[:, :RW] - 1.0) * ka), 0.0)
    kd_b = jnp.where(valid, k * (1.0 + (a[:, RW:] - 1.0) * ka), 0.0)
    bonus = _seg_sum(r * (kd_f + kd_b) * rk_ref[...], ones)
    b_f, b_b = kk * a[:, :RW], kk * a[:, RW:]
    cum_f = _cumsum3(tri_ref[0], logw[:, :RW])
    cum_b = _cumsum3(tri_ref[1], logw[:, RW:])
    p_f, ip_f = jnp.exp(cum_f), jnp.exp(-cum_f)
    p_b, ip_b = jnp.exp(cum_b), jnp.exp(-cum_b)
    kk_o[0] = kk
    vm = jnp.where(valid, v, 0.0)
    v_o[0] = vm.astype(BF16)
    lo_half = lax.broadcasted_iota(jnp.int32, (N_B, LANES), 1) < N_B
    for tb in range(tm // TIME_BLOCK):
        for hp in range(H_B // 2):
            xt = vm[tb * TIME_BLOCK:(tb + 1) * TIME_BLOCK, hp * LANES:(hp + 1) * LANES].T
            top, bot = xt[:N_B], xt[N_B:]
            vt_o[0, tb, hp, 0] = jnp.where(lo_half, top, pltpu.roll(bot, N_B, 1))
            vt_o[0, tb, hp, 1] = jnp.where(lo_half, pltpu.roll(top, N_B, 1), bot)
    wf_o[0] = decay[:, :RW]
    wb_o[0] = decay[:, RW:]
    kdf_o[0] = kd_f
    kdb_o[0] = kd_b
    bf_o[0] = b_f
    bb_o[0] = b_b
    rbf_o[0] = r * p_f
    rbb_o[0] = r * p_b
    ktf_o[0] = (kd_f * ip_f).astype(BF16)
    ktb_o[0] = (kd_b * ip_b).astype(BF16)
    btf_o[0] = (b_f * ip_f).astype(BF16)
    btb_o[0] = (b_b * ip_b).astype(BF16)
    g_o[0] = g
    bvg_o[0] = bonus * v * g


def _cumsum3(tri, x):
    h1 = x.astype(BF16)
    r1 = x - h1.astype(F32)
    h2 = r1.astype(BF16)
    h3 = (r1 - h2.astype(F32)).astype(BF16)
    return (jnp.dot(tri, h1, preferred_element_type=F32) + jnp.dot(tri, h2, preferred_element_type=F32)
            + jnp.dot(tri, h3, preferred_element_type=F32))


def _chunk_tri(tm):
    t = np.arange(tm)
    same = (t[:, None] // SCAN_CHUNK) == (t[None, :] // SCAN_CHUNK)
    return jnp.asarray(np.stack([same & (t[None, :] <= t[:, None]), same & (t[None, :] >= t[:, None])]), BF16)


def _block_diag2(a, b):
    z = jnp.zeros_like(a)
    return jnp.concatenate([jnp.concatenate([a, z], 1), jnp.concatenate([z, b], 1)], 0)


def _rwkv_prep(rw, seq_len, mu_prev, mu_next, w0, w2, a0, a2, g2, k_k, k_a, r_k, v_first, vres):
    B, Lp, n_rw = rw.shape
    tm = _row_tile(Lp)
    tpb = tm // SUBLANES
    first = vres is None
    vec = lambda a: a.reshape(1, -1)
    consts = [vec(mu_prev), vec(mu_next), _block_diag2(w2[0], w2[1]).astype(BF16), vec(w0),
              _block_diag2(a2[0], a2[1]).astype(BF16), vec(a0), g2.astype(BF16), vec(k_k), vec(k_a), vec(r_k),
              _seg_ones(RW), _chunk_tri(tm)]
    tile = lambda w: pl.BlockSpec((1, tm, w), lambda b, i: (b, i, 0))
    full = lambda a: pl.BlockSpec(a.shape, lambda b, i: (0,) * a.ndim)
    in_specs = [tile(n_rw),
                pl.BlockSpec((1, SUBLANES, n_rw), lambda b, i: (b, jnp.maximum(i * tpb - 1, 0), 0)),
                pl.BlockSpec((1, SUBLANES, n_rw), lambda b, i: (b, jnp.minimum((i + 1) * tpb, Lp // SUBLANES - 1), 0))]
    in_specs += [full(c) for c in consts]
    args = [rw, rw, rw] + consts
    if not first:
        v0, v1, v2 = vres
        v1p = jnp.concatenate([v1, jnp.zeros((RW, LANES - VRES_LORA), F32)], 1).astype(BF16)
        v2p = jnp.concatenate([v2, jnp.zeros((LANES - VRES_LORA, RW), F32)], 0).astype(BF16)
        extra = [vec(v0), v1p, v2p]
        in_specs += [tile(RW)] + [full(c) for c in extra]
        args += [v_first] + extra
    vt_shape = (B, Lp // TIME_BLOCK, H_B // 2, 2, N_B, LANES)
    vt_spec = pl.BlockSpec((1, tm // TIME_BLOCK) + vt_shape[2:], lambda b, i: (b, i, 0, 0, 0, 0))
    shapes = [jax.ShapeDtypeStruct((B, Lp, RW), BF16 if i in (1, 6, 7, 12, 13) else F32) for i in range(16)]
    shapes.append(jax.ShapeDtypeStruct(vt_shape, F32))
    specs = [tile(RW)] * 16 + [vt_spec]
    if first:
        shapes.append(jax.ShapeDtypeStruct((B, Lp, RW), F32))
        specs.append(tile(RW))
    out = pl.pallas_call(
        functools.partial(_prep_kernel, first=first, seq_len=seq_len, tm=tm),
        out_shape=tuple(shapes),
        grid=(B, Lp // tm),
        in_specs=in_specs,
        out_specs=tuple(specs),
        compiler_params=_cparams(("parallel", "parallel")),
        name="rwkv_prep",
    )(*args)
    return out


def _scan_kernel(kkf, wf, kdf, bf, vf, rbf, ktf, btf, vtf, kkb, wb, kdb, bb_, vb_, rbb, ktb, btb, vtb, ones_ref,
                 yf_ref, yb_ref, s_ref, s0_ref, u_ref, *, nb):
    j = pl.program_id(1)
    C = nb * 4
    half_t = TIME_BLOCK // 2
    assert half_t == SCAN_CHUNK

    @pl.when(j == 0)
    def _():
        s_ref[...] = jnp.zeros_like(s_ref)

    lane1 = lax.broadcasted_iota(jnp.int32, (N_B, LANES), 1)
    lo_half = lane1 < N_B

    ones = ones_ref[...]
    lane_id = lax.broadcasted_iota(jnp.int32, (C * N_B, LANES), 1)
    lane = lane_id % N_B
    head_base = (lane_id // N_B) * N_B
    row_refs = ((kkf, wf, kdf, bf), (kkb, wb, kdb, bb_))
    chunk_refs = ((vf, rbf, ktf, btf, yf_ref), (vb_, rbb, ktb, btb, yb_ref))
    vt_refs = (vtf, vtb)
    tr = lax.broadcasted_iota(jnp.int32, (N_B, LANES), 0)
    tc = lane1 % N_B
    earlier2 = (tc <= tr, tc >= tr)
    lanes_dims = ((1,), (1,))

    def step(d, tiles, row, oh, tt):
        def rows(tile):
            return jnp.concatenate(
                [jnp.broadcast_to(tile[n, row:row + 1, hp * LANES:(hp + 1) * LANES], (N_B, LANES))
                 for n in range(nb) for hp in range(4)], axis=0)

        kk, w, kd, b = [rows(x) for x in tiles]
        S = s_ref[d]
        sa = jnp.dot((S * kk).astype(BF16), ones[:LANES], preferred_element_type=F32)
        vt = jnp.concatenate([vt_refs[d][n, 0, hp, oh] for n in range(nb) for hp in range(4)], axis=0)
        vb = jnp.take_along_axis(vt, head_base + tt, axis=1, mode="promise_in_bounds")
        s_ref[d] = S * w - sa * b + vb * kd
        u_ref[d] = jnp.where(lane == tt, sa, u_ref[d])

    def make_body(half):
        def body(g2, carry):
            for sub in range(GROUPS_PER_ITER):
                g = g2 * GROUPS_PER_ITER + sub
                t0f = pl.multiple_of(half * half_t + g * SUBLANES, SUBLANES)
                t0b = pl.multiple_of(TIME_BLOCK - SUBLANES - (half * half_t + g * SUBLANES), SUBLANES)
                tiles_f = [x[:, pl.ds(t0f, SUBLANES), :] for x in row_refs[0]]
                tiles_b = [x[:, pl.ds(t0b, SUBLANES), :] for x in row_refs[1]]
                for i in range(SUBLANES):
                    step(0, tiles_f, i, half, g * SUBLANES + i)
                    step(1, tiles_b, SUBLANES - 1 - i, 1 - half, half_t - 1 - (g * SUBLANES + i))
            return carry
        return body

    def chunk_outputs(d, oh):
        vref, rref, ktref, btref, yref = chunk_refs[d]
        r0 = oh * half_t
        pairs = [(n, hp) for n in range(nb) for hp in range(4)]

        def blk(ref, n, hp):
            return ref[n, r0:r0 + half_t, hp * LANES:(hp + 1) * LANES]

        def stage1(n, hp):
            c = n * 4 + hp
            rb, kt, bt = blk(rref, n, hp), blk(ktref, n, hp), blk(btref, n, hp)
            s0h = s0_ref[d, c * N_B:(c + 1) * N_B, :].astype(BF16)
            r2 = jnp.concatenate([jnp.where(lo_half, rb, 0.0), jnp.where(lo_half, 0.0, rb)], axis=0)
            r2h, r2l = _split_bf16(r2)
            rhs = jnp.concatenate([kt, bt, s0h, s0h], axis=0)
            return lax.dot_general(jnp.concatenate([r2h, r2l], axis=1), jnp.concatenate([rhs, rhs], axis=1),
                                   (lanes_dims, ((), ())), preferred_element_type=F32)

        def stage2(n, hp, gb):
            c = n * 4 + hp
            vh = blk(vref, n, hp)
            ut = u_ref[d, c * N_B:(c + 1) * N_B, :]
            u2 = jnp.concatenate([ut, ut], axis=0).T
            per_head = []
            for h in range(2):
                gh = gb[h * N_B:(h + 1) * N_B]
                gh_, gl_ = _split_bf16(jnp.where(earlier2[d], gh[:, :LANES], 0.0))
                w = jnp.concatenate([vh, (-u2[h * N_B:(h + 1) * N_B]).astype(BF16)], axis=0)
                corr = jnp.dot(jnp.concatenate([gh_, gl_], axis=1), jnp.concatenate([w, w], axis=0),
                               preferred_element_type=F32)
                per_head.append(gh[:, LANES:] + corr)
            yref[n, r0:r0 + half_t, hp * LANES:(hp + 1) * LANES] = jnp.where(lo_half, per_head[0], per_head[1])

        ahead = 6
        gbs = [stage1(*p) for p in pairs[:ahead]]
        for i, p in enumerate(pairs):
            if i + ahead < len(pairs):
                gbs.append(stage1(*pairs[i + ahead]))
            stage2(*p, gbs[i])

    for half in range(2):
        s0_ref[...] = s_ref[...]
        u_ref[...] = jnp.zeros_like(u_ref)
        lax.fori_loop(0, half_t // (SUBLANES * GROUPS_PER_ITER), make_body(half), 0)
        chunk_outputs(0, half)
        chunk_outputs(1, 1 - half)


def _wkv_bidir(kk, v, vt, per_dir, nb):
    B, Lp, _ = kk.shape
    nblk = Lp // TIME_BLOCK
    ones = _seg_ones(LANES)
    fwd = lambda bi, j: (bi, j, 0)
    bwd = lambda bi, j: (bi, nblk - 1 - j, 0)
    blk = (nb, TIME_BLOCK, RW)
    vt_blk = (nb, 1) + vt.shape[2:]
    (wf, kdf, bf, rbf, ktf, btf), (wb, kdb, bb_, rbb, ktb, btb) = per_dir
    y_shape = jax.ShapeDtypeStruct((B, Lp, RW), F32)
    rows = nb * 4 * N_B
    return pl.pallas_call(
        functools.partial(_scan_kernel, nb=nb),
        out_shape=(y_shape, y_shape),
        grid=(B // nb, nblk),
        in_specs=[pl.BlockSpec(blk, fwd)] * 8 + [pl.BlockSpec(vt_blk, lambda bi, j: (bi, j, 0, 0, 0, 0))]
                 + [pl.BlockSpec(blk, bwd)] * 8 + [pl.BlockSpec(vt_blk, lambda bi, j: (bi, nblk - 1 - j, 0, 0, 0, 0))]
                 + [pl.BlockSpec(ones.shape, lambda bi, j: (0, 0))],
        out_specs=(pl.BlockSpec(blk, fwd), pl.BlockSpec(blk, bwd)),
        scratch_shapes=[pltpu.VMEM((2, rows, LANES), F32),
                        pltpu.VMEM((2, rows, LANES), F32),
                        pltpu.VMEM((2, rows, LANES), F32)],
        compiler_params=_cparams(("parallel", "arbitrary")),
        name="wkv_scan",
    )(kk, wf, kdf, bf, v, rbf, ktf, btf, vt, kk, wb, kdb, bb_, v, rbb, ktb, btb, vt, ones)


def _post_kernel(yf_ref, yb_ref, g_ref, bvg_ref, att_ref, h_ref, gng_ref, gnb_ref, wo_ref, l1g_ref, l1b_ref,
                 wrh_ref, wrl_ref, rb_ref, tri_ref, ones_ref,
                 h1_ref, rf_ref, ri_ref, cnt_ref, *, alpha, tm):
    ones = ones_ref[...]
    y = yf_ref[...] + yb_ref[...]
    mu = _seg_sum(y, ones) * (1.0 / N_B)
    yc = y - mu
    var = _seg_sum(yc * yc, ones) * (1.0 / N_B)
    yr = (yc * lax.rsqrt(var + RWKV_GN_EPS) * gng_ref[...] + gnb_ref[...]) * g_ref[...] + bvg_ref[...]
    half = H_A * V_DIM
    mixed = (jnp.dot(att_ref[...], wo_ref[:half, :], preferred_element_type=F32)
             + jnp.dot(yr.astype(BF16), wo_ref[half:, :], preferred_element_type=F32))
    h1 = _ln(h_ref[...] * alpha + mixed, l1g_ref[...], l1b_ref[...])
    h1_ref[...] = h1

    xh, xl = _split_bf16(h1)
    x = (jnp.dot(xh, wrh_ref[...], preferred_element_type=F32) + jnp.dot(xl, wrh_ref[...], preferred_element_type=F32)
         + jnp.dot(xh, wrl_ref[...], preferred_element_type=F32)) + rb_ref[...]
    lane = lax.broadcasted_iota(jnp.int32, x.shape, 1)
    lanef = lane.astype(F32)
    big = float(LANES)
    gmask = lane < N_GROUPS
    gx = jnp.where(gmask, x, NEG)
    gmax = jnp.max(gx, -1, keepdims=True)
    gidx = jnp.min(jnp.where(gx == gmax, lanef, big), -1, keepdims=True)
    gsum = jnp.sum(jnp.where(gmask, jnp.exp(gx - gmax), 0.0), -1, keepdims=True)
    lo = N_GROUPS + EXPERTS_PER_GROUP * gidx
    emask = (lanef >= lo) & (lanef < lo + EXPERTS_PER_GROUP)
    ex = jnp.where(emask, x, NEG)
    m1 = jnp.max(ex, -1, keepdims=True)
    i1 = jnp.min(jnp.where(emask & (ex == m1), lanef, big), -1, keepdims=True)
    ex2 = jnp.where(lanef == i1, NEG, ex)
    m2 = jnp.max(ex2, -1, keepdims=True)
    i2 = jnp.min(jnp.where(emask & (lanef != i1) & (ex2 == m2), lanef, big), -1, keepdims=True)
    esum = jnp.sum(jnp.where(emask, jnp.exp(ex - m1), 0.0), -1, keepdims=True)
    gp = 1.0 / gsum
    gate0 = gp * (1.0 / esum)
    gate1 = gp * (jnp.exp(m2 - m1) / esum)
    e0 = i1 - N_GROUPS
    e1 = i2 - N_GROUPS

    @pl.when(pl.program_id(0) == 0)
    def _():
        cnt_ref[...] = jnp.zeros_like(cnt_ref)

    onehot = jnp.where((lanef == e0) | (lanef == e1), 1.0, 0.0)
    prefix = jnp.dot(tri_ref[...], onehot.astype(BF16), preferred_element_type=F32) + cnt_ref[...]
    r0 = jnp.sum(jnp.where(lanef == e0, prefix, 0.0), -1, keepdims=True)
    r1 = jnp.sum(jnp.where(lanef == e1, prefix, 0.0), -1, keepdims=True)
    cnt_ref[...] += jnp.sum(onehot, 0, keepdims=True)
    rf_ref[...] = jnp.where(lane == 0, gate0, jnp.where(lane == 1, gate1, 0.0))
    ri = jnp.where(lane == 0, e0, jnp.where(lane == 1, e1, jnp.where(lane == 2, r0, jnp.where(lane == 3, r1, 0.0))))
    ri_ref[...] = ri.astype(jnp.int32)


def _post_mixer(yf, yb, g, bvg, att, h, gn_g, gn_b, w_out, ln_g, ln_b, w_group, b_group, w_expert, b_expert, alpha):
    T, D = h.shape
    tm = _row_tile(T)
    vec = lambda a: a.reshape(1, -1)
    zpad = LANES - N_GROUPS - N_EXPERTS
    wr = jnp.concatenate([w_group, w_expert, jnp.zeros((D, zpad), F32)], axis=1)
    wrh = wr.astype(BF16)
    wrl = (wr - wrh.astype(F32)).astype(BF16)
    rb = jnp.concatenate([b_group, b_expert, jnp.zeros((zpad,), F32)]).reshape(1, LANES)
    tri = jnp.asarray(np.arange(tm)[:, None] > np.arange(tm)[None, :], BF16)
    consts = [vec(gn_g), vec(gn_b), w_out.astype(BF16), vec(ln_g), vec(ln_b), wrh, wrl, rb, tri, _seg_ones(RW)]
    row = lambda w: pl.BlockSpec((tm, w), lambda i: (i, 0))
    full = lambda a: pl.BlockSpec(a.shape, lambda i: (0, 0))
    return pl.pallas_call(
        functools.partial(_post_kernel, alpha=alpha, tm=tm),
        out_shape=(jax.ShapeDtypeStruct((T, D), F32),
                   jax.ShapeDtypeStruct((T, LANES), F32),
                   jax.ShapeDtypeStruct((T, LANES), jnp.int32),
                   jax.ShapeDtypeStruct((1, LANES), F32)),
        grid=(T // tm,),
        in_specs=[row(RW)] * 4 + [row(H_A * V_DIM), row(D)] + [full(c) for c in consts],
        out_specs=(row(D), row(LANES), row(LANES), pl.BlockSpec((1, LANES), lambda i: (0, 0))),
        compiler_params=_cparams(("arbitrary",)),
        name="post_mixer",
    )(yf, yb, g, bvg, att, h, *consts)


def _row_copies(dest_smem, t, make):
    return [make(s, dest_smem[TOP_K * t + s]) for s in range(TOP_K)]


def _load_dest(dest_hbm, dest_smem, isem, tm):
    load = pltpu.make_async_copy(dest_hbm.at[pl.ds(pl.program_id(0) * tm * TOP_K, tm * TOP_K)], dest_smem, isem)
    load.start()
    load.wait()


def _dispatch_kernel(dest_hbm, h_ref, xz_hbm, xb_hbm, dest_smem, isem, sem, *, tm):
    del xz_hbm
    _load_dest(dest_hbm, dest_smem, isem, tm)

    def copies(t):
        src = h_ref.at[pl.ds(t, 1)]
        return _row_copies(dest_smem, t, lambda s, dst: pltpu.make_async_copy(src, xb_hbm.at[pl.ds(dst, 1)], sem))

    def start(t, c):
        for cp in copies(t):
            cp.start()
        return c

    def wait(t, c):
        for cp in copies(t):
            cp.wait()
        return c

    lax.fori_loop(0, tm, start, 0, unroll=8)
    lax.fori_loop(0, tm, wait, 0, unroll=8)


def _dispatch(dest, h, n_rows):
    T, D = h.shape
    tm = _row_tile(T)
    any_spec = pl.BlockSpec(memory_space=pl.ANY)
    return pl.pallas_call(
        functools.partial(_dispatch_kernel, tm=tm),
        out_shape=jax.ShapeDtypeStruct((n_rows, D), F32),
        grid=(T // tm,),
        in_specs=[any_spec, pl.BlockSpec((tm, D), lambda i: (i, 0)), any_spec],
        out_specs=any_spec,
        scratch_shapes=[pltpu.SMEM((tm * TOP_K,), jnp.int32), pltpu.SemaphoreType.DMA, pltpu.SemaphoreType.DMA],
        input_output_aliases={2: 0},
        compiler_params=_cparams(("arbitrary",)),
        name="moe_dispatch",
    )(dest, h, jnp.zeros((n_rows, D), F32))


def _expert_kernel(be_ref, nu_ref, x_ref, w1_ref, w3_ref, w2_ref, o_ref):
    @pl.when(pl.program_id(0) < nu_ref[0])
    def _():
        x = x_ref[...].astype(BF16)
        a = jnp.dot(x, w1_ref[0], preferred_element_type=F32)
        b = jnp.dot(x, w3_ref[0], preferred_element_type=F32)
        hid = (a * jax.nn.sigmoid(a)) * b
        o_ref[...] = jnp.dot(hid.astype(BF16), w2_ref[0], preferred_element_type=F32)


def _expert_ffn(xb, block_e, n_used, w1, w3, w2):
    n_rows, D = xb.shape
    n_blocks = n_rows // MOE_BLOCK
    blk = lambda i, be, nu: jnp.minimum(i, nu[0] - 1)
    wmap = lambda i, be, nu: (be[blk(i, be, nu)], 0, 0)
    grid_spec = pltpu.PrefetchScalarGridSpec(
        num_scalar_prefetch=2,
        grid=(n_blocks,),
        in_specs=[pl.BlockSpec((MOE_BLOCK, D), lambda i, be, nu: (blk(i, be, nu), 0)),
                  pl.BlockSpec((1, D, E_HID), wmap),
                  pl.BlockSpec((1, D, E_HID), wmap),
                  pl.BlockSpec((1, E_HID, D), wmap)],
        out_specs=pl.BlockSpec((MOE_BLOCK, D), lambda i, be, nu: (blk(i, be, nu), 0)),
    )
    return pl.pallas_call(
        _expert_kernel,
        out_shape=jax.ShapeDtypeStruct((n_rows, D), F32),
        grid_spec=grid_spec,
        compiler_params=_cparams(("arbitrary",)),
        name="moe_experts",
    )(block_e, n_used, xb, w1, w3, w2)


def _combine_kernel(dest_hbm, yb_hbm, gate_ref, h_ref, g_ref, b_ref, o_ref, dest_smem, buf, isem, sem, *, tm, alpha):
    _load_dest(dest_hbm, dest_smem, isem, tm)

    def copies(t):
        return _row_copies(dest_smem, t, lambda s, src: pltpu.make_async_copy(
            yb_hbm.at[pl.ds(src, 1)], buf.at[s, pl.ds(t, 1)], sem))

    def start(t, c):
        for cp in copies(t):
            cp.start()
        return c

    def wait(t, c):
        for cp in copies(t):
            cp.wait()
        return c

    lax.fori_loop(0, tm, start, 0, unroll=8)
    lax.fori_loop(0, tm, wait, 0, unroll=8)
    gate = gate_ref[...]
    ff = sum(buf[s] * gate[:, s:s + 1] for s in range(TOP_K))
    o_ref[...] = _ln(h_ref[...] * alpha + ff, g_ref[...], b_ref[...])


def _combine(dest, yb, route_f, h, ln_g, ln_b, alpha):
    T, D = h.shape
    tm = _row_tile(T)
    any_spec = pl.BlockSpec(memory_space=pl.ANY)
    row = lambda w: pl.BlockSpec((tm, w), lambda i: (i, 0))
    vec = pl.BlockSpec((1, D), lambda i: (0, 0))
    return pl.pallas_call(
        functools.partial(_combine_kernel, tm=tm, alpha=alpha),
        out_shape=jax.ShapeDtypeStruct((T, D), F32),
        grid=(T // tm,),
        in_specs=[any_spec, any_spec, row(LANES), row(D), vec, vec],
        out_specs=row(D),
        scratch_shapes=[pltpu.SMEM((tm * TOP_K,), jnp.int32), pltpu.VMEM((TOP_K, tm, D), F32),
                        pltpu.SemaphoreType.DMA, pltpu.SemaphoreType.DMA],
        compiler_params=_cparams(("arbitrary",)),
        name="moe_combine",
    )(dest, yb, route_f, h, ln_g.reshape(1, D), ln_b.reshape(1, D))


def _hier_moe(h, route_f, route_i, counts, w1, w3, w2, ln_g, ln_b, alpha):
    T, D = h.shape
    counts = counts[0, :N_EXPERTS].astype(jnp.int32)
    padded = (counts + MOE_BLOCK - 1) // MOE_BLOCK * MOE_BLOCK
    pends = jnp.cumsum(padded)
    pstart = pends - padded
    n_rows = (T * TOP_K + N_EXPERTS * (MOE_BLOCK - 1) + MOE_BLOCK - 1) // MOE_BLOCK * MOE_BLOCK
    n_blocks = n_rows // MOE_BLOCK
    block_start = jnp.arange(n_blocks, dtype=jnp.int32) * MOE_BLOCK
    block_e = jnp.minimum(jnp.sum(pends[None, :] <= block_start[:, None], axis=1), N_EXPERTS - 1).astype(jnp.int32)
    n_used = (pends[-1:] // MOE_BLOCK).astype(jnp.int32)
    dest = (jnp.take(pstart, route_i[:, :TOP_K]) + route_i[:, TOP_K:2 * TOP_K]).reshape(-1)
    xb = _dispatch(dest, h, n_rows)
    yb = _expert_ffn(xb, block_e, n_used, w1, w3, w2)
    return _combine(dest, yb, route_f, h, ln_g, ln_b, alpha)


def kernel(x, positions, meta_tokens, emb_ln_g, emb_ln_b, w_in, mla_q_norm, mla_kv_norm, mla_w_uq, mla_w_ukv, mla_out_norm, rwkv_mu_prev, rwkv_mu_next, rwkv_w0, rwkv_w2, rwkv_a0, rwkv_a2, rwkv_g2, rwkv_k_k, rwkv_k_a, rwkv_r_k, rwkv_gn_g, rwkv_gn_b, rwkv_v0, rwkv_v1, rwkv_v2, w_out, ln1_g, ln1_b, moe_w_group, moe_b_group, moe_w_expert, moe_b_expert, moe_w1, moe_w3, moe_w2, ln2_g, ln2_b):
    B, seq, D = x.shape
    depth = w_in.shape[0]
    alpha = (2 * depth) ** 0.25
    L = seq + N_META
    Lp = -(-L // LANES) * LANES
    T = B * Lp
    nb = next(n for n in (4, 2, 1) if B % n == 0)
    tk = 384 if Lp % 384 == 0 else LANES
    tq = next(t for t in (1408, 1152, 768, 384, LANES) if Lp % t == 0)

    meta = jnp.broadcast_to(meta_tokens.astype(x.dtype)[None], (B, N_META, D))
    h = jnp.concatenate([meta, x, jnp.zeros((B, Lp - L, D), x.dtype)], axis=1).reshape(T, D)
    h = _ln_residual(h, jnp.zeros_like(h), emb_ln_g, emb_ln_b, 1.0)

    key_bias = jnp.where(jnp.arange(Lp) < L, 0.0, NEG).astype(F32)[None, :]
    pos = jnp.concatenate([jnp.broadcast_to(jnp.arange(N_META, dtype=jnp.int32), (B, N_META)),
                           positions + N_META, jnp.zeros((B, Lp - L), jnp.int32)], axis=1)
    inv_freq = ROPE_THETA ** (-jnp.arange(0, QK_ROPE, 2, dtype=F32) / QK_ROPE)
    ang = pos.astype(F32)[..., None] * inv_freq
    cos, sin = jnp.cos(ang).reshape(T, -1), jnp.sin(ang).reshape(T, -1)
    zeros = jnp.zeros((T, LANES - QK_ROPE), F32)
    rope_cos = jnp.concatenate([cos, cos, zeros], axis=1)
    rope_sin = jnp.concatenate([-sin, sin, zeros], axis=1)
    qscale = (QK_NOPE + QK_ROPE) ** -0.5 * math.log2(math.e)

    v_first = None
    for li in range(depth):
        q, k, v, rw = _project(h, w_in[li], mla_q_norm[li], mla_kv_norm[li], mla_w_uq[li], mla_w_ukv[li],
                               rope_cos, rope_sin, qscale)
        y_att = _attention(q.reshape(B, Lp, -1), k.reshape(B, Lp, -1), v.reshape(B, Lp, -1),
                           key_bias, mla_out_norm[li], tq, tk)
        vres = None if li == 0 else (rwkv_v0[li - 1], rwkv_v1[li - 1], rwkv_v2[li - 1])
        outs = _rwkv_prep(rw.reshape(B, Lp, -1), L, rwkv_mu_prev[li], rwkv_mu_next[li], rwkv_w0[li], rwkv_w2[li],
                          rwkv_a0[li], rwkv_a2[li], rwkv_g2[li], rwkv_k_k[li], rwkv_k_a[li], rwkv_r_k[li],
                          v_first, vres)
        kk, vm, g, bvg, vt = outs[0], outs[1], outs[14], outs[15], outs[16]
        if li == 0:
            v_first = outs[17]
        y_f, y_b = _wkv_bidir(kk, vm, vt, (outs[2:8], outs[8:14]), nb)
        flat = lambda a: a.reshape(T, -1)
        h, route_f, route_i, counts = _post_mixer(
            flat(y_f), flat(y_b), flat(g), flat(bvg), flat(y_att), h, rwkv_gn_g[li], rwkv_gn_b[li], w_out[li],
            ln1_g[li], ln1_b[li], moe_w_group[li], moe_b_group[li], moe_w_expert[li], moe_b_expert[li], alpha)
        h = _hier_moe(h, route_f, route_i, counts,
                      moe_w1[li].astype(BF16), moe_w3[li].astype(BF16), moe_w2[li].astype(BF16),
                      ln2_g[li], ln2_b[li], alpha)
    return h.reshape(B, Lp, D)[:, N_META:L]
```

```python
import functools
import math

import numpy as np
import jax
import jax.numpy as jnp
from jax import lax
from jax.experimental import pallas as pl
from jax.experimental.pallas import tpu as pltpu

F32 = jnp.float32
BF16 = jnp.bfloat16

N_META = 16
H_A = 4
QK_NOPE = 128
QK_ROPE = 64
V_DIM = 128
Q_LORA = 256
KV_LORA = 128
ROPE_THETA = 10000.0
H_B = 8
N_B = 64
RW = H_B * N_B
DECAY_LORA = 64
AAA_LORA = 64
GATE_LORA = 128
VRES_LORA = 32
RWKV_GN_EPS = 64e-5
MLA_COLS = Q_LORA + KV_LORA + QK_ROPE
N_GROUPS = 4
EXPERTS_PER_GROUP = 8
N_EXPERTS = N_GROUPS * EXPERTS_PER_GROUP
TOP_K = 2
E_HID = 256
MOE_BLOCK = 256
LN_EPS = 1e-5
RMS_EPS = 1e-6

LANES = 128
SUBLANES = 8
TIME_BLOCK = 128
SCAN_CHUNK = 64
GROUPS_PER_ITER = 4
ROW_TILE = 512
VMEM_LIMIT = 56 * 1024 * 1024
HEAD_W = 2 * LANES
NEG = -1e30


def _cparams(sem):
    return pltpu.CompilerParams(dimension_semantics=sem, vmem_limit_bytes=VMEM_LIMIT)


def _row_tile(m):
    return next(t for t in (ROW_TILE, 384, 256, LANES) if m % t == 0)


def _split_bf16(x):
    hi = x.astype(BF16)
    lo = (x - hi.astype(F32)).astype(BF16)
    return hi, lo


def _seg_sum(x, ones):
    hi, lo = _split_bf16(x)
    return jnp.dot(jnp.concatenate([hi, lo], axis=-1), ones, preferred_element_type=F32)


def _seg_ones(width):
    m = np.arange(2 * width)[:, None] % width
    n = np.arange(width)[None, :]
    return jnp.asarray(m // N_B == n // N_B, BF16)


def _ln(x, g, b):
    mu = jnp.mean(x, -1, keepdims=True)
    xc = x - mu
    var = jnp.mean(xc * xc, -1, keepdims=True)
    return xc * lax.rsqrt(var + LN_EPS) * g + b


def _ln_kernel(x_ref, r_ref, g_ref, b_ref, o_ref, *, alpha):
    o_ref[...] = _ln(x_ref[...] * alpha + r_ref[...], g_ref[...], b_ref[...])


def _ln_residual(x, r, g, b, alpha):
    M, D = x.shape
    tm = _row_tile(M)
    row = pl.BlockSpec((tm, D), lambda i: (i, 0))
    vec = pl.BlockSpec((1, D), lambda i: (0, 0))
    return pl.pallas_call(
        functools.partial(_ln_kernel, alpha=alpha),
        out_shape=jax.ShapeDtypeStruct((M, D), F32),
        grid=(M // tm,),
        in_specs=[row, row, vec, vec],
        out_specs=row,
        compiler_params=_cparams(("parallel",)),
        name="layer_norm",
    )(x, r, g.reshape(1, D), b.reshape(1, D))


def _rms(x, g):
    return x * lax.rsqrt(jnp.mean(x * x, -1, keepdims=True) + RMS_EPS) * g


def _proj_kernel(h_ref, win_ref, qn_ref, kvn_ref, wuq_ref, wukv_ref, c_ref, s_ref,
                 q_ref, k_ref, v_ref, rw_ref, *, qscale):
    proj = jnp.dot(h_ref[...].astype(BF16), win_ref[...], preferred_element_type=F32)
    rw_ref[...] = proj[:, MLA_COLS + N_B:]
    q = jnp.dot(_rms(proj[:, :Q_LORA], qn_ref[...]).astype(BF16), wuq_ref[...], preferred_element_type=F32)
    kv = jnp.dot(_rms(proj[:, Q_LORA:Q_LORA + KV_LORA], kvn_ref[...]).astype(BF16), wukv_ref[...],
                 preferred_element_type=F32)
    cos, sin = c_ref[...], s_ref[...]
    lane = lax.broadcasted_iota(jnp.int32, cos.shape, 1)
    half = QK_ROPE // 2

    def rope(x):
        partner = jnp.where(lane < half, pltpu.roll(x, LANES - half, 1), pltpu.roll(x, half, 1))
        return x * cos + partner * sin

    k_pe = rope(proj[:, Q_LORA + KV_LORA:Q_LORA + KV_LORA + LANES]).astype(BF16)
    for h in range(H_A):
        o = h * HEAD_W
        q_ref[:, o:o + LANES] = (q[:, o:o + LANES] * qscale).astype(BF16)
        q_ref[:, o + LANES:o + HEAD_W] = (rope(q[:, o + LANES:o + HEAD_W]) * qscale).astype(BF16)
        k_ref[:, o:o + LANES] = kv[:, o:o + LANES].astype(BF16)
        k_ref[:, o + LANES:o + HEAD_W] = k_pe
        v_ref[:, h * V_DIM:(h + 1) * V_DIM] = kv[:, o + LANES:o + HEAD_W].astype(BF16)


def _project(h, w_in, q_norm, kv_norm, w_uq, w_ukv, rope_cos, rope_sin, qscale):
    T, D = h.shape
    tm = _row_tile(T)
    n_rw = w_in.shape[1] - MLA_COLS
    pad = jnp.zeros((D, N_B), F32)
    win = jnp.concatenate([w_in[:, :MLA_COLS], pad, w_in[:, MLA_COLS:]], axis=1).astype(BF16)
    wq = w_uq.reshape(Q_LORA, H_A, QK_NOPE + QK_ROPE)
    wq = jnp.concatenate([wq, jnp.zeros((Q_LORA, H_A, HEAD_W - QK_NOPE - QK_ROPE), F32)], axis=-1)
    wq = wq.reshape(Q_LORA, H_A * HEAD_W).astype(BF16)
    row = lambda w: pl.BlockSpec((tm, w), lambda i: (i, 0))
    full = lambda a: pl.BlockSpec(a.shape, lambda i: (0, 0))
    args = (h, win, q_norm.reshape(1, -1), kv_norm.reshape(1, -1), wq, w_ukv.astype(BF16), rope_cos, rope_sin)
    return pl.pallas_call(
        functools.partial(_proj_kernel, qscale=qscale),
        out_shape=(jax.ShapeDtypeStruct((T, H_A * HEAD_W), BF16),
                   jax.ShapeDtypeStruct((T, H_A * HEAD_W), BF16),
                   jax.ShapeDtypeStruct((T, H_A * V_DIM), BF16),
                   jax.ShapeDtypeStruct((T, n_rw), F32)),
        grid=(T // tm,),
        in_specs=[row(D)] + [full(a) for a in args[1:6]] + [row(LANES), row(LANES)],
        out_specs=(row(H_A * HEAD_W), row(H_A * HEAD_W), row(H_A * V_DIM), row(n_rw)),
        compiler_params=_cparams(("parallel",)),
        name="in_proj",
    )(*args)


def _attn_kernel(q_ref, k_ref, v_ref, bias_ref, g_ref, o_ref, m_ref, acc_ref, s_ref, p_ref, a_ref, *, tk):
    q = q_ref[0]
    n = k_ref.shape[1] // tk
    one_col = (lax.broadcasted_iota(jnp.int32, (tk, V_DIM), 1) == 0).astype(BF16)

    def scores(c):
        off = pl.multiple_of(c * tk, tk)
        s = lax.dot_general(q, k_ref[0, pl.ds(off, tk), :], (((1,), (1,)), ((), ())), preferred_element_type=F32)
        return s + bias_ref[:, pl.ds(off, tk)]

    def weighted_values(c):
        off = pl.multiple_of(c * tk, tk)
        v1 = jnp.concatenate([v_ref[0, pl.ds(off, tk), :], one_col], axis=1)
        acc_ref[...] = a_ref[...] * acc_ref[...] + jnp.dot(p_ref[...], v1, preferred_element_type=F32)

    m_ref[...] = jnp.full_like(m_ref, NEG)
    acc_ref[...] = jnp.zeros_like(acc_ref)
    p_ref[...] = jnp.zeros_like(p_ref)
    a_ref[...] = jnp.ones_like(a_ref)
    s_ref[...] = scores(0)

    def body(c, carry):
        weighted_values(jnp.maximum(c - 1, 0))
        s = s_ref[...]
        s_ref[...] = scores(jnp.minimum(c + 1, n - 1))
        m_prev = m_ref[...]
        m_new = jnp.maximum(m_prev, jnp.max(s, -1, keepdims=True))
        p_ref[...] = jnp.exp2(s - m_new).astype(BF16)
        a_ref[...] = jnp.exp2(m_prev - m_new)
        m_ref[...] = m_new
        return carry

    lax.fori_loop(0, n, body, 0)
    weighted_values(n - 1)
    acc = acc_ref[...]
    o = acc[:, :V_DIM] / acc[:, V_DIM:V_DIM + 1]
    o = o * lax.rsqrt(jnp.mean(o * o, -1, keepdims=True) + RMS_EPS) * g_ref[...]
    o_ref[0] = o.astype(BF16)


def _attention(q, k, v, bias, out_gain, tq, tk):
    B, Lp, _ = q.shape
    return pl.pallas_call(
        functools.partial(_attn_kernel, tk=tk),
        out_shape=jax.ShapeDtypeStruct((B, Lp, H_A * V_DIM), BF16),
        grid=(B, H_A, Lp // tq),
        in_specs=[pl.BlockSpec((1, tq, HEAD_W), lambda b, h, i: (b, i, h)),
                  pl.BlockSpec((1, Lp, HEAD_W), lambda b, h, i: (b, 0, h)),
                  pl.BlockSpec((1, Lp, V_DIM), lambda b, h, i: (b, 0, h)),
                  pl.BlockSpec((1, Lp), lambda b, h, i: (0, 0)),
                  pl.BlockSpec((1, V_DIM), lambda b, h, i: (0, h))],
        out_specs=pl.BlockSpec((1, tq, V_DIM), lambda b, h, i: (b, i, h)),
        scratch_shapes=[pltpu.VMEM((tq, 1), F32), pltpu.VMEM((tq, 2 * V_DIM), F32),
                        pltpu.VMEM((tq, tk), F32), pltpu.VMEM((tq, tk), BF16), pltpu.VMEM((tq, 1), F32)],
        compiler_params=_cparams(("parallel", "parallel", "parallel")),
        name="mla_attention",
    )(q, k, v, bias, out_gain.reshape(1, H_A * V_DIM))


def _prep_kernel(*refs, first, seq_len, tm):
    if first:
        (rw_ref, pv_ref, nx_ref, mup_ref, mun_ref, w2_ref, w0_ref, a2_ref, a0_ref, g2_ref, kk_ref, ka_ref, rk_ref,
         ones_ref, tri_ref, kk_o, v_o, wf_o, kdf_o, bf_o, rbf_o, ktf_o, btf_o, wb_o, kdb_o, bb_o, rbb_o, ktb_o, btb_o,
         g_o, bvg_o, vt_o, vfirst_o) = refs
    else:
        (rw_ref, pv_ref, nx_ref, mup_ref, mun_ref, w2_ref, w0_ref, a2_ref, a0_ref, g2_ref, kk_ref, ka_ref, rk_ref,
         ones_ref, tri_ref, vf_ref, v0_ref, v1_ref, v2_ref,
         kk_o, v_o, wf_o, kdf_o, bf_o, rbf_o, ktf_o, btf_o, wb_o, kdb_o, bb_o, rbb_o, ktb_o, btb_o, g_o, bvg_o,
         vt_o) = refs
    i = pl.program_id(1)
    n_t = pl.num_programs(1)
    row = lax.broadcasted_iota(jnp.int32, (tm, 1), 0)
    t = i * tm + row
    valid = t < seq_len
    rw = jnp.where(valid, rw_ref[0], 0.0)
    prev_row = jnp.where((i > 0) & (i * tm - 1 < seq_len), pv_ref[0, SUBLANES - 1:SUBLANES, :], 0.0)
    next_row = jnp.where((i < n_t - 1) & ((i + 1) * tm < seq_len), nx_ref[0, 0:1, :], 0.0)
    prev = jnp.where(row == 0, prev_row, pltpu.roll(rw, 1, 0))
    nxt = jnp.where(row == tm - 1, next_row, pltpu.roll(rw, tm - 1, 0))
    u = rw + mup_ref[...] * (prev - rw) + mun_ref[...] * (nxt - rw)
    r, k, v = u[:, :RW], u[:, RW:2 * RW], u[:, 2 * RW:3 * RW]
    wd = u[:, 3 * RW:3 * RW + LANES]
    ad = u[:, 3 * RW + LANES:3 * RW + 2 * LANES]
    gd = u[:, 3 * RW + 2 * LANES:]
    ones = ones_ref[...]
    if first:
        vfirst_o[0] = v
    else:
        low = jnp.dot(v.astype(BF16), v1_ref[...], preferred_element_type=F32)
        mix = jax.nn.sigmoid(v0_ref[...] + jnp.dot(low.astype(BF16), v2_ref[...], preferred_element_type=F32))
        v = v + (vf_ref[0] - v) * mix
    g = jnp.dot(jax.nn.sigmoid(gd).astype(BF16), g2_ref[...], preferred_element_type=F32)
    kk = k * kk_ref[...]
    kk = jnp.where(valid, kk * lax.rsqrt(_seg_sum(kk * kk, ones) + 1e-12), 0.0)
    wl = w0_ref[...] + jnp.dot(jnp.tanh(wd).astype(BF16), w2_ref[...], preferred_element_type=F32)
    logw = -math.exp(-0.5) * jax.nn.sigmoid(wl)
    decay = jnp.exp(logw)
    a = jax.nn.sigmoid(a0_ref[...] + jnp.dot(ad.astype(BF16), a2_ref[...], preferred_element_type=F32))
    ka = ka_ref[...]
    kd_f = jnp.where(valid, k * (1.0 + (a[:, :RW] - 1.0) * ka), 0.0)
    kd_b = jnp.where(valid, k * (1.0 + (a[:, RW:] - 1.0) * ka), 0.0)
    bonus = _seg_sum(r * (kd_f + kd_b) * rk_ref[...], ones)
    b_f, b_b = kk * a[:, :RW], kk * a[:, RW:]
    cum_f = _cumsum3(tri_ref[0], logw[:, :RW])
    cum_b = _cumsum3(tri_ref[1], logw[:, RW:])
    p_f, ip_f = jnp.exp(cum_f), jnp.exp(-cum_f)
    p_b, ip_b = jnp.exp(cum_b), jnp.exp(-cum_b)
    kk_o[0] = kk
    vm = jnp.where(valid, v, 0.0)
    v_o[0] = vm.astype(BF16)
    lo_half = lax.broadcasted_iota(jnp.int32, (N_B, LANES), 1) < N_B
    for tb in range(tm // TIME_BLOCK):
        for hp in range(H_B // 2):
            xt = vm[tb * TIME_BLOCK:(tb + 1) * TIME_BLOCK, hp * LANES:(hp + 1) * LANES].T
            top, bot = xt[:N_B], xt[N_B:]
            vt_o[0, tb, hp, 0] = jnp.where(lo_half, top, pltpu.roll(bot, N_B, 1))
            vt_o[0, tb, hp, 1] = jnp.where(lo_half, pltpu.roll(top, N_B, 1), bot)
    wf_o[0] = decay[:, :RW]
    wb_o[0] = decay[:, RW:]
    kdf_o[0] = kd_f
    kdb_o[0] = kd_b
    bf_o[0] = b_f
    bb_o[0] = b_b
    rbf_o[0] = r * p_f
    rbb_o[0] = r * p_b
    ktf_o[0] = (kd_f * ip_f).astype(BF16)
    ktb_o[0] = (kd_b * ip_b).astype(BF16)
    btf_o[0] = (b_f * ip_f).astype(BF16)
    btb_o[0] = (b_b * ip_b).astype(BF16)
    g_o[0] = g
    bvg_o[0] = bonus * v * g


def _cumsum3(tri, x):
    h1 = x.astype(BF16)
    r1 = x - h1.astype(F32)
    h2 = r1.astype(BF16)
    h3 = (r1 - h2.astype(F32)).astype(BF16)
    return (jnp.dot(tri, h1, preferred_element_type=F32) + jnp.dot(tri, h2, preferred_element_type=F32)
            + jnp.dot(tri, h3, preferred_element_type=F32))


def _chunk_tri(tm):
    t = np.arange(tm)
    same = (t[:, None] // SCAN_CHUNK) == (t[None, :] // SCAN_CHUNK)
    return jnp.asarray(np.stack([same & (t[None, :] <= t[:, None]), same & (t[None, :] >= t[:, None])]), BF16)


def _block_diag2(a, b):
    z = jnp.zeros_like(a)
    return jnp.concatenate([jnp.concatenate([a, z], 1), jnp.concatenate([z, b], 1)], 0)


def _rwkv_prep(rw, seq_len, mu_prev, mu_next, w0, w2, a0, a2, g2, k_k, k_a, r_k, v_first, vres):
    B, Lp, n_rw = rw.shape
    tm = _row_tile(Lp)
    tpb = tm // SUBLANES
    first = vres is None
    vec = lambda a: a.reshape(1, -1)
    consts = [vec(mu_prev), vec(mu_next), _block_diag2(w2[0], w2[1]).astype(BF16), vec(w0),
              _block_diag2(a2[0], a2[1]).astype(BF16), vec(a0), g2.astype(BF16), vec(k_k), vec(k_a), vec(r_k),
              _seg_ones(RW), _chunk_tri(tm)]
    tile = lambda w: pl.BlockSpec((1, tm, w), lambda b, i: (b, i, 0))
    full = lambda a: pl.BlockSpec(a.shape, lambda b, i: (0,) * a.ndim)
    in_specs = [tile(n_rw),
                pl.BlockSpec((1, SUBLANES, n_rw), lambda b, i: (b, jnp.maximum(i * tpb - 1, 0), 0)),
                pl.BlockSpec((1, SUBLANES, n_rw), lambda b, i: (b, jnp.minimum((i + 1) * tpb, Lp // SUBLANES - 1), 0))]
    in_specs += [full(c) for c in consts]
    args = [rw, rw, rw] + consts
    if not first:
        v0, v1, v2 = vres
        v1p = jnp.concatenate([v1, jnp.zeros((RW, LANES - VRES_LORA), F32)], 1).astype(BF16)
        v2p = jnp.concatenate([v2, jnp.zeros((LANES - VRES_LORA, RW), F32)], 0).astype(BF16)
        extra = [vec(v0), v1p, v2p]
        in_specs += [tile(RW)] + [full(c) for c in extra]
        args += [v_first] + extra
    vt_shape = (B, Lp // TIME_BLOCK, H_B // 2, 2, N_B, LANES)
    vt_spec = pl.BlockSpec((1, tm // TIME_BLOCK) + vt_shape[2:], lambda b, i: (b, i, 0, 0, 0, 0))
    shapes = [jax.ShapeDtypeStruct((B, Lp, RW), BF16 if i in (1, 6, 7, 12, 13) else F32) for i in range(16)]
    shapes.append(jax.ShapeDtypeStruct(vt_shape, F32))
    specs = [tile(RW)] * 16 + [vt_spec]
    if first:
        shapes.append(jax.ShapeDtypeStruct((B, Lp, RW), F32))
        specs.append(tile(RW))
    out = pl.pallas_call(
        functools.partial(_prep_kernel, first=first, seq_len=seq_len, tm=tm),
        out_shape=tuple(shapes),
        grid=(B, Lp // tm),
        in_specs=in_specs,
        out_specs=tuple(specs),
        compiler_params=_cparams(("parallel", "parallel")),
        name="rwkv_prep",
    )(*args)
    return out


def _scan_kernel(kkf, wf, kdf, bf, vf, rbf, ktf, btf, vtf, kkb, wb, kdb, bb_, vb_, rbb, ktb, btb, vtb, ones_ref,
                 yf_ref, yb_ref, s_ref, s0_ref, u_ref, *, nb):
    j = pl.program_id(1)
    C = nb * 4
    half_t = TIME_BLOCK // 2
    assert half_t == SCAN_CHUNK

    @pl.when(j == 0)
    def _():
        s_ref[...] = jnp.zeros_like(s_ref)

    lane1 = lax.broadcasted_iota(jnp.int32, (N_B, LANES), 1)
    lo_half = lane1 < N_B

    ones = ones_ref[...]
    lane_id = lax.broadcasted_iota(jnp.int32, (C * N_B, LANES), 1)
    lane = lane_id % N_B
    head_base = (lane_id // N_B) * N_B
    row_refs = ((kkf, wf, kdf, bf), (kkb, wb, kdb, bb_))
    chunk_refs = ((vf, rbf, ktf, btf, yf_ref), (vb_, rbb, ktb, btb, yb_ref))
    vt_refs = (vtf, vtb)
    tr = lax.broadcasted_iota(jnp.int32, (N_B, LANES), 0)
    tc = lane1 % N_B
    earlier2 = (tc <= tr, tc >= tr)
    lanes_dims = ((1,), (1,))

    def step(d, tiles, row, oh, tt):
        def rows(tile):
            return jnp.concatenate(
                [jnp.broadcast_to(tile[n, row:row + 1, hp * LANES:(hp + 1) * LANES], (N_B, LANES))
                 for n in range(nb) for hp in range(4)], axis=0)

        kk, w, kd, b = [rows(x) for x in tiles]
        S = s_ref[d]
        sa = jnp.dot((S * kk).astype(BF16), ones[:LANES], preferred_element_type=F32)
        vt = jnp.concatenate([vt_refs[d][n, 0, hp, oh] for n in range(nb) for hp in range(4)], axis=0)
        vb = jnp.take_along_axis(vt, head_base + tt, axis=1, mode="promise_in_bounds")
        s_ref[d] = S * w - sa * b + vb * kd
        u_ref[d] = jnp.where(lane == tt, sa, u_ref[d])

    def make_body(half):
        def body(g2, carry):
            for sub in range(GROUPS_PER_ITER):
                g = g2 * GROUPS_PER_ITER + sub
                t0f = pl.multiple_of(half * half_t + g * SUBLANES, SUBLANES)
                t0b = pl.multiple_of(TIME_BLOCK - SUBLANES - (half * half_t + g * SUBLANES), SUBLANES)
                tiles_f = [x[:, pl.ds(t0f, SUBLANES), :] for x in row_refs[0]]
                tiles_b = [x[:, pl.ds(t0b, SUBLANES), :] for x in row_refs[1]]
                for i in range(SUBLANES):
                    step(0, tiles_f, i, half, g * SUBLANES + i)
                    step(1, tiles_b, SUBLANES - 1 - i, 1 - half, half_t - 1 - (g * SUBLANES + i))
            return carry
        return body

    def chunk_outputs(d, oh):
        vref, rref, ktref, btref, yref = chunk_refs[d]
        r0 = oh * half_t
        pairs = [(n, hp) for n in range(nb) for hp in range(4)]

        def blk(ref, n, hp):
            return ref[n, r0:r0 + half_t, hp * LANES:(hp + 1) * LANES]

        def stage1(n, hp):
            c = n * 4 + hp
            rb, kt, bt = blk(rref, n, hp), blk(ktref, n, hp), blk(btref, n, hp)
            s0h = s0_ref[d, c * N_B:(c + 1) * N_B, :].astype(BF16)
            r2 = jnp.concatenate([jnp.where(lo_half, rb, 0.0), jnp.where(lo_half, 0.0, rb)], axis=0)
            r2h, r2l = _split_bf16(r2)
            rhs = jnp.concatenate([kt, bt, s0h, s0h], axis=0)
            return lax.dot_general(jnp.concatenate([r2h, r2l], axis=1), jnp.concatenate([rhs, rhs], axis=1),
                                   (lanes_dims, ((), ())), preferred_element_type=F32)

        def stage2(n, hp, gb):
            c = n * 4 + hp
            vh = blk(vref, n, hp)
            ut = u_ref[d, c * N_B:(c + 1) * N_B, :]
            u2 = jnp.concatenate([ut, ut], axis=0).T
            per_head = []
            for h in range(2):
                gh = gb[h * N_B:(h + 1) * N_B]
                gh_, gl_ = _split_bf16(jnp.where(earlier2[d], gh[:, :LANES], 0.0))
                w = jnp.concatenate([vh, (-u2[h * N_B:(h + 1) * N_B]).astype(BF16)], axis=0)
                corr = jnp.dot(jnp.concatenate([gh_, gl_], axis=1), jnp.concatenate([w, w], axis=0),
                               preferred_element_type=F32)
                per_head.append(gh[:, LANES:] + corr)
            yref[n, r0:r0 + half_t, hp * LANES:(hp + 1) * LANES] = jnp.where(lo_half, per_head[0], per_head[1])

        ahead = 6
        gbs = [stage1(*p) for p in pairs[:ahead]]
        for i, p in enumerate(pairs):
            if i + ahead < len(pairs):
                gbs.append(stage1(*pairs[i + ahead]))
            stage2(*p, gbs[i])

    for half in range(2):
        s0_ref[...] = s_ref[...]
        u_ref[...] = jnp.zeros_like(u_ref)
        lax.fori_loop(0, half_t // (SUBLANES * GROUPS_PER_ITER), make_body(half), 0)
        chunk_outputs(0, half)
        chunk_outputs(1, 1 - half)


def _wkv_bidir(kk, v, vt, per_dir, nb):
    B, Lp, _ = kk.shape
    nblk = Lp // TIME_BLOCK
    ones = _seg_ones(LANES)
    fwd = lambda bi, j: (bi, j, 0)
    bwd = lambda bi, j: (bi, nblk - 1 - j, 0)
    blk = (nb, TIME_BLOCK, RW)
    vt_blk = (nb, 1) + vt.shape[2:]
    (wf, kdf, bf, rbf, ktf, btf), (wb, kdb, bb_, rbb, ktb, btb) = per_dir
    y_shape = jax.ShapeDtypeStruct((B, Lp, RW), F32)
    rows = nb * 4 * N_B
    return pl.pallas_call(
        functools.partial(_scan_kernel, nb=nb),
        out_shape=(y_shape, y_shape),
        grid=(B // nb, nblk),
        in_specs=[pl.BlockSpec(blk, fwd)] * 8 + [pl.BlockSpec(vt_blk, lambda bi, j: (bi, j, 0, 0, 0, 0))]
                 + [pl.BlockSpec(blk, bwd)] * 8 + [pl.BlockSpec(vt_blk, lambda bi, j: (bi, nblk - 1 - j, 0, 0, 0, 0))]
                 + [pl.BlockSpec(ones.shape, lambda bi, j: (0, 0))],
        out_specs=(pl.BlockSpec(blk, fwd), pl.BlockSpec(blk, bwd)),
        scratch_shapes=[pltpu.VMEM((2, rows, LANES), F32),
                        pltpu.VMEM((2, rows, LANES), F32),
                        pltpu.VMEM((2, rows, LANES), F32)],
        compiler_params=_cparams(("parallel", "arbitrary")),
        name="wkv_scan",
    )(kk, wf, kdf, bf, v, rbf, ktf, btf, vt, kk, wb, kdb, bb_, v, rbb, ktb, btb, vt, ones)


def _post_kernel(yf_ref, yb_ref, g_ref, bvg_ref, att_ref, h_ref, gng_ref, gnb_ref, wo_ref, l1g_ref, l1b_ref,
                 wrh_ref, wrl_ref, rb_ref, tri_ref, ones_ref,
                 h1_ref, rf_ref, ri_ref, cnt_ref, *, alpha, tm):
    ones = ones_ref[...]
    y = yf_ref[...] + yb_ref[...]
    mu = _seg_sum(y, ones) * (1.0 / N_B)
    yc = y - mu
    var = _seg_sum(yc * yc, ones) * (1.0 / N_B)
    yr = (yc * lax.rsqrt(var + RWKV_GN_EPS) * gng_ref[...] + gnb_ref[...]) * g_ref[...] + bvg_ref[...]
    half = H_A * V_DIM
    mixed = (jnp.dot(att_ref[...], wo_ref[:half, :], preferred_element_type=F32)
             + jnp.dot(yr.astype(BF16), wo_ref[half:, :], preferred_element_type=F32))
    h1 = _ln(h_ref[...] * alpha + mixed, l1g_ref[...], l1b_ref[...])
    h1_ref[...] = h1

    xh, xl = _split_bf16(h1)
    x = (jnp.dot(xh, wrh_ref[...], preferred_element_type=F32) + jnp.dot(xl, wrh_ref[...], preferred_element_type=F32)
         + jnp.dot(xh, wrl_ref[...], preferred_element_type=F32)) + rb_ref[...]
    lane = lax.broadcasted_iota(jnp.int32, x.shape, 1)
    lanef = lane.astype(F32)
    big = float(LANES)
    gmask = lane < N_GROUPS
    gx = jnp.where(gmask, x, NEG)
    gmax = jnp.max(gx, -1, keepdims=True)
    gidx = jnp.min(jnp.where(gx == gmax, lanef, big), -1, keepdims=True)
    gsum = jnp.sum(jnp.where(gmask, jnp.exp(gx - gmax), 0.0), -1, keepdims=True)
    lo = N_GROUPS + EXPERTS_PER_GROUP * gidx
    emask = (lanef >= lo) & (lanef < lo + EXPERTS_PER_GROUP)
    ex = jnp.where(emask, x, NEG)
    m1 = jnp.max(ex, -1, keepdims=True)
    i1 = jnp.min(jnp.where(emask & (ex == m1), lanef, big), -1, keepdims=True)
    ex2 = jnp.where(lanef == i1, NEG, ex)
    m2 = jnp.max(ex2, -1, keepdims=True)
    i2 = jnp.min(jnp.where(emask & (lanef != i1) & (ex2 == m2), lanef, big), -1, keepdims=True)
    esum = jnp.sum(jnp.where(emask, jnp.exp(ex - m1), 0.0), -1, keepdims=True)
    gp = 1.0 / gsum
    gate0 = gp * (1.0 / esum)
    gate1 = gp * (jnp.exp(m2 - m1) / esum)
    e0 = i1 - N_GROUPS
    e1 = i2 - N_GROUPS

    @pl.when(pl.program_id(0) == 0)
    def _():
        cnt_ref[...] = jnp.zeros_like(cnt_ref)

    onehot = jnp.where((lanef == e0) | (lanef == e1), 1.0, 0.0)
    prefix = jnp.dot(tri_ref[...], onehot.astype(BF16), preferred_element_type=F32) + cnt_ref[...]
    r0 = jnp.sum(jnp.where(lanef == e0, prefix, 0.0), -1, keepdims=True)
    r1 = jnp.sum(jnp.where(lanef == e1, prefix, 0.0), -1, keepdims=True)
    cnt_ref[...] += jnp.sum(onehot, 0, keepdims=True)
    rf_ref[...] = jnp.where(lane == 0, gate0, jnp.where(lane == 1, gate1, 0.0))
    ri = jnp.where(lane == 0, e0, jnp.where(lane == 1, e1, jnp.where(lane == 2, r0, jnp.where(lane == 3, r1, 0.0))))
    ri_ref[...] = ri.astype(jnp.int32)


def _post_mixer(yf, yb, g, bvg, att, h, gn_g, gn_b, w_out, ln_g, ln_b, w_group, b_group, w_expert, b_expert, alpha):
    T, D = h.shape
    tm = _row_tile(T)
    vec = lambda a: a.reshape(1, -1)
    zpad = LANES - N_GROUPS - N_EXPERTS
    wr = jnp.concatenate([w_group, w_expert, jnp.zeros((D, zpad), F32)], axis=1)
    wrh = wr.astype(BF16)
    wrl = (wr - wrh.astype(F32)).astype(BF16)
    rb = jnp.concatenate([b_group, b_expert, jnp.zeros((zpad,), F32)]).reshape(1, LANES)
    tri = jnp.asarray(np.arange(tm)[:, None] > np.arange(tm)[None, :], BF16)
    consts = [vec(gn_g), vec(gn_b), w_out.astype(BF16), vec(ln_g), vec(ln_b), wrh, wrl, rb, tri, _seg_ones(RW)]
    row = lambda w: pl.BlockSpec((tm, w), lambda i: (i, 0))
    full = lambda a: pl.BlockSpec(a.shape, lambda i: (0, 0))
    return pl.pallas_call(
        functools.partial(_post_kernel, alpha=alpha, tm=tm),
        out_shape=(jax.ShapeDtypeStruct((T, D), F32),
                   jax.ShapeDtypeStruct((T, LANES), F32),
                   jax.ShapeDtypeStruct((T, LANES), jnp.int32),
                   jax.ShapeDtypeStruct((1, LANES), F32)),
        grid=(T // tm,),
        in_specs=[row(RW)] * 4 + [row(H_A * V_DIM), row(D)] + [full(c) for c in consts],
        out_specs=(row(D), row(LANES), row(LANES), pl.BlockSpec((1, LANES), lambda i: (0, 0))),
        compiler_params=_cparams(("arbitrary",)),
        name="post_mixer",
    )(yf, yb, g, bvg, att, h, *consts)


def _row_copies(dest_smem, t, make):
    return [make(s, dest_smem[TOP_K * t + s]) for s in range(TOP_K)]


def _load_dest(dest_hbm, dest_smem, isem, tm):
    load = pltpu.make_async_copy(dest_hbm.at[pl.ds(pl.program_id(0) * tm * TOP_K, tm * TOP_K)], dest_smem, isem)
    load.start()
    load.wait()


def _dispatch_kernel(dest_hbm, h_ref, xz_hbm, xb_hbm, dest_smem, isem, sem, *, tm):
    del xz_hbm
    _load_dest(dest_hbm, dest_smem, isem, tm)

    def copies(t):
        src = h_ref.at[pl.ds(t, 1)]
        return _row_copies(dest_smem, t, lambda s, dst: pltpu.make_async_copy(src, xb_hbm.at[pl.ds(dst, 1)], sem))

    def start(t, c):
        for s, cp in enumerate(copies(t)):
            cp.start(priority=s)
        return c

    def wait(t, c):
        for cp in copies(t):
            cp.wait()
        return c

    lax.fori_loop(0, tm, start, 0, unroll=8)
    lax.fori_loop(0, tm, wait, 0, unroll=8)


def _dispatch(dest, h, n_rows):
    T, D = h.shape
    tm = _row_tile(T)
    any_spec = pl.BlockSpec(memory_space=pl.ANY)
    return pl.pallas_call(
        functools.partial(_dispatch_kernel, tm=tm),
        out_shape=jax.ShapeDtypeStruct((n_rows, D), F32),
        grid=(T // tm,),
        in_specs=[any_spec, pl.BlockSpec((tm, D), lambda i: (i, 0)), any_spec],
        out_specs=any_spec,
        scratch_shapes=[pltpu.SMEM((tm * TOP_K,), jnp.int32), pltpu.SemaphoreType.DMA, pltpu.SemaphoreType.DMA],
        input_output_aliases={2: 0},
        compiler_params=_cparams(("arbitrary",)),
        name="moe_dispatch",
    )(dest, h, jnp.zeros((n_rows, D), F32))


def _expert_kernel(be_ref, nu_ref, x_ref, w1_ref, w3_ref, w2_ref, o_ref):
    @pl.when(pl.program_id(0) < nu_ref[0])
    def _():
        x = x_ref[...].astype(BF16)
        a = jnp.dot(x, w1_ref[0], preferred_element_type=F32)
        b = jnp.dot(x, w3_ref[0], preferred_element_type=F32)
        hid = (a * jax.nn.sigmoid(a)) * b
        o_ref[...] = jnp.dot(hid.astype(BF16), w2_ref[0], preferred_element_type=F32)


def _expert_ffn(xb, block_e, n_used, w1, w3, w2):
    n_rows, D = xb.shape
    n_blocks = n_rows // MOE_BLOCK
    blk = lambda i, be, nu: jnp.minimum(i, nu[0] - 1)
    wmap = lambda i, be, nu: (be[blk(i, be, nu)], 0, 0)
    grid_spec = pltpu.PrefetchScalarGridSpec(
        num_scalar_prefetch=2,
        grid=(n_blocks,),
        in_specs=[pl.BlockSpec((MOE_BLOCK, D), lambda i, be, nu: (blk(i, be, nu), 0)),
                  pl.BlockSpec((1, D, E_HID), wmap),
                  pl.BlockSpec((1, D, E_HID), wmap),
                  pl.BlockSpec((1, E_HID, D), wmap)],
        out_specs=pl.BlockSpec((MOE_BLOCK, D), lambda i, be, nu: (blk(i, be, nu), 0)),
    )
    return pl.pallas_call(
        _expert_kernel,
        out_shape=jax.ShapeDtypeStruct((n_rows, D), F32),
        grid_spec=grid_spec,
        compiler_params=_cparams(("arbitrary",)),
        name="moe_experts",
    )(block_e, n_used, xb, w1, w3, w2)


def _combine_kernel(dest_hbm, yb_hbm, gate_ref, h_ref, g_ref, b_ref, o_ref, dest_smem, buf, isem, sem, *, tm, alpha):
    _load_dest(dest_hbm, dest_smem, isem, tm)

    def copies(t):
        return _row_copies(dest_smem, t, lambda s, src: pltpu.make_async_copy(
            yb_hbm.at[pl.ds(src, 1)], buf.at[s, pl.ds(t, 1)], sem))

    def start(t, c):
        for s, cp in enumerate(copies(t)):
            cp.start(priority=s)
        return c

    def wait(t, c):
        for cp in copies(t):
            cp.wait()
        return c

    lax.fori_loop(0, tm, start, 0, unroll=8)
    lax.fori_loop(0, tm, wait, 0, unroll=8)
    gate = gate_ref[...]
    ff = sum(buf[s] * gate[:, s:s + 1] for s in range(TOP_K))
    o_ref[...] = _ln(h_ref[...] * alpha + ff, g_ref[...], b_ref[...])


def _combine(dest, yb, route_f, h, ln_g, ln_b, alpha):
    T, D = h.shape
    tm = _row_tile(T)
    any_spec = pl.BlockSpec(memory_space=pl.ANY)
    row = lambda w: pl.BlockSpec((tm, w), lambda i: (i, 0))
    vec = pl.BlockSpec((1, D), lambda i: (0, 0))
    return pl.pallas_call(
        functools.partial(_combine_kernel, tm=tm, alpha=alpha),
        out_shape=jax.ShapeDtypeStruct((T, D), F32),
        grid=(T // tm,),
        in_specs=[any_spec, any_spec, row(LANES), row(D), vec, vec],
        out_specs=row(D),
        scratch_shapes=[pltpu.SMEM((tm * TOP_K,), jnp.int32), pltpu.VMEM((TOP_K, tm, D), F32),
                        pltpu.SemaphoreType.DMA, pltpu.SemaphoreType.DMA],
        compiler_params=_cparams(("arbitrary",)),
        name="moe_combine",
    )(dest, yb, route_f, h, ln_g.reshape(1, D), ln_b.reshape(1, D))


def _hier_moe(h, route_f, route_i, counts, w1, w3, w2, ln_g, ln_b, alpha):
    T, D = h.shape
    counts = counts[0, :N_EXPERTS].astype(jnp.int32)
    padded = (counts + MOE_BLOCK - 1) // MOE_BLOCK * MOE_BLOCK
    pends = jnp.cumsum(padded)
    pstart = pends - padded
    n_rows = (T * TOP_K + N_EXPERTS * (MOE_BLOCK - 1) + MOE_BLOCK - 1) // MOE_BLOCK * MOE_BLOCK
    n_blocks = n_rows // MOE_BLOCK
    block_start = jnp.arange(n_blocks, dtype=jnp.int32) * MOE_BLOCK
    block_e = jnp.minimum(jnp.sum(pends[None, :] <= block_start[:, None], axis=1), N_EXPERTS - 1).astype(jnp.int32)
    n_used = (pends[-1:] // MOE_BLOCK).astype(jnp.int32)
    dest = (jnp.take(pstart, route_i[:, :TOP_K]) + route_i[:, TOP_K:2 * TOP_K]).reshape(-1)
    xb = _dispatch(dest, h, n_rows)
    yb = _expert_ffn(xb, block_e, n_used, w1, w3, w2)
    return _combine(dest, yb, route_f, h, ln_g, ln_b, alpha)


def kernel(x, positions, meta_tokens, emb_ln_g, emb_ln_b, w_in, mla_q_norm, mla_kv_norm, mla_w_uq, mla_w_ukv, mla_out_norm, rwkv_mu_prev, rwkv_mu_next, rwkv_w0, rwkv_w2, rwkv_a0, rwkv_a2, rwkv_g2, rwkv_k_k, rwkv_k_a, rwkv_r_k, rwkv_gn_g, rwkv_gn_b, rwkv_v0, rwkv_v1, rwkv_v2, w_out, ln1_g, ln1_b, moe_w_group, moe_b_group, moe_w_expert, moe_b_expert, moe_w1, moe_w3, moe_w2, ln2_g, ln2_b):
    B, seq, D = x.shape
    depth = w_in.shape[0]
    alpha = (2 * depth) ** 0.25
    L = seq + N_META
    Lp = -(-L // LANES) * LANES
    T = B * Lp
    nb = next(n for n in (4, 2, 1) if B % n == 0)
    tk = 384 if Lp % 384 == 0 else LANES
    tq = next(t for t in (1408, 1152, 768, 384, LANES) if Lp % t == 0)

    meta = jnp.broadcast_to(meta_tokens.astype(x.dtype)[None], (B, N_META, D))
    h = jnp.concatenate([meta, x, jnp.zeros((B, Lp - L, D), x.dtype)], axis=1).reshape(T, D)
    h = _ln_residual(h, jnp.zeros_like(h), emb_ln_g, emb_ln_b, 1.0)

    key_bias = jnp.where(jnp.arange(Lp) < L, 0.0, NEG).astype(F32)[None, :]
    pos = jnp.concatenate([jnp.broadcast_to(jnp.arange(N_META, dtype=jnp.int32), (B, N_META)),
                           positions + N_META, jnp.zeros((B, Lp - L), jnp.int32)], axis=1)
    inv_freq = ROPE_THETA ** (-jnp.arange(0, QK_ROPE, 2, dtype=F32) / QK_ROPE)
    ang = pos.astype(F32)[..., None] * inv_freq
    cos, sin = jnp.cos(ang).reshape(T, -1), jnp.sin(ang).reshape(T, -1)
    zeros = jnp.zeros((T, LANES - QK_ROPE), F32)
    rope_cos = jnp.concatenate([cos, cos, zeros], axis=1)
    rope_sin = jnp.concatenate([-sin, sin, zeros], axis=1)
    qscale = (QK_NOPE + QK_ROPE) ** -0.5 * math.log2(math.e)

    v_first = None
    for li in range(depth):
        q, k, v, rw = _project(h, w_in[li], mla_q_norm[li], mla_kv_norm[li], mla_w_uq[li], mla_w_ukv[li],
                               rope_cos, rope_sin, qscale)
        y_att = _attention(q.reshape(B, Lp, -1), k.reshape(B, Lp, -1), v.reshape(B, Lp, -1),
                           key_bias, mla_out_norm[li], tq, tk)
        vres = None if li == 0 else (rwkv_v0[li - 1], rwkv_v1[li - 1], rwkv_v2[li - 1])
        outs = _rwkv_prep(rw.reshape(B, Lp, -1), L, rwkv_mu_prev[li], rwkv_mu_next[li], rwkv_w0[li], rwkv_w2[li],
                          rwkv_a0[li], rwkv_a2[li], rwkv_g2[li], rwkv_k_k[li], rwkv_k_a[li], rwkv_r_k[li],
                          v_first, vres)
        kk, vm, g, bvg, vt = outs[0], outs[1], outs[14], outs[15], outs[16]
        if li == 0:
            v_first = outs[17]
        y_f, y_b = _wkv_bidir(kk, vm, vt, (outs[2:8], outs[8:14]), nb)
        flat = lambda a: a.reshape(T, -1)
        h, route_f, route_i, counts = _post_mixer(
            flat(y_f), flat(y_b), flat(g), flat(bvg), flat(y_att), h, rwkv_gn_g[li], rwkv_gn_b[li], w_out[li],
            ln1_g[li], ln1_b[li], moe_w_group[li], moe_b_group[li], moe_w_expert[li], moe_b_expert[li], alpha)
        h = _hier_moe(h, route_f, route_i, counts,
                      moe_w1[li].astype(BF16), moe_w3[li].astype(BF16), moe_w2[li].astype(BF16),
                      ln2_g[li], ln2_b[li], alpha)
    return h.reshape(B, Lp, D)[:, N_META:L]
```
